```python
import jax, jax.numpy as jnp
from jax import lax
import numpy as np

D_MODEL = 1024
BATCH = 32
SEQ = 256
DEPTH = 1
DEC_BATCH = 4
DEC_SEQ = 4096
PAST_LEN = 256

GRID_W = 64
CONV_WIDTH = D_MODEL // 2
MLSTM_HEADS = 4
MLSTM_DH = (D_MODEL // 2) // MLSTM_HEADS
MLSTM_W = MLSTM_HEADS * MLSTM_DH
MLSTM_CHUNK = 64
N_DIRS = 2
IN_COLS = 3 * CONV_WIDTH + 4 * MLSTM_W + 2 * N_DIRS * MLSTM_HEADS
PEER_HEADS = 8
PEER_DQ = 256
PEER_NKEYS = 128
PEER_N = PEER_NKEYS * PEER_NKEYS
PEER_TOPK = 16
PEER_BLOCK = 128
EPS = 1e-6

kernel_name = 'hymba_shortconv_bimlstm_peer_dit_step'


def rms_norm(x, g):
    xf = x.astype(jnp.float32)
    y = xf * lax.rsqrt(jnp.mean(xf * xf, axis=-1, keepdims=True) + EPS)
    return (y * g.astype(jnp.float32)).astype(x.dtype)


def ada_mod(cvec, w_ada, b_ada):
    m = jax.nn.silu(cvec) @ w_ada + b_ada
    return jnp.split(m[:, None, :], 6, axis=-1)


def conv3(u, w, b):
    up = jnp.pad(u, ((0, 0), (1, 1), (0, 0)))
    return up[:, :-2] * w[0] + up[:, 1:-1] * w[1] + up[:, 2:] * w[2] + b


def short_conv_group(bg, cg, xg, conv_w, conv_b, on_grid):
    u = cg * xg
    if on_grid:
        n, L, ch = u.shape
        rows = L // GRID_W
        y = conv3(u.reshape(n * rows, GRID_W, ch), conv_w, conv_b).reshape(n, L, ch)
    else:
        y = conv3(u, conv_w, conv_b)
    return bg * y


def mlstm_chunkwise(q, k, v, ig, lf, C0, n0, m0):
    Bn, H, L, Dh = q.shape
    T = MLSTM_CHUNK
    nc = L // T
    tril = jnp.tril(jnp.ones((T, T), dtype=bool))

    def to_chunks(a):
        return jnp.moveaxis(a.reshape((Bn, H, nc, T) + a.shape[3:]), 2, 0)

    def step(carry, inp):
        C, n, m = carry
        qc, kc, vc, ic, fc = inp
        b = jnp.cumsum(fc, axis=-1)
        dmat = jnp.where(tril, b[..., :, None] - b[..., None, :] + ic[..., None, :], -jnp.inf)
        inter = b + m[..., None]
        m_t = jnp.maximum(inter, jnp.max(dmat, axis=-1))
        s = jnp.einsum('bhtd,bhsd->bhts', qc, kc) * jnp.exp(dmat - m_t[..., None])
        a = jnp.exp(inter - m_t)
        num = a[..., None] * jnp.einsum('bhtk,bhkv->bhtv', qc, C) + jnp.einsum('bhts,bhsv->bhtv', s, vc)
        den = a * jnp.einsum('bhtk,bhk->bht', qc, n) + jnp.sum(s, axis=-1)
        h = num / jnp.maximum(jnp.abs(den), jnp.exp(-m_t))[..., None]
        m_new = m_t[..., -1]
        w_s = jnp.exp(b[..., -1:] - b + ic - m_new[..., None])
        decay = jnp.exp(b[..., -1] + m - m_new)
        C_new = decay[..., None, None] * C + jnp.einsum('bhs,bhsk,bhsv->bhkv', w_s, kc, vc)
        n_new = decay[..., None] * n + jnp.einsum('bhs,bhsk->bhk', w_s, kc)
        return (C_new, n_new, m_new), h

    xs = (to_chunks(q), to_chunks(k), to_chunks(v), to_chunks(ig), to_chunks(lf))
    (C, n, m), h = lax.scan(step, (C0, n0, m0), xs)
    h = jnp.moveaxis(h, 0, 2).reshape(Bn, H, L, Dh)
    return h, C, n, m


def mlstm_group(qg, kg, vg, og, igr, fgr, b_igate, b_fgate, g_head, C0, n0, m0):
    Bn, L, _ = qg.shape
    f32 = jnp.float32

    def heads(a):
        return a.reshape(Bn, L, MLSTM_HEADS, MLSTM_DH).transpose(0, 2, 1, 3).astype(f32)

    q = heads(qg) * (MLSTM_DH ** -0.5)
    k = heads(kg)
    v = heads(vg)
    ig = (igr.reshape(Bn, L, N_DIRS, MLSTM_HEADS).astype(f32) + b_igate.astype(f32)).transpose(2, 0, 3, 1)
    lf = jax.nn.log_sigmoid(fgr.reshape(Bn, L, N_DIRS, MLSTM_HEADS).astype(f32) + b_fgate.astype(f32)).transpose(2, 0, 3, 1)
    C0 = C0.astype(f32); n0 = n0.astype(f32); m0 = m0.astype(f32)

    def flip(a):
        return jnp.flip(a, axis=2)

    h_f, Cf, nf, mf = mlstm_chunkwise(q, k, v, ig[0], lf[0], C0[:, 0], n0[:, 0], m0[:, 0])
    h_b, Cb, nb, mb = mlstm_chunkwise(flip(q), flip(k), flip(v), flip(ig[1]), flip(lf[1]),
                                      C0[:, 1], n0[:, 1], m0[:, 1])
    h = h_f + flip(h_b)
    h = h * lax.rsqrt(jnp.mean(h * h, axis=-1, keepdims=True) + EPS) * g_head.astype(f32)[None, :, None, :]
    h = h.transpose(0, 2, 1, 3).reshape(Bn, L, MLSTM_W).astype(qg.dtype) * jax.nn.sigmoid(og)
    return h, jnp.stack([Cf, Cb], axis=1), jnp.stack([nf, nb], axis=1), jnp.stack([mf, mb], axis=1)


def token_mixers(h, w_in, b_igate, b_fgate, conv_w, conv_b, g_mlstm, w_out, C0, n0, m0, on_grid):
    proj = h @ w_in
    cw, mw, g = CONV_WIDTH, MLSTM_W, N_DIRS * MLSTM_HEADS
    cuts = [cw, 2 * cw, 3 * cw, 3 * cw + mw, 3 * cw + 2 * mw, 3 * cw + 3 * mw, 3 * cw + 4 * mw, 3 * cw + 4 * mw + g]
    bg, cg, xg, qg, kg, vg, og, igr, fgr = jnp.split(proj, cuts, axis=-1)
    conv_out = short_conv_group(bg, cg, xg, conv_w, conv_b, on_grid)
    ml_out, Cn, nn_, mn = mlstm_group(qg, kg, vg, og, igr, fgr, b_igate, b_fgate, g_mlstm, C0, n0, m0)
    out = jnp.concatenate([conv_out, ml_out], axis=-1) @ w_out
    return out, Cn, nn_, mn


def peer(h, wq, k1, k2, u_tab, v_tab):
    Bn, L, D = h.shape
    half = PEER_DQ // 2
    toks = h.reshape(-1, PEER_BLOCK, D)

    def block(xb):
        T = xb.shape[0]
        q = (xb @ wq).reshape(T, PEER_HEADS, PEER_DQ)
        s1 = jnp.einsum('thd,kd->thk', q[..., :half], k1).astype(jnp.float32)
        s2 = jnp.einsum('thd,kd->thk', q[..., half:], k2).astype(jnp.float32)
        v1, i1 = lax.top_k(s1, PEER_TOPK)
        v2, i2 = lax.top_k(s2, PEER_TOPK)
        cand = (v1[..., :, None] + v2[..., None, :]).reshape(T, PEER_HEADS, PEER_TOPK * PEER_TOPK)
        vs, flat = lax.top_k(cand, PEER_TOPK)
        e = (jnp.take_along_axis(i1, flat // PEER_TOPK, axis=-1) * PEER_NKEYS
             + jnp.take_along_axis(i2, flat % PEER_TOPK, axis=-1))
        gate = jax.nn.softmax(vs, axis=-1)
        a = jnp.einsum('td,thkd->thk', xb, u_tab[e]).astype(jnp.float32)
        w = (gate * jax.nn.gelu(a, approximate=False)).astype(xb.dtype)
        return jnp.einsum('thk,thkd->td', w, v_tab[e])

    return lax.map(block, toks).reshape(Bn, L, D)


def trunk_layer(x, cvec, w_ada, b_ada, g_norm1, w_in, b_igate, b_fgate, conv_w, conv_b, g_mlstm, w_out,
                g_norm2, peer_wq, peer_k1, peer_k2, peer_u, peer_v, C0, n0, m0, on_grid):
    sh1, sc1, ga1, sh2, sc2, ga2 = ada_mod(cvec, w_ada, b_ada)
    h = rms_norm(x, g_norm1) * (1 + sc1) + sh1
    mix, Cn, nn_, mn = token_mixers(h, w_in, b_igate, b_fgate, conv_w, conv_b, g_mlstm, w_out, C0, n0, m0, on_grid)
    x = x + ga1 * mix
    h = rms_norm(x, g_norm2) * (1 + sc2) + sh2
    x = x + ga2 * peer(h, peer_wq, peer_k1, peer_k2, peer_u, peer_v)
    return x, Cn, nn_, mn


def setup_inputs(seed: int = 0) -> dict:
    key = jax.random.key(seed)
    ks = jax.random.split(key, 24)
    nrm = jax.random.normal
    D = D_MODEL
    f = jnp.float32
    return {
        'x_prompt': nrm(ks[0], (BATCH, SEQ, D), f),
        'x_sample': nrm(ks[1], (DEC_BATCH, DEC_SEQ, D), f),
        'state_C': 0.5 * nrm(ks[2], (DEC_BATCH, DEPTH, N_DIRS, MLSTM_HEADS, MLSTM_DH, MLSTM_DH), f),
        'state_n': 0.5 * nrm(ks[3], (DEC_BATCH, DEPTH, N_DIRS, MLSTM_HEADS, MLSTM_DH), f),
        'state_m': nrm(ks[4], (DEC_BATCH, DEPTH, N_DIRS, MLSTM_HEADS), f),
        'c': nrm(ks[5], (DEC_BATCH, D), f),
        'c_ctx': nrm(ks[6], (D,), f),
        'w_ada': 0.5 * D ** -0.5 * nrm(ks[7], (DEPTH, D, 6 * D), f),
        'b_ada': 0.01 * nrm(ks[8], (DEPTH, 6 * D), f),
        'g_norm1': 1.0 + 0.01 * nrm(ks[9], (DEPTH, D), f),
        'w_in': D ** -0.5 * nrm(ks[10], (DEPTH, D, IN_COLS), f),
        'b_igate': 0.1 * nrm(ks[11], (DEPTH, N_DIRS, MLSTM_HEADS), f),
        'b_fgate': 3.0 + 0.1 * nrm(ks[12], (DEPTH, N_DIRS, MLSTM_HEADS), f),
        'conv_w': 3.0 ** -0.5 * nrm(ks[13], (DEPTH, 3, CONV_WIDTH), f),
        'conv_b': 0.01 * nrm(ks[14], (DEPTH, CONV_WIDTH), f),
        'g_mlstm': 1.0 + 0.01 * nrm(ks[15], (DEPTH, MLSTM_HEADS, MLSTM_DH), f),
        'w_out': D ** -0.5 * nrm(ks[16], (DEPTH, D, D), f),
        'g_norm2': 1.0 + 0.01 * nrm(ks[17], (DEPTH, D), f),
        'peer_wq': D ** -0.5 * nrm(ks[18], (DEPTH, D, PEER_HEADS * PEER_DQ), f),
        'peer_k1': (PEER_DQ // 2) ** -0.5 * nrm(ks[19], (DEPTH, PEER_NKEYS, PEER_DQ // 2), f),
        'peer_k2': (PEER_DQ // 2) ** -0.5 * nrm(ks[20], (DEPTH, PEER_NKEYS, PEER_DQ // 2), f),
        'peer_u': D ** -0.5 * nrm(ks[21], (DEPTH, PEER_N, D), f),
        'peer_v': 0.3 * nrm(ks[22], (DEPTH, PEER_N, D), f),
        'g_final': 1.0 + 0.01 * nrm(ks[23], (D,), f),
    }


def reference(x_prompt, x_sample, state_C, state_n, state_m, c, c_ctx, w_ada, b_ada, g_norm1, w_in,
              b_igate, b_fgate, conv_w, conv_b, g_mlstm, w_out, g_norm2, peer_wq, peer_k1, peer_k2,
              peer_u, peer_v, g_final):
    f32 = jnp.float32
    B = x_prompt.shape[0]
    zC = jnp.zeros((B, N_DIRS, MLSTM_HEADS, MLSTM_DH, MLSTM_DH), f32)
    zn = jnp.zeros((B, N_DIRS, MLSTM_HEADS, MLSTM_DH), f32)
    zm = jnp.zeros((B, N_DIRS, MLSTM_HEADS), f32)
    cctx = c_ctx[None, :]
    xp, xs = x_prompt, x_sample
    Cs, ns, ms = [], [], []
    for l in range(DEPTH):
        lw = (w_ada[l], b_ada[l], g_norm1[l], w_in[l], b_igate[l], b_fgate[l], conv_w[l], conv_b[l],
              g_mlstm[l], w_out[l], g_norm2[l], peer_wq[l], peer_k1[l], peer_k2[l], peer_u[l], peer_v[l])
        xp, Cn, nn_, mn = trunk_layer(xp, cctx, *lw, zC, zn, zm, False)
        Cs.append(Cn); ns.append(nn_); ms.append(mn)
        xs, _, _, _ = trunk_layer(xs, c, *lw, state_C[:, l], state_n[:, l], state_m[:, l], True)
    y_prompt = rms_norm(xp, g_final)
    y_sample = rms_norm(xs, g_final)
    new_C = jnp.stack(Cs, axis=1)
    new_n = jnp.stack(ns, axis=1)
    new_m = jnp.stack(ms, axis=1)
    return (y_prompt, y_sample, new_C, new_n, new_m)
```

```python
import functools

import numpy as np
import jax
import jax.numpy as jnp
from jax import lax
from jax.experimental import pallas as pl
from jax.experimental.pallas import tpu as pltpu

F32 = jnp.float32
BF16 = jnp.bfloat16

D_MODEL = 1024
CONV_W = 512
N_HEADS = 4
DH = 128
MLSTM_W = N_HEADS * DH
N_DIRS = 2
GRID_W = 64
N_GROUPS = 7
GATE_COL0 = N_GROUPS * 512
N_GATES = 2 * N_DIRS * N_HEADS
PEER_HEADS = 8
NKEYS = 128
TOPK = 16
N_EXPERTS = NKEYS * NKEYS
EPS = 1e-6
LANES = 128
NEG_INF = float("-inf")

VMEM_LIMIT = 56 * 1024 * 1024


def _cparams(sem):
    return pltpu.CompilerParams(dimension_semantics=sem, vmem_limit_bytes=VMEM_LIMIT)


def _mod_kernel(cv_ref, w_ref, b_ref, o_ref):
    cv = cv_ref[...]
    s = cv * (1.0 / (1.0 + jnp.exp(-cv)))
    o_ref[...] = jnp.dot(s, w_ref[...], preferred_element_type=F32) + b_ref[...]


def _modulation(cv8, w_ada, b_ada):
    n = w_ada.shape[1]
    tn = 1024
    return pl.pallas_call(
        _mod_kernel,
        grid=(n // tn,),
        in_specs=[pl.BlockSpec((8, D_MODEL), lambda j: (0, 0)),
                  pl.BlockSpec((D_MODEL, tn), lambda j: (0, j)),
                  pl.BlockSpec((1, tn), lambda j: (0, j))],
        out_specs=pl.BlockSpec((8, tn), lambda j: (0, j)),
        out_shape=jax.ShapeDtypeStruct((8, n), F32),
        compiler_params=_cparams(("arbitrary",)),
        name="modulation",
    )(cv8, w_ada, b_ada.reshape(1, n))


def _mod_row(i, tm, n_ctx_tok, lat_len):
    n_ctx_tiles = n_ctx_tok // tm
    tiles_per_seq = lat_len // tm
    is_ctx = i < n_ctx_tiles
    row = jnp.where(is_ctx, 0, 1 + (i - n_ctx_tiles) // tiles_per_seq)
    return is_ctx, row


def _rms(x):
    return x * lax.rsqrt(jnp.mean(x * x, axis=-1, keepdims=True) + EPS)


def _log_sigmoid(z):
    return jnp.minimum(z, 0.0) - jnp.log(1.0 + jnp.exp(-jnp.abs(z)))


def _inproj_kernel(x_ref, mod_ref, g1_ref, win_ref, wg_ref, wgt_ref, bg_ref, bgt_ref, cw_ref, cb_ref,
                   conv_ref, q_ref, k_ref, v_ref, o_ref, gates_ref, gatest_ref,
                   *, tm, n_ctx_tok, ctx_len, lat_len):
    i = pl.program_id(0)
    is_ctx, row = _mod_row(i, tm, n_ctx_tok, lat_len)
    sh1 = mod_ref[pl.ds(row, 1), 0:D_MODEL]
    sc1 = mod_ref[pl.ds(row, 1), D_MODEL:2 * D_MODEL]
    h = _rms(x_ref[...]) * g1_ref[...] * (1.0 + sc1) + sh1
    hb = h.astype(BF16)

    def grp(g):
        return jnp.dot(hb, win_ref[:, g * 512:(g + 1) * 512], preferred_element_type=F32)

    u = grp(1) * grp(2)
    pm = jnp.where(is_ctx, ctx_len - 1, GRID_W - 1)
    pos = lax.broadcasted_iota(jnp.int32, (tm, 1), 0) & pm
    u_prev = jnp.where(pos == 0, 0.0, pltpu.roll(u, 1, axis=0))
    u_next = jnp.where(pos == pm, 0.0, pltpu.roll(u, tm - 1, axis=0))
    y = u_prev * cw_ref[0:1, :] + u * cw_ref[1:2, :] + u_next * cw_ref[2:3, :] + cb_ref[...]
    conv_ref[...] = (grp(0) * y).astype(BF16)

    q_ref[...] = (grp(3) * (DH ** -0.5)).astype(BF16)
    k_ref[...] = grp(4).astype(BF16)
    v_ref[...] = grp(5).astype(BF16)
    o_ref[...] = grp(6)

    z = jnp.dot(hb, wg_ref[...], preferred_element_type=F32) + bg_ref[...]
    lane = lax.broadcasted_iota(jnp.int32, z.shape, 1)
    gates_ref[...] = jnp.where(lane >= N_DIRS * N_HEADS, _log_sigmoid(z), z)
    zt = lax.dot_general(wgt_ref[...], hb, (((1,), (1,)), ((), ())), preferred_element_type=F32) + bgt_ref[...]
    sub = lax.broadcasted_iota(jnp.int32, zt.shape, 0)
    gatest_ref[...] = jnp.where(sub >= N_DIRS * N_HEADS, _log_sigmoid(zt), zt)


def _in_proj(x_all, mod, g1, w_in, b_igate, b_fgate, conv_w, conv_b, *, tm, n_ctx_tok, ctx_len, lat_len):
    T = x_all.shape[0]
    win = w_in[:, :GATE_COL0].astype(BF16)
    wg = jnp.zeros((D_MODEL, LANES), F32).at[:, :N_GATES].set(w_in[:, GATE_COL0:]).astype(BF16)
    bgate = jnp.zeros((1, LANES), F32).at[0, :N_GATES].set(
        jnp.concatenate([b_igate.reshape(-1), b_fgate.reshape(-1)]))
    tok = lambda w: pl.BlockSpec((tm, w), lambda i: (i, 0))
    full = lambda a: pl.BlockSpec(a.shape, lambda i: (0,) * a.ndim)
    args = (x_all, mod, g1.reshape(1, -1), win, wg, wg.T, bgate, bgate.T, conv_w, conv_b.reshape(1, -1))
    return pl.pallas_call(
        functools.partial(_inproj_kernel, tm=tm, n_ctx_tok=n_ctx_tok, ctx_len=ctx_len, lat_len=lat_len),
        grid=(T // tm,),
        in_specs=[tok(D_MODEL)] + [full(a) for a in args[1:]],
        out_specs=[tok(CONV_W), tok(MLSTM_W), tok(MLSTM_W), tok(MLSTM_W), tok(MLSTM_W), tok(LANES),
                   pl.BlockSpec((LANES, tm), lambda i: (0, i))],
        out_shape=[jax.ShapeDtypeStruct((T, CONV_W), BF16),
                   jax.ShapeDtypeStruct((T, MLSTM_W), BF16),
                   jax.ShapeDtypeStruct((T, MLSTM_W), BF16),
                   jax.ShapeDtypeStruct((T, MLSTM_W), BF16),
                   jax.ShapeDtypeStruct((T, MLSTM_W), F32),
                   jax.ShapeDtypeStruct((T, LANES), F32),
                   jax.ShapeDtypeStruct((LANES, T), F32)],
        compiler_params=_cparams(("arbitrary",)),
        name="in_proj",
    )(*args)


def _split_bf16(x):
    hi = x.astype(BF16)
    return hi, (x - hi.astype(F32)).astype(BF16)


def _mlstm_kernel(qf_ref, kf_ref, vf_ref, gf_ref, gtf_ref, qb_ref, kb_ref, vb_ref, gb_ref, gtb_ref,
                  c0_ref, m0_ref, hf_ref, hb_ref, *rest, tc, write_state):
    if write_state:
        cout_ref, mout_ref, c_scr, m_scr = rest
    else:
        c_scr, m_scr = rest
    c = pl.program_id(1)

    @pl.when(c == 0)
    def _():
        c_scr[...] = c0_ref[...]
        m_scr[...] = m0_ref[...]

    r_io = lax.broadcasted_iota(jnp.int32, (tc, tc), 0)
    c_io = lax.broadcasted_iota(jnp.int32, (tc, tc), 1)
    ones = jnp.ones((tc, DH), BF16)

    for d in range(N_DIRS):
        q_ref, k_ref, v_ref, g_ref, gt_ref, h_ref = (
            (qf_ref, kf_ref, vf_ref, gf_ref, gtf_ref, hf_ref) if d == 0 else
            (qb_ref, kb_ref, vb_ref, gb_ref, gtb_ref, hb_ref))
        causal = (c_io <= r_io) if d == 0 else (c_io >= r_io)
        last = tc - 1 if d == 0 else 0
        tri = causal.astype(BF16)
        tri_t = ((r_io <= c_io) if d == 0 else (r_io >= c_io)).astype(BF16)
        g = g_ref[...]
        gt = gt_ref[...]
        g_hi, g_lo = _split_bf16(g)
        gt_hi, gt_lo = _split_bf16(gt)
        bcum = (jnp.dot(tri, g_hi, preferred_element_type=F32) + jnp.dot(tri, g_lo, preferred_element_type=F32))
        bcum_t = (jnp.dot(gt_hi, tri_t, preferred_element_type=F32)
                  + jnp.dot(gt_lo, tri_t, preferred_element_type=F32))
        for hd in range(N_HEADS):
            gi = d * N_HEADS + hd
            gfi = N_DIRS * N_HEADS + gi
            sl = slice(hd * DH, (hd + 1) * DH)
            q = q_ref[:, sl]
            k = k_ref[:, sl]
            v = v_ref[:, sl]
            ig_c = g[:, gi:gi + 1]
            ig_r = gt[gi:gi + 1, :]
            b_c = bcum[:, gfi:gfi + 1]
            b_r = bcum_t[gfi:gfi + 1, :]
            m_prev = m_scr[gi:gi + 1, 0:1]
            dmat = jnp.where(causal, b_c - b_r + ig_r, NEG_INF)
            inter = b_c + m_prev
            m_t = jnp.maximum(inter, jnp.max(dmat, axis=-1, keepdims=True))
            s = lax.dot_general(q, k, (((1,), (1,)), ((), ())), preferred_element_type=F32) * jnp.exp(dmat - m_t)
            a = jnp.exp(inter - m_t)
            cext = c_scr[gi]
            vext = jnp.concatenate([v, ones], axis=1)
            nd = (a * jnp.dot(q, cext.astype(BF16), preferred_element_type=F32)
                  + jnp.dot(s.astype(BF16), vext, preferred_element_type=F32))
            num = nd[:, :DH]
            den = nd[:, DH:]
            h_ref[:, sl] = num / jnp.maximum(jnp.abs(den), jnp.exp(-m_t))
            m_new = m_t[last:last + 1, :]
            b_last = b_c[last:last + 1, :]
            w_c = jnp.exp(b_last - b_c + ig_c - m_new)
            decay = jnp.exp(b_last + m_prev - m_new)
            kw_t = (k.astype(F32) * w_c).T.astype(BF16)
            c_scr[gi] = decay * cext + jnp.dot(kw_t, vext, preferred_element_type=F32)
            m_scr[gi:gi + 1, :] = jnp.broadcast_to(m_new, (1, LANES))

    if write_state:
        @pl.when(c == pl.num_programs(1) - 1)
        def _():
            cout_ref[...] = c_scr[...]
            mout_ref[...] = m_scr[...]


def _mlstm(q, k, v, gates, gates_t, c0ext, m0, *, tok0, nseq, seq_len, tc, write_state):
    nc = seq_len // tc
    blk0 = tok0 // tc
    fwd = lambda b, c: b * nc + c
    bwd = lambda b, c: b * nc + nc - 1 - c
    tokspec = lambda w, f, o: pl.BlockSpec((tc, w), lambda b, c: (o + f(b, c), 0))
    gtspec = lambda f: pl.BlockSpec((LANES, tc), lambda b, c: (0, blk0 + f(b, c)))
    ng = N_DIRS * N_HEADS
    in_specs = [tokspec(MLSTM_W, fwd, blk0)] * 3 + [tokspec(LANES, fwd, blk0), gtspec(fwd)] \
        + [tokspec(MLSTM_W, bwd, blk0)] * 3 + [tokspec(LANES, bwd, blk0), gtspec(bwd)] \
        + [pl.BlockSpec((None, ng, DH, 2 * DH), lambda b, c: (b, 0, 0, 0)),
           pl.BlockSpec((None, ng, LANES), lambda b, c: (b, 0, 0))]
    out_specs = [tokspec(MLSTM_W, fwd, 0), tokspec(MLSTM_W, bwd, 0)]
    out_shape = [jax.ShapeDtypeStruct((nseq * seq_len, MLSTM_W), F32)] * 2
    if write_state:
        out_specs += [pl.BlockSpec((None, ng, DH, 2 * DH), lambda b, c: (b, 0, 0, 0)),
                      pl.BlockSpec((None, ng, LANES), lambda b, c: (b, 0, 0))]
        out_shape += [jax.ShapeDtypeStruct((nseq, ng, DH, 2 * DH), F32),
                      jax.ShapeDtypeStruct((nseq, ng, LANES), F32)]
    return pl.pallas_call(
        functools.partial(_mlstm_kernel, tc=tc, write_state=write_state),
        grid=(nseq, nc),
        in_specs=in_specs,
        out_specs=out_specs,
        out_shape=out_shape,
        scratch_shapes=[pltpu.VMEM((ng, DH, 2 * DH), F32), pltpu.VMEM((ng, LANES), F32)],
        compiler_params=_cparams(("arbitrary", "arbitrary")),
        name="mlstm_ctx" if write_state else "mlstm_lat",
    )(q, k, v, gates, gates_t, q, k, v, gates, gates_t, c0ext, m0)


def _mix_kernel(x_ref, conv_ref, hf_ref, hb_ref, og_ref, mod_ref, gm_ref, wout_ref, g2_ref, wq_ref,
                k1_ref, k2_ref, x1_ref, h2_ref, sc_ref, *, tm, n_ctx_tok, lat_len):
    i = pl.program_id(0)
    _, row = _mod_row(i, tm, n_ctx_tok, lat_len)
    mrow = lambda j: mod_ref[pl.ds(row, 1), j * D_MODEL:(j + 1) * D_MODEL]
    hs = hf_ref[...] + hb_ref[...]
    hn = jnp.concatenate([_rms(hs[:, hd * DH:(hd + 1) * DH]) for hd in range(N_HEADS)], axis=1)
    og = og_ref[...]
    ml = (hn * gm_ref[...] * (1.0 / (1.0 + jnp.exp(-og)))).astype(BF16)
    mix = (jnp.dot(conv_ref[...], wout_ref[0:CONV_W, :], preferred_element_type=F32)
           + jnp.dot(ml, wout_ref[CONV_W:, :], preferred_element_type=F32))
    x1 = x_ref[...] + mrow(2) * mix
    x1_ref[...] = x1
    h2 = (_rms(x1) * g2_ref[...] * (1.0 + mrow(4)) + mrow(3)).astype(BF16)
    h2_ref[...] = h2
    nt = (((1,), (1,)), ((), ()))
    for hd in range(PEER_HEADS):
        qh = jnp.dot(h2, wq_ref[:, hd * 2 * NKEYS:(hd + 1) * 2 * NKEYS], preferred_element_type=F32).astype(BF16)
        sc_ref[2 * hd] = lax.dot_general(k1_ref[...], qh[:, :NKEYS], nt, preferred_element_type=F32)
        sc_ref[2 * hd + 1] = lax.dot_general(k2_ref[...], qh[:, NKEYS:], nt, preferred_element_type=F32)


def _mix(x_all, conv, hf, hb, og, mod, g_mlstm, w_out, g2, wq, k1, k2, *, tm, n_ctx_tok, lat_len):
    T = x_all.shape[0]
    tok = lambda w: pl.BlockSpec((tm, w), lambda i: (i, 0))
    full = lambda a: pl.BlockSpec(a.shape, lambda i: (0,) * a.ndim)
    consts = (mod, g_mlstm.reshape(1, -1), w_out.astype(BF16), g2.reshape(1, -1), wq.astype(BF16),
              k1.astype(BF16), k2.astype(BF16))
    return pl.pallas_call(
        functools.partial(_mix_kernel, tm=tm, n_ctx_tok=n_ctx_tok, lat_len=lat_len),
        grid=(T // tm,),
        in_specs=[tok(D_MODEL), tok(CONV_W), tok(MLSTM_W), tok(MLSTM_W), tok(MLSTM_W)] + [full(a) for a in consts],
        out_specs=[tok(D_MODEL), tok(D_MODEL), pl.BlockSpec((2 * PEER_HEADS, NKEYS, tm), lambda i: (0, 0, i))],
        out_shape=[jax.ShapeDtypeStruct((T, D_MODEL), F32),
                   jax.ShapeDtypeStruct((T, D_MODEL), BF16),
                   jax.ShapeDtypeStruct((2 * PEER_HEADS, NKEYS, T), F32)],
        compiler_params=_cparams(("arbitrary",)),
        name="mix",
    )(x_all, conv, hf, hb, og, *consts)


_CAND_GROUPS = ((0, 16), (1, 8), (2, 8), (3, 8), (4, 8), (5, 8), (6, 8), (7, 8))
_CAND_ROWS = sum(n for _, n in _CAND_GROUPS) + 8


def _cand_meta():
    meta = np.zeros((2, _CAND_ROWS, LANES), np.float32)
    o = 0
    for r1, n in _CAND_GROUPS:
        for r2 in range(n):
            meta[0, o + r2] = r1 * TOPK + r2
            meta[1, o + r2] = 0.0 if (r1 + 1) * (r2 + 1) <= TOPK else NEG_INF
        o += n
    for r1 in range(8, 16):
        meta[0, o + r1 - 8] = r1 * TOPK
    return jnp.asarray(meta)


def _top16(s, key_io):
    rank = jnp.full(s.shape, float(TOPK), F32)
    cur = s
    vals = []
    for r in range(TOPK):
        mx = jnp.max(cur, axis=0, keepdims=True)
        idx = jnp.min(jnp.where(cur == mx, key_io, float(NKEYS)), axis=0, keepdims=True)
        sel = key_io == idx
        rank = jnp.where(sel, float(r), rank)
        cur = jnp.where(sel, NEG_INF, cur)
        vals.append(mx)
    return rank, vals


def _select_kernel(sc_ref, meta_ref, r2_ref, p2_ref, c_ref, p1_ref, *, tt):
    key_io = lax.broadcasted_iota(jnp.int32, (NKEYS, LANES), 0).astype(F32)
    flat = meta_ref[0]
    bias = meta_ref[1]

    def body(it, carry):
        hd = it // (tt // LANES)
        l0 = pl.multiple_of((it % (tt // LANES)) * LANES, LANES)
        s1 = sc_ref[2 * hd, :, pl.ds(l0, LANES)]
        s2 = sc_ref[2 * hd + 1, :, pl.ds(l0, LANES)]
        rank1, v1 = _top16(s1, key_io)
        rank2, v2 = _top16(s2, key_io)
        v2a = jnp.concatenate(v2, axis=0)
        parts = [v1[r1] + v2a[0:n] for r1, n in _CAND_GROUPS]
        parts.append(jnp.concatenate(v1[8:], axis=0) + v2[0])
        cand0 = jnp.concatenate(parts, axis=0) + bias
        cand = cand0
        chosen = jnp.zeros(cand.shape, F32)
        for _ in range(TOPK):
            mx = jnp.max(cand, axis=0, keepdims=True)
            idx = jnp.min(jnp.where(cand == mx, flat, 1e9), axis=0, keepdims=True)
            sel = flat == idx
            chosen = jnp.where(sel, 1.0, chosen)
            cand = jnp.where(sel, NEG_INF, cand)
        cmax = v1[0] + v2[0]
        z = jnp.sum(jnp.where(chosen > 0.0, jnp.exp(cand0 - cmax), 0.0), axis=0, keepdims=True)
        counts = []
        o = 0
        for _, n in _CAND_GROUPS:
            counts.append(jnp.sum(chosen[o:o + n], axis=0, keepdims=True))
            o += n
        for r in range(8):
            counts.append(chosen[o + r:o + r + 1])
        ckey = jnp.zeros((NKEYS, LANES), F32)
        for r in range(TOPK):
            ckey = jnp.where(rank1 == float(r), counts[r], ckey)
        r2_ref[hd, :, pl.ds(l0, LANES)] = rank2
        p2_ref[hd, :, pl.ds(l0, LANES)] = jnp.exp(s2 - v2[0])
        c_ref[hd, :, pl.ds(l0, LANES)] = ckey
        p1_ref[hd, :, pl.ds(l0, LANES)] = jnp.exp(s1 - v1[0]) / z
        return carry

    lax.fori_loop(0, PEER_HEADS * (tt // LANES), body, 0)


def _select(scores, *, tt):
    T = scores.shape[-1]
    meta = _cand_meta()
    tab = pl.BlockSpec((PEER_HEADS, NKEYS, tt), lambda i: (0, 0, i))
    return pl.pallas_call(
        functools.partial(_select_kernel, tt=tt),
        grid=(T // tt,),
        in_specs=[pl.BlockSpec((2 * PEER_HEADS, NKEYS, tt), lambda i: (0, 0, i)),
                  pl.BlockSpec(meta.shape, lambda i: (0, 0, 0))],
        out_specs=[tab] * 4,
        out_shape=[jax.ShapeDtypeStruct((PEER_HEADS, NKEYS, T), F32)] * 4,
        compiler_params=_cparams(("arbitrary",)),
        name="select",
    )(scores, meta)


_SQRT_HALF = 0.7071067811865476
_JROWS = 16


def _peer_kernel(h2_ref, u_ref, vt_ref, r2_ref, p2_ref, c_ref, p1_ref, x1_ref, mod_ref, gf_ref, y_ref,
                 a_scr, w_scr, acc_scr, *, tb, nb, n_ctx_tok, lat_len):
    i = pl.program_id(0)
    n = pl.program_id(1)
    n_i = nb // NKEYS
    n_lg = tb // LANES

    @pl.when(n == 0)
    def _():
        acc_scr[...] = jnp.zeros_like(acc_scr)

    a_scr[...] = lax.dot_general(u_ref[...], h2_ref[...], (((1,), (1,)), ((), ())), preferred_element_type=F32)

    def lane_group(lg, carry):
        lanes = pl.ds(pl.multiple_of(lg * LANES, LANES), LANES)
        for ig in range(n_i // 8):
            i0 = pl.multiple_of(n * n_i + ig * 8, 8)
            c8 = [c_ref[hd, pl.ds(i0, 8), lanes] for hd in range(PEER_HEADS)]
            p18 = [p1_ref[hd, pl.ds(i0, 8), lanes] for hd in range(PEER_HEADS)]
            for i8 in range(8):
                ii = ig * 8 + i8
                crow = [c8[hd][i8:i8 + 1, :] for hd in range(PEER_HEADS)]
                p1row = [p18[hd][i8:i8 + 1, :] for hd in range(PEER_HEADS)]

                def jgroup(jg, carry2, ii=ii, crow=crow, p1row=p1row):
                    j0 = pl.multiple_of(jg * _JROWS, _JROWS)
                    rows = pl.ds(j0, _JROWS)
                    arow = pl.ds(ii * NKEYS + j0, _JROWS)
                    a = a_scr[arow, lanes]
                    g = jnp.zeros((_JROWS, LANES), F32)
                    for hd in range(PEER_HEADS):
                        sel = jnp.where(r2_ref[hd, rows, lanes] < crow[hd], p2_ref[hd, rows, lanes], 0.0)
                        g = g + sel * p1row[hd]
                    w = g * (0.5 * a * (1.0 + lax.erf(a * _SQRT_HALF)))
                    w_scr[arow, lanes] = w.astype(BF16)
                    return carry2

                lax.fori_loop(0, NKEYS // _JROWS, jgroup, 0)
        return carry

    lax.fori_loop(0, n_lg, lane_group, 0)

    acc_scr[...] += jnp.dot(vt_ref[...], w_scr[...], preferred_element_type=F32)

    @pl.when(n == pl.num_programs(1) - 1)
    def _():
        _, row = _mod_row(i, tb, n_ctx_tok, lat_len)
        ga2 = mod_ref[pl.ds(row, 1), 5 * D_MODEL:6 * D_MODEL]
        x2 = x1_ref[...] + ga2 * acc_scr[...].T
        y_ref[...] = _rms(x2) * gf_ref[...]


def _peer(h2, u_bf, vt_bf, r2, p2, cc, p1, x1, mod, g_final, *, tb, nb, n_ctx_tok, lat_len):
    T = h2.shape[0]
    tab = pl.BlockSpec((PEER_HEADS, NKEYS, tb), lambda i, n: (0, 0, i))
    return pl.pallas_call(
        functools.partial(_peer_kernel, tb=tb, nb=nb, n_ctx_tok=n_ctx_tok, lat_len=lat_len),
        grid=(T // tb, N_EXPERTS // nb),
        in_specs=[pl.BlockSpec((tb, D_MODEL), lambda i, n: (i, 0)),
                  pl.BlockSpec((nb, D_MODEL), lambda i, n: (n, 0)),
                  pl.BlockSpec((D_MODEL, nb), lambda i, n: (0, n)),
                  tab, tab, tab, tab,
                  pl.BlockSpec((tb, D_MODEL), lambda i, n: (i, 0)),
                  pl.BlockSpec(mod.shape, lambda i, n: (0, 0)),
                  pl.BlockSpec((1, D_MODEL), lambda i, n: (0, 0))],
        out_specs=pl.BlockSpec((tb, D_MODEL), lambda i, n: (i, 0)),
        out_shape=jax.ShapeDtypeStruct((T, D_MODEL), F32),
        scratch_shapes=[pltpu.VMEM((nb, tb), F32), pltpu.VMEM((nb, tb), BF16), pltpu.VMEM((D_MODEL, tb), F32)],
        compiler_params=_cparams(("arbitrary", "arbitrary")),
        name="peer",
    )(h2, u_bf, vt_bf, r2, p2, cc, p1, x1, mod, g_final.reshape(1, -1))


def _layer(x_prompt, x_sample, state_C, state_n, state_m, c, c_ctx, w_ada, b_ada, g_norm1, w_in, b_igate, b_fgate,
           conv_w, conv_b, g_mlstm, w_out, g_norm2, peer_wq, peer_k1, peer_k2, peer_u, peer_v, g_final,
           *, tm, tc_lat, tt, tb, nb):
    B, S, _ = x_prompt.shape
    DB, DS, _ = x_sample.shape
    n_ctx_tok = B * S
    ng = N_DIRS * N_HEADS
    x_all = jnp.concatenate([x_prompt.reshape(n_ctx_tok, D_MODEL), x_sample.reshape(DB * DS, D_MODEL)], axis=0)
    cv8 = jnp.zeros((8, D_MODEL), F32).at[0].set(c_ctx).at[1:1 + DB].set(c)
    mod = _modulation(cv8, w_ada, b_ada)

    conv, q, k, v, og, gates, gates_t = _in_proj(
        x_all, mod, g_norm1, w_in, b_igate, b_fgate, conv_w, conv_b,
        tm=tm, n_ctx_tok=n_ctx_tok, ctx_len=S, lat_len=DS)

    def ext_state(C, n_, m_):
        nb_ = jnp.broadcast_to(n_[..., None], n_.shape + (DH,))
        cext = jnp.concatenate([C, nb_], axis=-1).reshape(-1, ng, DH, 2 * DH)
        mext = jnp.broadcast_to(m_.reshape(-1, ng, 1), (C.shape[0], ng, LANES))
        return cext, mext

    z_c, z_m = ext_state(jnp.zeros((B, N_DIRS, N_HEADS, DH, DH), F32), jnp.zeros((B, N_DIRS, N_HEADS, DH), F32),
                         jnp.zeros((B, N_DIRS, N_HEADS), F32))
    hf_c, hb_c, c_fin, m_fin = _mlstm(q, k, v, gates, gates_t, z_c, z_m, tok0=0, nseq=B, seq_len=S, tc=S,
                                      write_state=True)
    l_c, l_m = ext_state(state_C, state_n, state_m)
    hf_l, hb_l = _mlstm(q, k, v, gates, gates_t, l_c, l_m, tok0=n_ctx_tok, nseq=DB, seq_len=DS, tc=tc_lat,
                        write_state=False)
    hf = jnp.concatenate([hf_c, hf_l], axis=0)
    hb = jnp.concatenate([hb_c, hb_l], axis=0)

    x1, h2, scores = _mix(x_all, conv, hf, hb, og, mod, g_mlstm, w_out, g_norm2, peer_wq, peer_k1, peer_k2,
                          tm=tm, n_ctx_tok=n_ctx_tok, lat_len=DS)
    r2, p2, cc, p1 = _select(scores, tt=tt)
    y = _peer(h2, peer_u.astype(BF16), peer_v.astype(BF16).T, r2, p2, cc, p1, x1, mod, g_final,
              tb=tb, nb=nb, n_ctx_tok=n_ctx_tok, lat_len=DS)

    y_prompt = y[:n_ctx_tok].reshape(B, S, D_MODEL)
    y_sample = y[n_ctx_tok:].reshape(DB, DS, D_MODEL)
    new_C = c_fin[..., :DH].reshape(B, 1, N_DIRS, N_HEADS, DH, DH)
    new_n = c_fin[..., DH].reshape(B, 1, N_DIRS, N_HEADS, DH)
    new_m = m_fin[..., 0].reshape(B, 1, N_DIRS, N_HEADS)
    return y_prompt, y_sample, new_C, new_n, new_m


def kernel(x_prompt, x_sample, state_C, state_n, state_m, c, c_ctx, w_ada, b_ada, g_norm1, w_in, b_igate, b_fgate,
           conv_w, conv_b, g_mlstm, w_out, g_norm2, peer_wq, peer_k1, peer_k2, peer_u, peer_v, g_final):
    return _layer(x_prompt, x_sample, state_C[:, 0], state_n[:, 0], state_m[:, 0], c, c_ctx, w_ada[0], b_ada[0],
                  g_norm1[0], w_in[0], b_igate[0], b_fgate[0], conv_w[0], conv_b[0], g_mlstm[0], w_out[0],
                  g_norm2[0], peer_wq[0], peer_k1[0], peer_k2[0], peer_u[0], peer_v[0], g_final,
                  tm=512, tc_lat=256, tt=256, tb=512, nb=1024)
```

```python
import functools

import numpy as np
import jax
import jax.numpy as jnp
from jax import lax
from jax.experimental import pallas as pl
from jax.experimental.pallas import tpu as pltpu

F32 = jnp.float32
BF16 = jnp.bfloat16

D_MODEL = 1024
CONV_W = 512
N_HEADS = 4
DH = 128
MLSTM_W = N_HEADS * DH
N_DIRS = 2
GRID_W = 64
N_GROUPS = 7
GATE_COL0 = N_GROUPS * 512
N_GATES = 2 * N_DIRS * N_HEADS
PEER_HEADS = 8
NKEYS = 128
TOPK = 16
N_EXPERTS = NKEYS * NKEYS
EPS = 1e-6
LANES = 128
NEG_INF = float("-inf")

VMEM_LIMIT = 56 * 1024 * 1024


def _cparams(sem):
    return pltpu.CompilerParams(dimension_semantics=sem, vmem_limit_bytes=VMEM_LIMIT)


def _mod_kernel(cv_ref, w_ref, b_ref, o_ref):
    cv = cv_ref[...]
    s = cv * (1.0 / (1.0 + jnp.exp(-cv)))
    o_ref[...] = jnp.dot(s, w_ref[...], preferred_element_type=F32) + b_ref[...]


def _modulation(cv8, w_ada, b_ada):
    n = w_ada.shape[1]
    tn = 1024
    return pl.pallas_call(
        _mod_kernel,
        grid=(n // tn,),
        in_specs=[pl.BlockSpec((8, D_MODEL), lambda j: (0, 0)),
                  pl.BlockSpec((D_MODEL, tn), lambda j: (0, j)),
                  pl.BlockSpec((1, tn), lambda j: (0, j))],
        out_specs=pl.BlockSpec((8, tn), lambda j: (0, j)),
        out_shape=jax.ShapeDtypeStruct((8, n), F32),
        compiler_params=_cparams(("arbitrary",)),
        name="modulation",
    )(cv8, w_ada, b_ada.reshape(1, n))


def _mod_row(i, tm, n_ctx_tok, lat_len):
    n_ctx_tiles = n_ctx_tok // tm
    tiles_per_seq = lat_len // tm
    is_ctx = i < n_ctx_tiles
    row = jnp.where(is_ctx, 0, 1 + (i - n_ctx_tiles) // tiles_per_seq)
    return is_ctx, row


def _rms(x):
    return x * lax.rsqrt(jnp.mean(x * x, axis=-1, keepdims=True) + EPS)


def _log_sigmoid(z):
    return jnp.minimum(z, 0.0) - jnp.log(1.0 + jnp.exp(-jnp.abs(z)))


def _inproj_kernel(x_ref, mod_ref, g1_ref, win_ref, wg_ref, wgt_ref, bg_ref, bgt_ref, cw_ref, cb_ref,
                   conv_ref, q_ref, k_ref, v_ref, o_ref, gates_ref, gatest_ref,
                   *, tm, n_ctx_tok, ctx_len, lat_len):
    i = pl.program_id(0)
    is_ctx, row = _mod_row(i, tm, n_ctx_tok, lat_len)
    sh1 = mod_ref[pl.ds(row, 1), 0:D_MODEL]
    sc1 = mod_ref[pl.ds(row, 1), D_MODEL:2 * D_MODEL]
    h = _rms(x_ref[...]) * g1_ref[...] * (1.0 + sc1) + sh1
    hb = h.astype(BF16)

    def grp(g):
        return jnp.dot(hb, win_ref[:, g * 512:(g + 1) * 512], preferred_element_type=F32)

    u = grp(1) * grp(2)
    pm = jnp.where(is_ctx, ctx_len - 1, GRID_W - 1)
    pos = lax.broadcasted_iota(jnp.int32, (tm, 1), 0) & pm
    u_prev = jnp.where(pos == 0, 0.0, pltpu.roll(u, 1, axis=0))
    u_next = jnp.where(pos == pm, 0.0, pltpu.roll(u, tm - 1, axis=0))
    y = u_prev * cw_ref[0:1, :] + u * cw_ref[1:2, :] + u_next * cw_ref[2:3, :] + cb_ref[...]
    conv_ref[...] = (grp(0) * y).astype(BF16)

    q_ref[...] = (grp(3) * (DH ** -0.5)).astype(BF16)
    k_ref[...] = grp(4).astype(BF16)
    v_ref[...] = grp(5).astype(BF16)
    o_ref[...] = grp(6)

    z = jnp.dot(hb, wg_ref[...], preferred_element_type=F32) + bg_ref[...]
    lane = lax.broadcasted_iota(jnp.int32, z.shape, 1)
    gates_ref[...] = jnp.where(lane >= N_DIRS * N_HEADS, _log_sigmoid(z), z)
    zt = lax.dot_general(wgt_ref[...], hb, (((1,), (1,)), ((), ())), preferred_element_type=F32) + bgt_ref[...]
    sub = lax.broadcasted_iota(jnp.int32, zt.shape, 0)
    gatest_ref[...] = jnp.where(sub >= N_DIRS * N_HEADS, _log_sigmoid(zt), zt)


def _in_proj(x_all, mod, g1, w_in, b_igate, b_fgate, conv_w, conv_b, *, tm, n_ctx_tok, ctx_len, lat_len):
    T = x_all.shape[0]
    win = w_in[:, :GATE_COL0].astype(BF16)
    wg = jnp.zeros((D_MODEL, LANES), F32).at[:, :N_GATES].set(w_in[:, GATE_COL0:]).astype(BF16)
    bgate = jnp.zeros((1, LANES), F32).at[0, :N_GATES].set(
        jnp.concatenate([b_igate.reshape(-1), b_fgate.reshape(-1)]))
    tok = lambda w: pl.BlockSpec((tm, w), lambda i: (i, 0))
    full = lambda a: pl.BlockSpec(a.shape, lambda i: (0,) * a.ndim)
    args = (x_all, mod, g1.reshape(1, -1), win, wg, wg.T, bgate, bgate.T, conv_w, conv_b.reshape(1, -1))
    return pl.pallas_call(
        functools.partial(_inproj_kernel, tm=tm, n_ctx_tok=n_ctx_tok, ctx_len=ctx_len, lat_len=lat_len),
        grid=(T // tm,),
        in_specs=[tok(D_MODEL)] + [full(a) for a in args[1:]],
        out_specs=[tok(CONV_W), tok(MLSTM_W), tok(MLSTM_W), tok(MLSTM_W), tok(MLSTM_W), tok(LANES),
                   pl.BlockSpec((LANES, tm), lambda i: (0, i))],
        out_shape=[jax.ShapeDtypeStruct((T, CONV_W), BF16),
                   jax.ShapeDtypeStruct((T, MLSTM_W), BF16),
                   jax.ShapeDtypeStruct((T, MLSTM_W), BF16),
                   jax.ShapeDtypeStruct((T, MLSTM_W), BF16),
                   jax.ShapeDtypeStruct((T, MLSTM_W), F32),
                   jax.ShapeDtypeStruct((T, LANES), F32),
                   jax.ShapeDtypeStruct((LANES, T), F32)],
        compiler_params=_cparams(("arbitrary",)),
        name="in_proj",
    )(*args)


def _split_bf16(x):
    hi = x.astype(BF16)
    return hi, (x - hi.astype(F32)).astype(BF16)


def _mlstm_kernel(qf_ref, kf_ref, vf_ref, gf_ref, gtf_ref, qb_ref, kb_ref, vb_ref, gb_ref, gtb_ref,
                  c0_ref, m0_ref, hf_ref, hb_ref, *rest, tc, write_state):
    if write_state:
        cout_ref, mout_ref, c_scr, m_scr = rest
    else:
        c_scr, m_scr = rest
    c = pl.program_id(1)

    @pl.when(c == 0)
    def _():
        c_scr[...] = c0_ref[...]
        m_scr[...] = m0_ref[...]

    r_io = lax.broadcasted_iota(jnp.int32, (tc, tc), 0)
    c_io = lax.broadcasted_iota(jnp.int32, (tc, tc), 1)
    ones = jnp.ones((tc, DH), BF16)

    for d in range(N_DIRS):
        q_ref, k_ref, v_ref, g_ref, gt_ref, h_ref = (
            (qf_ref, kf_ref, vf_ref, gf_ref, gtf_ref, hf_ref) if d == 0 else
            (qb_ref, kb_ref, vb_ref, gb_ref, gtb_ref, hb_ref))
        causal = (c_io <= r_io) if d == 0 else (c_io >= r_io)
        last = tc - 1 if d == 0 else 0
        tri = causal.astype(BF16)
        tri_t = ((r_io <= c_io) if d == 0 else (r_io >= c_io)).astype(BF16)
        g = g_ref[...]
        gt = gt_ref[...]
        g_hi, g_lo = _split_bf16(g)
        gt_hi, gt_lo = _split_bf16(gt)
        bcum = (jnp.dot(tri, g_hi, preferred_element_type=F32) + jnp.dot(tri, g_lo, preferred_element_type=F32))
        bcum_t = (jnp.dot(gt_hi, tri_t, preferred_element_type=F32)
                  + jnp.dot(gt_lo, tri_t, preferred_element_type=F32))
        for hd in range(N_HEADS):
            gi = d * N_HEADS + hd
            gfi = N_DIRS * N_HEADS + gi
            sl = slice(hd * DH, (hd + 1) * DH)
            q = q_ref[:, sl]
            k = k_ref[:, sl]
            v = v_ref[:, sl]
            ig_c = g[:, gi:gi + 1]
            ig_r = gt[gi:gi + 1, :]
            b_c = bcum[:, gfi:gfi + 1]
            b_r = bcum_t[gfi:gfi + 1, :]
            m_prev = m_scr[gi:gi + 1, 0:1]
            dmat = jnp.where(causal, b_c - b_r + ig_r, NEG_INF)
            inter = b_c + m_prev
            m_t = jnp.maximum(inter, jnp.max(dmat, axis=-1, keepdims=True))
            s = lax.dot_general(q, k, (((1,), (1,)), ((), ())), preferred_element_type=F32) * jnp.exp(dmat - m_t)
            a = jnp.exp(inter - m_t)
            cext = c_scr[gi]
            vext = jnp.concatenate([v, ones], axis=1)
            nd = (a * jnp.dot(q, cext.astype(BF16), preferred_element_type=F32)
                  + jnp.dot(s.astype(BF16), vext, preferred_element_type=F32))
            num = nd[:, :DH]
            den = nd[:, DH:]
            h_ref[:, sl] = num / jnp.maximum(jnp.abs(den), jnp.exp(-m_t))
            m_new = m_t[last:last + 1, :]
            b_last = b_c[last:last + 1, :]
            w_c = jnp.exp(b_last - b_c + ig_c - m_new)
            decay = jnp.exp(b_last + m_prev - m_new)
            kw_t = (k.astype(F32) * w_c).T.astype(BF16)
            c_scr[gi] = decay * cext + jnp.dot(kw_t, vext, preferred_element_type=F32)
            m_scr[gi:gi + 1, :] = jnp.broadcast_to(m_new, (1, LANES))

    if write_state:
        @pl.when(c == pl.num_programs(1) - 1)
        def _():
            cout_ref[...] = c_scr[...]
            mout_ref[...] = m_scr[...]


def _mlstm(q, k, v, gates, gates_t, c0ext, m0, *, tok0, nseq, seq_len, tc, write_state):
    nc = seq_len // tc
    blk0 = tok0 // tc
    fwd = lambda b, c: b * nc + c
    bwd = lambda b, c: b * nc + nc - 1 - c
    tokspec = lambda w, f, o: pl.BlockSpec((tc, w), lambda b, c: (o + f(b, c), 0))
    gtspec = lambda f: pl.BlockSpec((LANES, tc), lambda b, c: (0, blk0 + f(b, c)))
    ng = N_DIRS * N_HEADS
    in_specs = [tokspec(MLSTM_W, fwd, blk0)] * 3 + [tokspec(LANES, fwd, blk0), gtspec(fwd)] \
        + [tokspec(MLSTM_W, bwd, blk0)] * 3 + [tokspec(LANES, bwd, blk0), gtspec(bwd)] \
        + [pl.BlockSpec((None, ng, DH, 2 * DH), lambda b, c: (b, 0, 0, 0)),
           pl.BlockSpec((None, ng, LANES), lambda b, c: (b, 0, 0))]
    out_specs = [tokspec(MLSTM_W, fwd, 0), tokspec(MLSTM_W, bwd, 0)]
    out_shape = [jax.ShapeDtypeStruct((nseq * seq_len, MLSTM_W), F32)] * 2
    if write_state:
        out_specs += [pl.BlockSpec((None, ng, DH, 2 * DH), lambda b, c: (b, 0, 0, 0)),
                      pl.BlockSpec((None, ng, LANES), lambda b, c: (b, 0, 0))]
        out_shape += [jax.ShapeDtypeStruct((nseq, ng, DH, 2 * DH), F32),
                      jax.ShapeDtypeStruct((nseq, ng, LANES), F32)]
    return pl.pallas_call(
        functools.partial(_mlstm_kernel, tc=tc, write_state=write_state),
        grid=(nseq, nc),
        in_specs=in_specs,
        out_specs=out_specs,
        out_shape=out_shape,
        scratch_shapes=[pltpu.VMEM((ng, DH, 2 * DH), F32), pltpu.VMEM((ng, LANES), F32)],
        compiler_params=_cparams(("arbitrary", "arbitrary")),
        name="mlstm_ctx" if write_state else "mlstm_lat",
    )(q, k, v, gates, gates_t, q, k, v, gates, gates_t, c0ext, m0)


def _mix_kernel(x_ref, conv_ref, hf_ref, hb_ref, og_ref, mod_ref, gm_ref, wout_ref, g2_ref, wq_ref,
                k1_ref, k2_ref, x1_ref, h2_ref, sc_ref, *, tm, n_ctx_tok, lat_len):
    i = pl.program_id(0)
    _, row = _mod_row(i, tm, n_ctx_tok, lat_len)
    mrow = lambda j: mod_ref[pl.ds(row, 1), j * D_MODEL:(j + 1) * D_MODEL]
    hs = hf_ref[...] + hb_ref[...]
    hn = jnp.concatenate([_rms(hs[:, hd * DH:(hd + 1) * DH]) for hd in range(N_HEADS)], axis=1)
    og = og_ref[...]
    ml = (hn * gm_ref[...] * (1.0 / (1.0 + jnp.exp(-og)))).astype(BF16)
    mix = (jnp.dot(conv_ref[...], wout_ref[0:CONV_W, :], preferred_element_type=F32)
           + jnp.dot(ml, wout_ref[CONV_W:, :], preferred_element_type=F32))
    x1 = x_ref[...] + mrow(2) * mix
    x1_ref[...] = x1
    h2 = (_rms(x1) * g2_ref[...] * (1.0 + mrow(4)) + mrow(3)).astype(BF16)
    h2_ref[...] = h2
    nt = (((1,), (1,)), ((), ()))
    for hd in range(PEER_HEADS):
        qh = jnp.dot(h2, wq_ref[:, hd * 2 * NKEYS:(hd + 1) * 2 * NKEYS], preferred_element_type=F32).astype(BF16)
        sc_ref[2 * hd] = lax.dot_general(k1_ref[...], qh[:, :NKEYS], nt, preferred_element_type=F32)
        sc_ref[2 * hd + 1] = lax.dot_general(k2_ref[...], qh[:, NKEYS:], nt, preferred_element_type=F32)


def _mix(x_all, conv, hf, hb, og, mod, g_mlstm, w_out, g2, wq, k1, k2, *, tm, n_ctx_tok, lat_len):
    T = x_all.shape[0]
    tok = lambda w: pl.BlockSpec((tm, w), lambda i: (i, 0))
    full = lambda a: pl.BlockSpec(a.shape, lambda i: (0,) * a.ndim)
    consts = (mod, g_mlstm.reshape(1, -1), w_out.astype(BF16), g2.reshape(1, -1), wq.astype(BF16),
              k1.astype(BF16), k2.astype(BF16))
    return pl.pallas_call(
        functools.partial(_mix_kernel, tm=tm, n_ctx_tok=n_ctx_tok, lat_len=lat_len),
        grid=(T // tm,),
        in_specs=[tok(D_MODEL), tok(CONV_W), tok(MLSTM_W), tok(MLSTM_W), tok(MLSTM_W)] + [full(a) for a in consts],
        out_specs=[tok(D_MODEL), tok(D_MODEL), pl.BlockSpec((2 * PEER_HEADS, NKEYS, tm), lambda i: (0, 0, i))],
        out_shape=[jax.ShapeDtypeStruct((T, D_MODEL), F32),
                   jax.ShapeDtypeStruct((T, D_MODEL), BF16),
                   jax.ShapeDtypeStruct((2 * PEER_HEADS, NKEYS, T), F32)],
        compiler_params=_cparams(("arbitrary",)),
        name="mix",
    )(x_all, conv, hf, hb, og, *consts)


_CAND_GROUPS = ((0, 16), (1, 8), (2, 8), (3, 8), (4, 8), (5, 8), (6, 8), (7, 8))
_CAND_ROWS = sum(n for _, n in _CAND_GROUPS) + 8


def _cand_meta():
    meta = np.zeros((2, _CAND_ROWS, LANES), np.float32)
    o = 0
    for r1, n in _CAND_GROUPS:
        for r2 in range(n):
            meta[0, o + r2] = r1 * TOPK + r2
            meta[1, o + r2] = 0.0 if (r1 + 1) * (r2 + 1) <= TOPK else NEG_INF
        o += n
    for r1 in range(8, 16):
        meta[0, o + r1 - 8] = r1 * TOPK
    return jnp.asarray(meta)


def _top16(s, key_io):
    rank = jnp.full(s.shape, float(TOPK), F32)
    cur = s
    vals = []
    for r in range(TOPK):
        mx = jnp.max(cur, axis=0, keepdims=True)
        idx = jnp.min(jnp.where(cur == mx, key_io, float(NKEYS)), axis=0, keepdims=True)
        sel = key_io == idx
        rank = jnp.where(sel, float(r), rank)
        cur = jnp.where(sel, NEG_INF, cur)
        vals.append(mx)
    return rank, vals


def _select_kernel(sc_ref, meta_ref, r2_ref, p2_ref, c_ref, p1_ref, *, tt):
    key_io = lax.broadcasted_iota(jnp.int32, (NKEYS, LANES), 0).astype(F32)
    flat = meta_ref[0]
    bias = meta_ref[1]

    def body(it, carry):
        hd = it // (tt // LANES)
        l0 = pl.multiple_of((it % (tt // LANES)) * LANES, LANES)
        s1 = sc_ref[2 * hd, :, pl.ds(l0, LANES)]
        s2 = sc_ref[2 * hd + 1, :, pl.ds(l0, LANES)]
        rank1, v1 = _top16(s1, key_io)
        rank2, v2 = _top16(s2, key_io)
        v2a = jnp.concatenate(v2, axis=0)
        parts = [v1[r1] + v2a[0:n] for r1, n in _CAND_GROUPS]
        parts.append(jnp.concatenate(v1[8:], axis=0) + v2[0])
        cand0 = jnp.concatenate(parts, axis=0) + bias
        cand = cand0
        chosen = jnp.zeros(cand.shape, F32)
        for _ in range(TOPK):
            mx = jnp.max(cand, axis=0, keepdims=True)
            idx = jnp.min(jnp.where(cand == mx, flat, 1e9), axis=0, keepdims=True)
            sel = flat == idx
            chosen = jnp.where(sel, 1.0, chosen)
            cand = jnp.where(sel, NEG_INF, cand)
        cmax = v1[0] + v2[0]
        z = jnp.sum(jnp.where(chosen > 0.0, jnp.exp(cand0 - cmax), 0.0), axis=0, keepdims=True)
        counts = []
        o = 0
        for _, n in _CAND_GROUPS:
            counts.append(jnp.sum(chosen[o:o + n], axis=0, keepdims=True))
            o += n
        for r in range(8):
            counts.append(chosen[o + r:o + r + 1])
        ckey = jnp.zeros((NKEYS, LANES), F32)
        for r in range(TOPK):
            ckey = jnp.where(rank1 == float(r), counts[r], ckey)
        r2_ref[hd, :, pl.ds(l0, LANES)] = rank2.astype(BF16)
        p2_ref[hd, :, pl.ds(l0, LANES)] = jnp.exp(s2 - v2[0]).astype(BF16)
        c_ref[hd, :, pl.ds(l0, LANES)] = ckey
        p1_ref[hd, :, pl.ds(l0, LANES)] = jnp.exp(s1 - v1[0]) / z
        return carry

    lax.fori_loop(0, PEER_HEADS * (tt // LANES), body, 0)


def _select(scores, *, tt):
    T = scores.shape[-1]
    meta = _cand_meta()
    tab = pl.BlockSpec((PEER_HEADS, NKEYS, tt), lambda i: (0, 0, i))
    return pl.pallas_call(
        functools.partial(_select_kernel, tt=tt),
        grid=(T // tt,),
        in_specs=[pl.BlockSpec((2 * PEER_HEADS, NKEYS, tt), lambda i: (0, 0, i)),
                  pl.BlockSpec(meta.shape, lambda i: (0, 0, 0))],
        out_specs=[tab] * 4,
        out_shape=[jax.ShapeDtypeStruct((PEER_HEADS, NKEYS, T), BF16)] * 2
        + [jax.ShapeDtypeStruct((PEER_HEADS, NKEYS, T), F32)] * 2,
        compiler_params=_cparams(("arbitrary",)),
        name="select",
    )(scores, meta)


_SQRT_HALF = 0.7071067811865476
_JROWS = 16
_PEER_NB = 8 * NKEYS
_SUB = _PEER_NB // 2


def _peer_kernel(h2_ref, u_ref, vt_ref, r2_ref, p2_ref, c_ref, p1_ref, x1_ref, mod_ref, gf_ref, y_ref,
                 a_scr, w_scr, acc_scr, r2_scr, p2_scr, *, tb, n_ctx_tok, lat_len):
    i = pl.program_id(0)
    n = pl.program_id(1)
    nb = _PEER_NB
    n_i = nb // NKEYS
    n_lg = tb // LANES

    @pl.when(n == 0)
    def _():
        acc_scr[...] = jnp.zeros_like(acc_scr)
        r2_scr[...] = r2_ref[...]
        p2_scr[...] = p2_ref[...]

    n_sub = nb // _SUB
    nt = (((1,), (1,)), ((), ()))
    for s in range(n_sub):
        rs = slice(s * _SUB, (s + 1) * _SUB)
        a_scr[rs, :] = lax.dot_general(u_ref[rs, :], h2_ref[...], nt, preferred_element_type=F32)

    for s in range(n_sub):
        for lg in range(n_lg):
            lanes = slice(lg * LANES, (lg + 1) * LANES)
            i0 = pl.multiple_of(n * n_i, 8)
            c8 = [c_ref[hd, pl.ds(i0, 8), lanes] for hd in range(PEER_HEADS)]
            p18 = [p1_ref[hd, pl.ds(i0, 8), lanes] for hd in range(PEER_HEADS)]
            for i8 in range(s * (_SUB // NKEYS), (s + 1) * (_SUB // NKEYS)):
                bcast = lambda t: jnp.broadcast_to(t[i8:i8 + 1, :], (_JROWS, LANES)).astype(BF16)
                crow = [bcast(c8[hd]) for hd in range(PEER_HEADS)]
                p1row = [bcast(p18[hd]) for hd in range(PEER_HEADS)]
                for jg in range(NKEYS // _JROWS):
                    rows = slice(jg * _JROWS, (jg + 1) * _JROWS)
                    arow = slice(i8 * NKEYS + jg * _JROWS, i8 * NKEYS + (jg + 1) * _JROWS)
                    a = a_scr[arow, lanes]
                    g = jnp.zeros((_JROWS, LANES), BF16)
                    for hd in range(PEER_HEADS):
                        p2 = p2_scr[hd, rows, lanes]
                        g = g + jnp.where(r2_scr[hd, rows, lanes] < crow[hd], p2, jnp.zeros_like(p2)) * p1row[hd]
                    gelu = 0.5 * a * (1.0 + lax.erf(a * _SQRT_HALF))
                    w_scr[arow, lanes] = g * gelu.astype(BF16)
        rs = slice(s * _SUB, (s + 1) * _SUB)
        acc_scr[...] += jnp.dot(vt_ref[:, rs], w_scr[rs, :], preferred_element_type=F32)

    @pl.when(n == pl.num_programs(1) - 1)
    def _():
        _, row = _mod_row(i, tb, n_ctx_tok, lat_len)
        ga2 = mod_ref[pl.ds(row, 1), 5 * D_MODEL:6 * D_MODEL]
        x2 = x1_ref[...] + ga2 * acc_scr[...].T
        y_ref[...] = _rms(x2) * gf_ref[...]


def _peer(h2, u_bf, vt_bf, r2, p2, cc, p1, x1, mod, g_final, *, tb, n_ctx_tok, lat_len):
    T = h2.shape[0]
    nb = _PEER_NB
    tab = pl.BlockSpec((PEER_HEADS, NKEYS, tb), lambda i, n: (0, 0, i))
    return pl.pallas_call(
        functools.partial(_peer_kernel, tb=tb, n_ctx_tok=n_ctx_tok, lat_len=lat_len),
        grid=(T // tb, N_EXPERTS // nb),
        in_specs=[pl.BlockSpec((tb, D_MODEL), lambda i, n: (i, 0)),
                  pl.BlockSpec((nb, D_MODEL), lambda i, n: (n, 0)),
                  pl.BlockSpec((D_MODEL, nb), lambda i, n: (0, n)),
                  tab, tab, tab, tab,
                  pl.BlockSpec((tb, D_MODEL), lambda i, n: (i, 0)),
                  pl.BlockSpec(mod.shape, lambda i, n: (0, 0)),
                  pl.BlockSpec((1, D_MODEL), lambda i, n: (0, 0))],
        out_specs=pl.BlockSpec((tb, D_MODEL), lambda i, n: (i, 0)),
        out_shape=jax.ShapeDtypeStruct((T, D_MODEL), F32),
        scratch_shapes=[pltpu.VMEM((nb, tb), F32), pltpu.VMEM((nb, tb), BF16), pltpu.VMEM((D_MODEL, tb), F32),
                        pltpu.VMEM((PEER_HEADS, NKEYS, tb), BF16), pltpu.VMEM((PEER_HEADS, NKEYS, tb), BF16)],
        compiler_params=_cparams(("arbitrary", "arbitrary")),
        name="peer",
    )(h2, u_bf, vt_bf, r2, p2, cc, p1, x1, mod, g_final.reshape(1, -1))


def _layer(x_prompt, x_sample, state_C, state_n, state_m, c, c_ctx, w_ada, b_ada, g_norm1, w_in, b_igate, b_fgate,
           conv_w, conv_b, g_mlstm, w_out, g_norm2, peer_wq, peer_k1, peer_k2, peer_u, peer_v, g_final,
           *, tm, tc_lat, tt, tb):
    B, S, _ = x_prompt.shape
    DB, DS, _ = x_sample.shape
    n_ctx_tok = B * S
    ng = N_DIRS * N_HEADS
    x_all = jnp.concatenate([x_prompt.reshape(n_ctx_tok, D_MODEL), x_sample.reshape(DB * DS, D_MODEL)], axis=0)
    cv8 = jnp.zeros((8, D_MODEL), F32).at[0].set(c_ctx).at[1:1 + DB].set(c)
    mod = _modulation(cv8, w_ada, b_ada)

    conv, q, k, v, og, gates, gates_t = _in_proj(
        x_all, mod, g_norm1, w_in, b_igate, b_fgate, conv_w, conv_b,
        tm=tm, n_ctx_tok=n_ctx_tok, ctx_len=S, lat_len=DS)

    def ext_state(C, n_, m_):
        nb_ = jnp.broadcast_to(n_[..., None], n_.shape + (DH,))
        cext = jnp.concatenate([C, nb_], axis=-1).reshape(-1, ng, DH, 2 * DH)
        mext = jnp.broadcast_to(m_.reshape(-1, ng, 1), (C.shape[0], ng, LANES))
        return cext, mext

    z_c, z_m = ext_state(jnp.zeros((B, N_DIRS, N_HEADS, DH, DH), F32), jnp.zeros((B, N_DIRS, N_HEADS, DH), F32),
                         jnp.zeros((B, N_DIRS, N_HEADS), F32))
    hf_c, hb_c, c_fin, m_fin = _mlstm(q, k, v, gates, gates_t, z_c, z_m, tok0=0, nseq=B, seq_len=S, tc=S,
                                      write_state=True)
    l_c, l_m = ext_state(state_C, state_n, state_m)
    hf_l, hb_l = _mlstm(q, k, v, gates, gates_t, l_c, l_m, tok0=n_ctx_tok, nseq=DB, seq_len=DS, tc=tc_lat,
                        write_state=False)
    hf = jnp.concatenate([hf_c, hf_l], axis=0)
    hb = jnp.concatenate([hb_c, hb_l], axis=0)

    x1, h2, scores = _mix(x_all, conv, hf, hb, og, mod, g_mlstm, w_out, g_norm2, peer_wq, peer_k1, peer_k2,
                          tm=tm, n_ctx_tok=n_ctx_tok, lat_len=DS)
    r2, p2, cc, p1 = _select(scores, tt=tt)
    y = _peer(h2, peer_u.astype(BF16), peer_v.astype(BF16).T, r2, p2, cc, p1, x1, mod, g_final,
              tb=tb, n_ctx_tok=n_ctx_tok, lat_len=DS)

    y_prompt = y[:n_ctx_tok].reshape(B, S, D_MODEL)
    y_sample = y[n_ctx_tok:].reshape(DB, DS, D_MODEL)
    new_C = c_fin[..., :DH].reshape(B, 1, N_DIRS, N_HEADS, DH, DH)
    new_n = c_fin[..., DH].reshape(B, 1, N_DIRS, N_HEADS, DH)
    new_m = m_fin[..., 0].reshape(B, 1, N_DIRS, N_HEADS)
    return y_prompt, y_sample, new_C, new_n, new_m


def kernel(x_prompt, x_sample, state_C, state_n, state_m, c, c_ctx, w_ada, b_ada, g_norm1, w_in, b_igate, b_fgate,
           conv_w, conv_b, g_mlstm, w_out, g_norm2, peer_wq, peer_k1, peer_k2, peer_u, peer_v, g_final):
    return _layer(x_prompt, x_sample, state_C[:, 0], state_n[:, 0], state_m[:, 0], c, c_ctx, w_ada[0], b_ada[0],
                  g_norm1[0], w_in[0], b_igate[0], b_fgate[0], conv_w[0], conv_b[0], g_mlstm[0], w_out[0],
                  g_norm2[0], peer_wq[0], peer_k1[0], peer_k2[0], peer_u[0], peer_v[0], g_final,
                  tm=512, tc_lat=256, tt=256, tb=512)
```

```python
import functools

import numpy as np
import jax
import jax.numpy as jnp
from jax import lax
from jax.experimental import pallas as pl
from jax.experimental.pallas import tpu as pltpu

F32 = jnp.float32
BF16 = jnp.bfloat16

D_MODEL = 1024
CONV_W = 512
N_HEADS = 4
DH = 128
MLSTM_W = N_HEADS * DH
N_DIRS = 2
GRID_W = 64
N_GROUPS = 7
GATE_COL0 = N_GROUPS * 512
N_GATES = 2 * N_DIRS * N_HEADS
PEER_HEADS = 8
NKEYS = 128
TOPK = 16
N_EXPERTS = NKEYS * NKEYS
EPS = 1e-6
LANES = 128
NEG_INF = float("-inf")

VMEM_LIMIT = 56 * 1024 * 1024


def _cparams(sem):
    return pltpu.CompilerParams(dimension_semantics=sem, vmem_limit_bytes=VMEM_LIMIT)


def _mod_kernel(cv_ref, w_ref, b_ref, o_ref):
    cv = cv_ref[...]
    s = cv * (1.0 / (1.0 + jnp.exp(-cv)))
    o_ref[...] = jnp.dot(s, w_ref[...], preferred_element_type=F32) + b_ref[...]


def _modulation(cv8, w_ada, b_ada):
    n = w_ada.shape[1]
    tn = 1024
    return pl.pallas_call(
        _mod_kernel,
        grid=(n // tn,),
        in_specs=[pl.BlockSpec((8, D_MODEL), lambda j: (0, 0)),
                  pl.BlockSpec((D_MODEL, tn), lambda j: (0, j)),
                  pl.BlockSpec((1, tn), lambda j: (0, j))],
        out_specs=pl.BlockSpec((8, tn), lambda j: (0, j)),
        out_shape=jax.ShapeDtypeStruct((8, n), F32),
        compiler_params=_cparams(("arbitrary",)),
        name="modulation",
    )(cv8, w_ada, b_ada.reshape(1, n))


def _mod_row(i, tm, n_ctx_tok, lat_len):
    n_ctx_tiles = n_ctx_tok // tm
    tiles_per_seq = lat_len // tm
    is_ctx = i < n_ctx_tiles
    row = jnp.where(is_ctx, 0, 1 + (i - n_ctx_tiles) // tiles_per_seq)
    return is_ctx, row


def _rms(x):
    return x * lax.rsqrt(jnp.mean(x * x, axis=-1, keepdims=True) + EPS)


def _log_sigmoid(z):
    return jnp.minimum(z, 0.0) - jnp.log(1.0 + jnp.exp(-jnp.abs(z)))


def _inproj_kernel(x_ref, mod_ref, g1_ref, win_ref, wg_ref, wgt_ref, bg_ref, bgt_ref, cw_ref, cb_ref,
                   conv_ref, q_ref, k_ref, v_ref, o_ref, gates_ref, gatest_ref,
                   *, tm, n_ctx_tok, ctx_len, lat_len):
    i = pl.program_id(0)
    is_ctx, row = _mod_row(i, tm, n_ctx_tok, lat_len)
    sh1 = mod_ref[pl.ds(row, 1), 0:D_MODEL]
    sc1 = mod_ref[pl.ds(row, 1), D_MODEL:2 * D_MODEL]
    h = _rms(x_ref[...]) * g1_ref[...] * (1.0 + sc1) + sh1
    hb = h.astype(BF16)

    def grp(g):
        return jnp.dot(hb, win_ref[:, g * 512:(g + 1) * 512], preferred_element_type=F32)

    u = grp(1) * grp(2)
    pm = jnp.where(is_ctx, ctx_len - 1, GRID_W - 1)
    pos = lax.broadcasted_iota(jnp.int32, (tm, 1), 0) & pm
    u_prev = jnp.where(pos == 0, 0.0, pltpu.roll(u, 1, axis=0))
    u_next = jnp.where(pos == pm, 0.0, pltpu.roll(u, tm - 1, axis=0))
    y = u_prev * cw_ref[0:1, :] + u * cw_ref[1:2, :] + u_next * cw_ref[2:3, :] + cb_ref[...]
    conv_ref[...] = (grp(0) * y).astype(BF16)

    q_ref[...] = (grp(3) * (DH ** -0.5)).astype(BF16)
    k_ref[...] = grp(4).astype(BF16)
    v_ref[...] = grp(5).astype(BF16)
    o_ref[...] = grp(6)

    z = jnp.dot(hb, wg_ref[...], preferred_element_type=F32) + bg_ref[...]
    lane = lax.broadcasted_iota(jnp.int32, z.shape, 1)
    gates_ref[...] = jnp.where(lane >= N_DIRS * N_HEADS, _log_sigmoid(z), z)
    zt = lax.dot_general(wgt_ref[...], hb, (((1,), (1,)), ((), ())), preferred_element_type=F32) + bgt_ref[...]
    sub = lax.broadcasted_iota(jnp.int32, zt.shape, 0)
    gatest_ref[...] = jnp.where(sub >= N_DIRS * N_HEADS, _log_sigmoid(zt), zt)


def _in_proj(x_all, mod, g1, w_in, b_igate, b_fgate, conv_w, conv_b, *, tm, n_ctx_tok, ctx_len, lat_len):
    T = x_all.shape[0]
    win = w_in[:, :GATE_COL0].astype(BF16)
    wg = jnp.zeros((D_MODEL, LANES), F32).at[:, :N_GATES].set(w_in[:, GATE_COL0:]).astype(BF16)
    bgate = jnp.zeros((1, LANES), F32).at[0, :N_GATES].set(
        jnp.concatenate([b_igate.reshape(-1), b_fgate.reshape(-1)]))
    tok = lambda w: pl.BlockSpec((tm, w), lambda i: (i, 0))
    full = lambda a: pl.BlockSpec(a.shape, lambda i: (0,) * a.ndim)
    args = (x_all, mod, g1.reshape(1, -1), win, wg, wg.T, bgate, bgate.T, conv_w, conv_b.reshape(1, -1))
    return pl.pallas_call(
        functools.partial(_inproj_kernel, tm=tm, n_ctx_tok=n_ctx_tok, ctx_len=ctx_len, lat_len=lat_len),
        grid=(T // tm,),
        in_specs=[tok(D_MODEL)] + [full(a) for a in args[1:]],
        out_specs=[tok(CONV_W), tok(MLSTM_W), tok(MLSTM_W), tok(MLSTM_W), tok(MLSTM_W), tok(LANES),
                   pl.BlockSpec((LANES, tm), lambda i: (0, i))],
        out_shape=[jax.ShapeDtypeStruct((T, CONV_W), BF16),
                   jax.ShapeDtypeStruct((T, MLSTM_W), BF16),
                   jax.ShapeDtypeStruct((T, MLSTM_W), BF16),
                   jax.ShapeDtypeStruct((T, MLSTM_W), BF16),
                   jax.ShapeDtypeStruct((T, MLSTM_W), F32),
                   jax.ShapeDtypeStruct((T, LANES), F32),
                   jax.ShapeDtypeStruct((LANES, T), F32)],
        compiler_params=_cparams(("arbitrary",)),
        name="in_proj",
    )(*args)


def _split_bf16(x):
    hi = x.astype(BF16)
    return hi, (x - hi.astype(F32)).astype(BF16)


def _mlstm_kernel(qf_ref, kf_ref, vf_ref, gf_ref, gtf_ref, qb_ref, kb_ref, vb_ref, gb_ref, gtb_ref,
                  c0_ref, m0_ref, hf_ref, hb_ref, *rest, tc, write_state):
    if write_state:
        cout_ref, mout_ref, c_scr, m_scr = rest
    else:
        c_scr, m_scr = rest
    c = pl.program_id(1)

    @pl.when(c == 0)
    def _():
        c_scr[...] = c0_ref[...]
        m_scr[...] = m0_ref[...]

    r_io = lax.broadcasted_iota(jnp.int32, (tc, tc), 0)
    c_io = lax.broadcasted_iota(jnp.int32, (tc, tc), 1)
    ones = jnp.ones((tc, DH), BF16)

    for d in range(N_DIRS):
        q_ref, k_ref, v_ref, g_ref, gt_ref, h_ref = (
            (qf_ref, kf_ref, vf_ref, gf_ref, gtf_ref, hf_ref) if d == 0 else
            (qb_ref, kb_ref, vb_ref, gb_ref, gtb_ref, hb_ref))
        causal = (c_io <= r_io) if d == 0 else (c_io >= r_io)
        last = tc - 1 if d == 0 else 0
        tri = causal.astype(BF16)
        tri_t = ((r_io <= c_io) if d == 0 else (r_io >= c_io)).astype(BF16)
        g = g_ref[...]
        gt = gt_ref[...]
        g_hi, g_lo = _split_bf16(g)
        gt_hi, gt_lo = _split_bf16(gt)
        bcum = (jnp.dot(tri, g_hi, preferred_element_type=F32) + jnp.dot(tri, g_lo, preferred_element_type=F32))
        bcum_t = (jnp.dot(gt_hi, tri_t, preferred_element_type=F32)
                  + jnp.dot(gt_lo, tri_t, preferred_element_type=F32))
        for hd in range(N_HEADS):
            gi = d * N_HEADS + hd
            gfi = N_DIRS * N_HEADS + gi
            sl = slice(hd * DH, (hd + 1) * DH)
            q = q_ref[:, sl]
            k = k_ref[:, sl]
            v = v_ref[:, sl]
            ig_c = g[:, gi:gi + 1]
            ig_r = gt[gi:gi + 1, :]
            b_c = bcum[:, gfi:gfi + 1]
            b_r = bcum_t[gfi:gfi + 1, :]
            m_prev = m_scr[gi:gi + 1, 0:1]
            dmat = jnp.where(causal, b_c - b_r + ig_r, NEG_INF)
            inter = b_c + m_prev
            m_t = jnp.maximum(inter, jnp.max(dmat, axis=-1, keepdims=True))
            s = lax.dot_general(q, k, (((1,), (1,)), ((), ())), preferred_element_type=F32) * jnp.exp(dmat - m_t)
            a = jnp.exp(inter - m_t)
            cext = c_scr[gi]
            vext = jnp.concatenate([v, ones], axis=1)
            nd = (a * jnp.dot(q, cext.astype(BF16), preferred_element_type=F32)
                  + jnp.dot(s.astype(BF16), vext, preferred_element_type=F32))
            num = nd[:, :DH]
            den = nd[:, DH:]
            h_ref[:, sl] = num / jnp.maximum(jnp.abs(den), jnp.exp(-m_t))
            m_new = m_t[last:last + 1, :]
            b_last = b_c[last:last + 1, :]
            w_c = jnp.exp(b_last - b_c + ig_c - m_new)
            decay = jnp.exp(b_last + m_prev - m_new)
            kw_t = (k.astype(F32) * w_c).T.astype(BF16)
            c_scr[gi] = decay * cext + jnp.dot(kw_t, vext, preferred_element_type=F32)
            m_scr[gi:gi + 1, :] = jnp.broadcast_to(m_new, (1, LANES))

    if write_state:
        @pl.when(c == pl.num_programs(1) - 1)
        def _():
            cout_ref[...] = c_scr[...]
            mout_ref[...] = m_scr[...]


def _mlstm(q, k, v, gates, gates_t, c0ext, m0, *, tok0, nseq, seq_len, tc, write_state):
    nc = seq_len // tc
    blk0 = tok0 // tc
    fwd = lambda b, c: b * nc + c
    bwd = lambda b, c: b * nc + nc - 1 - c
    tokspec = lambda w, f, o: pl.BlockSpec((tc, w), lambda b, c: (o + f(b, c), 0))
    gtspec = lambda f: pl.BlockSpec((LANES, tc), lambda b, c: (0, blk0 + f(b, c)))
    ng = N_DIRS * N_HEADS
    in_specs = [tokspec(MLSTM_W, fwd, blk0)] * 3 + [tokspec(LANES, fwd, blk0), gtspec(fwd)] \
        + [tokspec(MLSTM_W, bwd, blk0)] * 3 + [tokspec(LANES, bwd, blk0), gtspec(bwd)] \
        + [pl.BlockSpec((None, ng, DH, 2 * DH), lambda b, c: (b, 0, 0, 0)),
           pl.BlockSpec((None, ng, LANES), lambda b, c: (b, 0, 0))]
    out_specs = [tokspec(MLSTM_W, fwd, 0), tokspec(MLSTM_W, bwd, 0)]
    out_shape = [jax.ShapeDtypeStruct((nseq * seq_len, MLSTM_W), F32)] * 2
    if write_state:
        out_specs += [pl.BlockSpec((None, ng, DH, 2 * DH), lambda b, c: (b, 0, 0, 0)),
                      pl.BlockSpec((None, ng, LANES), lambda b, c: (b, 0, 0))]
        out_shape += [jax.ShapeDtypeStruct((nseq, ng, DH, 2 * DH), F32),
                      jax.ShapeDtypeStruct((nseq, ng, LANES), F32)]
    return pl.pallas_call(
        functools.partial(_mlstm_kernel, tc=tc, write_state=write_state),
        grid=(nseq, nc),
        in_specs=in_specs,
        out_specs=out_specs,
        out_shape=out_shape,
        scratch_shapes=[pltpu.VMEM((ng, DH, 2 * DH), F32), pltpu.VMEM((ng, LANES), F32)],
        compiler_params=_cparams(("arbitrary", "arbitrary")),
        name="mlstm_ctx" if write_state else "mlstm_lat",
    )(q, k, v, gates, gates_t, q, k, v, gates, gates_t, c0ext, m0)


def _mix_kernel(x_ref, conv_ref, hf_ref, hb_ref, og_ref, mod_ref, gm_ref, wout_ref, g2_ref, wq_ref,
                k1_ref, k2_ref, x1_ref, h2_ref, sc_ref, *, tm, n_ctx_tok, lat_len):
    i = pl.program_id(0)
    _, row = _mod_row(i, tm, n_ctx_tok, lat_len)
    mrow = lambda j: mod_ref[pl.ds(row, 1), j * D_MODEL:(j + 1) * D_MODEL]
    hs = hf_ref[...] + hb_ref[...]
    hn = jnp.concatenate([_rms(hs[:, hd * DH:(hd + 1) * DH]) for hd in range(N_HEADS)], axis=1)
    og = og_ref[...]
    ml = (hn * gm_ref[...] * (1.0 / (1.0 + jnp.exp(-og)))).astype(BF16)
    mix = (jnp.dot(conv_ref[...], wout_ref[0:CONV_W, :], preferred_element_type=F32)
           + jnp.dot(ml, wout_ref[CONV_W:, :], preferred_element_type=F32))
    x1 = x_ref[...] + mrow(2) * mix
    x1_ref[...] = x1
    h2 = (_rms(x1) * g2_ref[...] * (1.0 + mrow(4)) + mrow(3)).astype(BF16)
    h2_ref[...] = h2
    nt = (((1,), (1,)), ((), ()))
    for hd in range(PEER_HEADS):
        qh = jnp.dot(h2, wq_ref[:, hd * 2 * NKEYS:(hd + 1) * 2 * NKEYS], preferred_element_type=F32).astype(BF16)
        sc_ref[2 * hd] = lax.dot_general(k1_ref[...], qh[:, :NKEYS], nt, preferred_element_type=F32)
        sc_ref[2 * hd + 1] = lax.dot_general(k2_ref[...], qh[:, NKEYS:], nt, preferred_element_type=F32)


def _mix(x_all, conv, hf, hb, og, mod, g_mlstm, w_out, g2, wq, k1, k2, *, tm, n_ctx_tok, lat_len):
    T = x_all.shape[0]
    tok = lambda w: pl.BlockSpec((tm, w), lambda i: (i, 0))
    full = lambda a: pl.BlockSpec(a.shape, lambda i: (0,) * a.ndim)
    consts = (mod, g_mlstm.reshape(1, -1), w_out.astype(BF16), g2.reshape(1, -1), wq.astype(BF16),
              k1.astype(BF16), k2.astype(BF16))
    return pl.pallas_call(
        functools.partial(_mix_kernel, tm=tm, n_ctx_tok=n_ctx_tok, lat_len=lat_len),
        grid=(T // tm,),
        in_specs=[tok(D_MODEL), tok(CONV_W), tok(MLSTM_W), tok(MLSTM_W), tok(MLSTM_W)] + [full(a) for a in consts],
        out_specs=[tok(D_MODEL), tok(D_MODEL), pl.BlockSpec((2 * PEER_HEADS, NKEYS, tm), lambda i: (0, 0, i))],
        out_shape=[jax.ShapeDtypeStruct((T, D_MODEL), F32),
                   jax.ShapeDtypeStruct((T, D_MODEL), BF16),
                   jax.ShapeDtypeStruct((2 * PEER_HEADS, NKEYS, T), F32)],
        compiler_params=_cparams(("arbitrary",)),
        name="mix",
    )(x_all, conv, hf, hb, og, *consts)


_CAND_GROUPS = ((0, 16), (1, 8), (2, 8), (3, 8), (4, 8), (5, 8), (6, 8), (7, 8))
_CAND_ROWS = sum(n for _, n in _CAND_GROUPS) + 8


def _cand_meta():
    meta = np.zeros((2, _CAND_ROWS, LANES), np.float32)
    o = 0
    for r1, n in _CAND_GROUPS:
        for r2 in range(n):
            meta[0, o + r2] = r1 * TOPK + r2
            meta[1, o + r2] = 0.0 if (r1 + 1) * (r2 + 1) <= TOPK else NEG_INF
        o += n
    for r1 in range(8, 16):
        meta[0, o + r1 - 8] = r1 * TOPK
    return jnp.asarray(meta)


def _top16(s, key_io):
    rank = jnp.full(s.shape, float(TOPK), F32)
    cur = s
    vals = []
    for r in range(TOPK):
        mx = jnp.max(cur, axis=0, keepdims=True)
        idx = jnp.min(jnp.where(cur == mx, key_io, float(NKEYS)), axis=0, keepdims=True)
        sel = key_io == idx
        rank = jnp.where(sel, float(r), rank)
        cur = jnp.where(sel, NEG_INF, cur)
        vals.append(mx)
    return rank, vals


def _select_kernel(sc_ref, meta_ref, r2_ref, p2_ref, c_ref, p1_ref, *, tt):
    key_io = lax.broadcasted_iota(jnp.int32, (NKEYS, LANES), 0).astype(F32)
    flat = meta_ref[0]
    bias = meta_ref[1]

    def body(it, carry):
        hd = it // (tt // LANES)
        l0 = pl.multiple_of((it % (tt // LANES)) * LANES, LANES)
        s1 = sc_ref[2 * hd, :, pl.ds(l0, LANES)]
        s2 = sc_ref[2 * hd + 1, :, pl.ds(l0, LANES)]
        rank1, v1 = _top16(s1, key_io)
        rank2, v2 = _top16(s2, key_io)
        v2a = jnp.concatenate(v2, axis=0)
        parts = [v1[r1] + v2a[0:n] for r1, n in _CAND_GROUPS]
        parts.append(jnp.concatenate(v1[8:], axis=0) + v2[0])
        cand0 = jnp.concatenate(parts, axis=0) + bias
        cand = cand0
        chosen = jnp.zeros(cand.shape, F32)
        for _ in range(TOPK):
            mx = jnp.max(cand, axis=0, keepdims=True)
            idx = jnp.min(jnp.where(cand == mx, flat, 1e9), axis=0, keepdims=True)
            sel = flat == idx
            chosen = jnp.where(sel, 1.0, chosen)
            cand = jnp.where(sel, NEG_INF, cand)
        cmax = v1[0] + v2[0]
        z = jnp.sum(jnp.where(chosen > 0.0, jnp.exp(cand0 - cmax), 0.0), axis=0, keepdims=True)
        counts = []
        o = 0
        for _, n in _CAND_GROUPS:
            counts.append(jnp.sum(chosen[o:o + n], axis=0, keepdims=True))
            o += n
        for r in range(8):
            counts.append(chosen[o + r:o + r + 1])
        ckey = jnp.zeros((NKEYS, LANES), F32)
        for r in range(TOPK):
            ckey = jnp.where(rank1 == float(r), counts[r], ckey)
        r2_ref[hd, :, pl.ds(l0, LANES)] = rank2.astype(BF16)
        p2_ref[hd, :, pl.ds(l0, LANES)] = jnp.exp(s2 - v2[0]).astype(BF16)
        c_ref[hd, :, pl.ds(l0, LANES)] = ckey
        p1_ref[hd, :, pl.ds(l0, LANES)] = jnp.exp(s1 - v1[0]) / z
        return carry

    lax.fori_loop(0, PEER_HEADS * (tt // LANES), body, 0)


def _select(scores, *, tt):
    T = scores.shape[-1]
    meta = _cand_meta()
    tab = pl.BlockSpec((PEER_HEADS, NKEYS, tt), lambda i: (0, 0, i))
    return pl.pallas_call(
        functools.partial(_select_kernel, tt=tt),
        grid=(T // tt,),
        in_specs=[pl.BlockSpec((2 * PEER_HEADS, NKEYS, tt), lambda i: (0, 0, i)),
                  pl.BlockSpec(meta.shape, lambda i: (0, 0, 0))],
        out_specs=[tab] * 4,
        out_shape=[jax.ShapeDtypeStruct((PEER_HEADS, NKEYS, T), BF16)] * 2
        + [jax.ShapeDtypeStruct((PEER_HEADS, NKEYS, T), F32)] * 2,
        compiler_params=_cparams(("arbitrary",)),
        name="select",
    )(scores, meta)


_SQRT_HALF = 0.7071067811865476
_JROWS = 16
_PEER_NB = 8 * NKEYS
_PEER_STEPS = N_EXPERTS // _PEER_NB


def _peer_kernel(h2t_ref, u_ref, vt_ref, r2_ref, p2_ref, c_ref, p1_ref, x1_ref, mod_ref, gf_ref, y_ref,
                 a_scr, w_scr, acc_scr, tab_scr, *, tb, n_ctx_tok, lat_len):
    L = pl.program_id(0)
    n_lg = tb // LANES
    vpu_blk = L - 1
    acc_blk = L - 2

    @pl.when(L == 0)
    def _():
        a_scr[...] = jnp.zeros_like(a_scr)
        w_scr[...] = jnp.zeros_like(w_scr)
        acc_scr[...] = jnp.zeros_like(acc_scr)

    @pl.when((L == 0) | (vpu_blk % _PEER_STEPS == 0))
    def _():
        for jg in range(NKEYS // _JROWS):
            rows = slice(jg * _JROWS, (jg + 1) * _JROWS)
            for lg in range(n_lg):
                lanes = slice(lg * LANES, (lg + 1) * LANES)
                for hd in range(PEER_HEADS):
                    tab_scr[jg, lg, hd * _JROWS:(hd + 1) * _JROWS, :] = r2_ref[hd, rows, lanes]
                    tab_scr[jg, lg, (PEER_HEADS + hd) * _JROWS:(PEER_HEADS + hd + 1) * _JROWS, :] = \
                        p2_ref[hd, rows, lanes]

    @pl.when((acc_blk >= 0) & (acc_blk % _PEER_STEPS == 0))
    def _():
        acc_scr[...] = jnp.zeros_like(acc_scr)

    def stages(cur):
        prv = 1 - cur
        a_scr[cur] = jnp.dot(u_ref[...], h2t_ref[...], preferred_element_type=F32)
        acc_scr[...] += jnp.dot(vt_ref[...], w_scr[cur], preferred_element_type=F32)
        for lg in range(n_lg):
            lanes = slice(lg * LANES, (lg + 1) * LANES)
            for i8 in range(_PEER_NB // NKEYS):
                bcast = lambda ref, hd: jnp.broadcast_to(ref[hd, i8:i8 + 1, lanes], (_JROWS, LANES)).astype(BF16)
                crow = [bcast(c_ref, hd) for hd in range(PEER_HEADS)]
                p1row = [bcast(p1_ref, hd) for hd in range(PEER_HEADS)]
                for jg in range(NKEYS // _JROWS):
                    rows = slice(jg * _JROWS, (jg + 1) * _JROWS)
                    arow = slice(i8 * NKEYS + jg * _JROWS, i8 * NKEYS + (jg + 1) * _JROWS)
                    a = a_scr[prv, arow, lanes]
                    g = jnp.zeros((_JROWS, LANES), BF16)
                    for hd in range(PEER_HEADS):
                        r2 = tab_scr[jg, lg, hd * _JROWS:(hd + 1) * _JROWS, :]
                        p2 = tab_scr[jg, lg, (PEER_HEADS + hd) * _JROWS:(PEER_HEADS + hd + 1) * _JROWS, :]
                        g = g + jnp.where(r2 < crow[hd], p2, jnp.zeros_like(p2)) * p1row[hd]
                    gelu = 0.5 * a * (1.0 + lax.erf(a * _SQRT_HALF))
                    w_scr[prv, arow, lanes] = g * gelu.astype(BF16)

    for slot in range(2):
        pl.when(L % 2 == slot)(functools.partial(stages, slot))

    @pl.when((acc_blk >= 0) & (acc_blk % _PEER_STEPS == _PEER_STEPS - 1))
    def _():
        _, row = _mod_row(acc_blk // _PEER_STEPS, tb, n_ctx_tok, lat_len)
        ga2 = mod_ref[pl.ds(row, 1), 5 * D_MODEL:6 * D_MODEL]
        x2 = x1_ref[...] + ga2 * acc_scr[...].T
        y_ref[...] = _rms(x2) * gf_ref[...]


def _peer(h2t, u_bf, vt_bf, r2, p2, cc, p1, x1, mod, g_final, *, tb, n_ctx_tok, lat_len):
    T = h2t.shape[1]
    nb = _PEER_NB
    n_tiles = T // tb
    n_blocks = n_tiles * _PEER_STEPS
    blk = lambda L, lag: jnp.clip(L - lag, 0, n_blocks - 1)
    tile = lambda L, lag: blk(L, lag) // _PEER_STEPS
    step = lambda L, lag: blk(L, lag) % _PEER_STEPS
    tab = pl.BlockSpec((PEER_HEADS, NKEYS, tb), lambda L: (0, 0, tile(L, 1)))
    rowtab = pl.BlockSpec((PEER_HEADS, nb // NKEYS, tb), lambda L: (0, step(L, 1), tile(L, 1)))
    return pl.pallas_call(
        functools.partial(_peer_kernel, tb=tb, n_ctx_tok=n_ctx_tok, lat_len=lat_len),
        grid=(n_blocks + 2,),
        in_specs=[pl.BlockSpec((D_MODEL, tb), lambda L: (0, tile(L, 0))),
                  pl.BlockSpec((nb, D_MODEL), lambda L: (step(L, 0), 0)),
                  pl.BlockSpec((D_MODEL, nb), lambda L: (0, step(L, 2))),
                  tab, tab, rowtab, rowtab,
                  pl.BlockSpec((tb, D_MODEL), lambda L: (tile(L, 2), 0)),
                  pl.BlockSpec(mod.shape, lambda L: (0, 0)),
                  pl.BlockSpec((1, D_MODEL), lambda L: (0, 0))],
        out_specs=pl.BlockSpec((tb, D_MODEL), lambda L: (tile(L, 2), 0)),
        out_shape=jax.ShapeDtypeStruct((T, D_MODEL), F32),
        scratch_shapes=[pltpu.VMEM((2, nb, tb), F32), pltpu.VMEM((2, nb, tb), BF16), pltpu.VMEM((D_MODEL, tb), F32),
                        pltpu.VMEM((NKEYS // _JROWS, tb // LANES, 2 * PEER_HEADS * _JROWS, LANES), BF16)],
        compiler_params=_cparams(("arbitrary",)),
        name="peer",
    )(h2t, u_bf, vt_bf, r2, p2, cc, p1, x1, mod, g_final.reshape(1, -1))


def _layer(x_prompt, x_sample, state_C, state_n, state_m, c, c_ctx, w_ada, b_ada, g_norm1, w_in, b_igate, b_fgate,
           conv_w, conv_b, g_mlstm, w_out, g_norm2, peer_wq, peer_k1, peer_k2, peer_u, peer_v, g_final,
           *, tm, tc_lat, tt, tb):
    B, S, _ = x_prompt.shape
    DB, DS, _ = x_sample.shape
    n_ctx_tok = B * S
    ng = N_DIRS * N_HEADS
    x_all = jnp.concatenate([x_prompt.reshape(n_ctx_tok, D_MODEL), x_sample.reshape(DB * DS, D_MODEL)], axis=0)
    cv8 = jnp.zeros((8, D_MODEL), F32).at[0].set(c_ctx).at[1:1 + DB].set(c)
    mod = _modulation(cv8, w_ada, b_ada)

    conv, q, k, v, og, gates, gates_t = _in_proj(
        x_all, mod, g_norm1, w_in, b_igate, b_fgate, conv_w, conv_b,
        tm=tm, n_ctx_tok=n_ctx_tok, ctx_len=S, lat_len=DS)

    def ext_state(C, n_, m_):
        nb_ = jnp.broadcast_to(n_[..., None], n_.shape + (DH,))
        cext = jnp.concatenate([C, nb_], axis=-1).reshape(-1, ng, DH, 2 * DH)
        mext = jnp.broadcast_to(m_.reshape(-1, ng, 1), (C.shape[0], ng, LANES))
        return cext, mext

    z_c, z_m = ext_state(jnp.zeros((B, N_DIRS, N_HEADS, DH, DH), F32), jnp.zeros((B, N_DIRS, N_HEADS, DH), F32),
                         jnp.zeros((B, N_DIRS, N_HEADS), F32))
    hf_c, hb_c, c_fin, m_fin = _mlstm(q, k, v, gates, gates_t, z_c, z_m, tok0=0, nseq=B, seq_len=S, tc=S,
                                      write_state=True)
    l_c, l_m = ext_state(state_C, state_n, state_m)
    hf_l, hb_l = _mlstm(q, k, v, gates, gates_t, l_c, l_m, tok0=n_ctx_tok, nseq=DB, seq_len=DS, tc=tc_lat,
                        write_state=False)
    hf = jnp.concatenate([hf_c, hf_l], axis=0)
    hb = jnp.concatenate([hb_c, hb_l], axis=0)

    x1, h2, scores = _mix(x_all, conv, hf, hb, og, mod, g_mlstm, w_out, g_norm2, peer_wq, peer_k1, peer_k2,
                          tm=tm, n_ctx_tok=n_ctx_tok, lat_len=DS)
    r2, p2, cc, p1 = _select(scores, tt=tt)
    y = _peer(h2.T, peer_u.astype(BF16), peer_v.astype(BF16).T, r2, p2, cc, p1, x1, mod, g_final,
              tb=tb, n_ctx_tok=n_ctx_tok, lat_len=DS)

    y_prompt = y[:n_ctx_tok].reshape(B, S, D_MODEL)
    y_sample = y[n_ctx_tok:].reshape(DB, DS, D_MODEL)
    new_C = c_fin[..., :DH].reshape(B, 1, N_DIRS, N_HEADS, DH, DH)
    new_n = c_fin[..., DH].reshape(B, 1, N_DIRS, N_HEADS, DH)
    new_m = m_fin[..., 0].reshape(B, 1, N_DIRS, N_HEADS)
    return y_prompt, y_sample, new_C, new_n, new_m


def kernel(x_prompt, x_sample, state_C, state_n, state_m, c, c_ctx, w_ada, b_ada, g_norm1, w_in, b_igate, b_fgate,
           conv_w, conv_b, g_mlstm, w_out, g_norm2, peer_wq, peer_k1, peer_k2, peer_u, peer_v, g_final):
    return _layer(x_prompt, x_sample, state_C[:, 0], state_n[:, 0], state_m[:, 0], c, c_ctx, w_ada[0], b_ada[0],
                  g_norm1[0], w_in[0], b_igate[0], b_fgate[0], conv_w[0], conv_b[0], g_mlstm[0], w_out[0],
                  g_norm2[0], peer_wq[0], peer_k1[0], peer_k2[0], peer_u[0], peer_v[0], g_final,
                  tm=512, tc_lat=256, tt=256, tb=512)
```

```python
import functools

import numpy as np
import jax
import jax.numpy as jnp
from jax import lax
from jax.experimental import pallas as pl
from jax.experimental.pallas import tpu as pltpu

F32 = jnp.float32
BF16 = jnp.bfloat16

D_MODEL = 1024
CONV_W = 512
N_HEADS = 4
DH = 128
MLSTM_W = N_HEADS * DH
N_DIRS = 2
GRID_W = 64
N_GROUPS = 7
GATE_COL0 = N_GROUPS * 512
N_GATES = 2 * N_DIRS * N_HEADS
PEER_HEADS = 8
NKEYS = 128
TOPK = 16
N_EXPERTS = NKEYS * NKEYS
EPS = 1e-6
LANES = 128
NEG_INF = float("-inf")

VMEM_LIMIT = 56 * 1024 * 1024


def _cparams(sem):
    return pltpu.CompilerParams(dimension_semantics=sem, vmem_limit_bytes=VMEM_LIMIT)


def _mod_kernel(cv_ref, w_ref, b_ref, o_ref):
    cv = cv_ref[...]
    s = cv * (1.0 / (1.0 + jnp.exp(-cv)))
    o_ref[...] = jnp.dot(s, w_ref[...], preferred_element_type=F32) + b_ref[...]


def _modulation(cv8, w_ada, b_ada):
    n = w_ada.shape[1]
    tn = 1024
    return pl.pallas_call(
        _mod_kernel,
        grid=(n // tn,),
        in_specs=[pl.BlockSpec((8, D_MODEL), lambda j: (0, 0)),
                  pl.BlockSpec((D_MODEL, tn), lambda j: (0, j)),
                  pl.BlockSpec((1, tn), lambda j: (0, j))],
        out_specs=pl.BlockSpec((8, tn), lambda j: (0, j)),
        out_shape=jax.ShapeDtypeStruct((8, n), F32),
        compiler_params=_cparams(("arbitrary",)),
        name="modulation",
    )(cv8, w_ada, b_ada.reshape(1, n))


def _mod_row(i, tm, n_ctx_tok, lat_len):
    n_ctx_tiles = n_ctx_tok // tm
    tiles_per_seq = lat_len // tm
    is_ctx = i < n_ctx_tiles
    row = jnp.where(is_ctx, 0, 1 + (i - n_ctx_tiles) // tiles_per_seq)
    return is_ctx, row


def _rms(x):
    return x * lax.rsqrt(jnp.mean(x * x, axis=-1, keepdims=True) + EPS)


def _log_sigmoid(z):
    return jnp.minimum(z, 0.0) - jnp.log(1.0 + jnp.exp(-jnp.abs(z)))


def _inproj_kernel(x_ref, mod_ref, g1_ref, win_ref, wg_ref, wgt_ref, bg_ref, bgt_ref, cw_ref, cb_ref,
                   conv_ref, q_ref, k_ref, v_ref, o_ref, gates_ref, gatest_ref,
                   *, tm, n_ctx_tok, ctx_len, lat_len):
    i = pl.program_id(0)
    is_ctx, row = _mod_row(i, tm, n_ctx_tok, lat_len)
    sh1 = mod_ref[pl.ds(row, 1), 0:D_MODEL]
    sc1 = mod_ref[pl.ds(row, 1), D_MODEL:2 * D_MODEL]
    h = _rms(x_ref[...]) * g1_ref[...] * (1.0 + sc1) + sh1
    hb = h.astype(BF16)

    def grp(g):
        return jnp.dot(hb, win_ref[:, g * 512:(g + 1) * 512], preferred_element_type=F32)

    u = grp(1) * grp(2)
    pm = jnp.where(is_ctx, ctx_len - 1, GRID_W - 1)
    pos = lax.broadcasted_iota(jnp.int32, (tm, 1), 0) & pm
    u_prev = jnp.where(pos == 0, 0.0, pltpu.roll(u, 1, axis=0))
    u_next = jnp.where(pos == pm, 0.0, pltpu.roll(u, tm - 1, axis=0))
    y = u_prev * cw_ref[0:1, :] + u * cw_ref[1:2, :] + u_next * cw_ref[2:3, :] + cb_ref[...]
    conv_ref[...] = (grp(0) * y).astype(BF16)

    q_ref[...] = (grp(3) * (DH ** -0.5)).astype(BF16)
    k_ref[...] = grp(4).astype(BF16)
    v_ref[...] = grp(5).astype(BF16)
    o_ref[...] = grp(6)

    z = jnp.dot(hb, wg_ref[...], preferred_element_type=F32) + bg_ref[...]
    lane = lax.broadcasted_iota(jnp.int32, z.shape, 1)
    gates_ref[...] = jnp.where(lane >= N_DIRS * N_HEADS, _log_sigmoid(z), z)
    zt = lax.dot_general(wgt_ref[...], hb, (((1,), (1,)), ((), ())), preferred_element_type=F32) + bgt_ref[...]
    sub = lax.broadcasted_iota(jnp.int32, zt.shape, 0)
    gatest_ref[...] = jnp.where(sub >= N_DIRS * N_HEADS, _log_sigmoid(zt), zt)


def _in_proj(x_all, mod, g1, w_in, b_igate, b_fgate, conv_w, conv_b, *, tm, n_ctx_tok, ctx_len, lat_len):
    T = x_all.shape[0]
    win = w_in[:, :GATE_COL0].astype(BF16)
    wg = jnp.zeros((D_MODEL, LANES), F32).at[:, :N_GATES].set(w_in[:, GATE_COL0:]).astype(BF16)
    bgate = jnp.zeros((1, LANES), F32).at[0, :N_GATES].set(
        jnp.concatenate([b_igate.reshape(-1), b_fgate.reshape(-1)]))
    tok = lambda w: pl.BlockSpec((tm, w), lambda i: (i, 0))
    full = lambda a: pl.BlockSpec(a.shape, lambda i: (0,) * a.ndim)
    args = (x_all, mod, g1.reshape(1, -1), win, wg, wg.T, bgate, bgate.T, conv_w, conv_b.reshape(1, -1))
    return pl.pallas_call(
        functools.partial(_inproj_kernel, tm=tm, n_ctx_tok=n_ctx_tok, ctx_len=ctx_len, lat_len=lat_len),
        grid=(T // tm,),
        in_specs=[tok(D_MODEL)] + [full(a) for a in args[1:]],
        out_specs=[tok(CONV_W), tok(MLSTM_W), tok(MLSTM_W), tok(MLSTM_W), tok(MLSTM_W), tok(LANES),
                   pl.BlockSpec((LANES, tm), lambda i: (0, i))],
        out_shape=[jax.ShapeDtypeStruct((T, CONV_W), BF16),
                   jax.ShapeDtypeStruct((T, MLSTM_W), BF16),
                   jax.ShapeDtypeStruct((T, MLSTM_W), BF16),
                   jax.ShapeDtypeStruct((T, MLSTM_W), BF16),
                   jax.ShapeDtypeStruct((T, MLSTM_W), F32),
                   jax.ShapeDtypeStruct((T, LANES), F32),
                   jax.ShapeDtypeStruct((LANES, T), F32)],
        compiler_params=_cparams(("arbitrary",)),
        name="in_proj",
    )(*args)


def _split_bf16(x):
    hi = x.astype(BF16)
    return hi, (x - hi.astype(F32)).astype(BF16)


def _mlstm_kernel(qf_ref, kf_ref, vf_ref, gf_ref, gtf_ref, qb_ref, kb_ref, vb_ref, gb_ref, gtb_ref,
                  c0_ref, m0_ref, hf_ref, hb_ref, *rest, tc, write_state):
    if write_state:
        cout_ref, mout_ref, c_scr, m_scr = rest
    else:
        c_scr, m_scr = rest
    c = pl.program_id(1)

    @pl.when(c == 0)
    def _():
        c_scr[...] = c0_ref[...]
        m_scr[...] = m0_ref[...]

    r_io = lax.broadcasted_iota(jnp.int32, (tc, tc), 0)
    c_io = lax.broadcasted_iota(jnp.int32, (tc, tc), 1)
    ones = jnp.ones((tc, DH), BF16)

    for d in range(N_DIRS):
        q_ref, k_ref, v_ref, g_ref, gt_ref, h_ref = (
            (qf_ref, kf_ref, vf_ref, gf_ref, gtf_ref, hf_ref) if d == 0 else
            (qb_ref, kb_ref, vb_ref, gb_ref, gtb_ref, hb_ref))
        causal = (c_io <= r_io) if d == 0 else (c_io >= r_io)
        last = tc - 1 if d == 0 else 0
        tri = causal.astype(BF16)
        tri_t = ((r_io <= c_io) if d == 0 else (r_io >= c_io)).astype(BF16)
        g = g_ref[...]
        gt = gt_ref[...]
        g_hi, g_lo = _split_bf16(g)
        gt_hi, gt_lo = _split_bf16(gt)
        bcum = (jnp.dot(tri, g_hi, preferred_element_type=F32) + jnp.dot(tri, g_lo, preferred_element_type=F32))
        bcum_t = (jnp.dot(gt_hi, tri_t, preferred_element_type=F32)
                  + jnp.dot(gt_lo, tri_t, preferred_element_type=F32))
        for hd in range(N_HEADS):
            gi = d * N_HEADS + hd
            gfi = N_DIRS * N_HEADS + gi
            sl = slice(hd * DH, (hd + 1) * DH)
            q = q_ref[:, sl]
            k = k_ref[:, sl]
            v = v_ref[:, sl]
            ig_c = g[:, gi:gi + 1]
            ig_r = gt[gi:gi + 1, :]
            b_c = bcum[:, gfi:gfi + 1]
            b_r = bcum_t[gfi:gfi + 1, :]
            m_prev = m_scr[gi:gi + 1, 0:1]
            dmat = jnp.where(causal, b_c - b_r + ig_r, NEG_INF)
            inter = b_c + m_prev
            m_t = jnp.maximum(inter, jnp.max(dmat, axis=-1, keepdims=True))
            s = lax.dot_general(q, k, (((1,), (1,)), ((), ())), preferred_element_type=F32) * jnp.exp(dmat - m_t)
            a = jnp.exp(inter - m_t)
            cext = c_scr[gi]
            vext = jnp.concatenate([v, ones], axis=1)
            nd = (a * jnp.dot(q, cext.astype(BF16), preferred_element_type=F32)
                  + jnp.dot(s.astype(BF16), vext, preferred_element_type=F32))
            num = nd[:, :DH]
            den = nd[:, DH:]
            h_ref[:, sl] = num / jnp.maximum(jnp.abs(den), jnp.exp(-m_t))
            m_new = m_t[last:last + 1, :]
            b_last = b_c[last:last + 1, :]
            w_c = jnp.exp(b_last - b_c + ig_c - m_new)
            decay = jnp.exp(b_last + m_prev - m_new)
            kw_t = (k.astype(F32) * w_c).T.astype(BF16)
            c_scr[gi] = decay * cext + jnp.dot(kw_t, vext, preferred_element_type=F32)
            m_scr[gi:gi + 1, :] = jnp.broadcast_to(m_new, (1, LANES))

    if write_state:
        @pl.when(c == pl.num_programs(1) - 1)
        def _():
            cout_ref[...] = c_scr[...]
            mout_ref[...] = m_scr[...]


def _mlstm(q, k, v, gates, gates_t, c0ext, m0, *, tok0, nseq, seq_len, tc, write_state):
    nc = seq_len // tc
    blk0 = tok0 // tc
    fwd = lambda b, c: b * nc + c
    bwd = lambda b, c: b * nc + nc - 1 - c
    tokspec = lambda w, f, o: pl.BlockSpec((tc, w), lambda b, c: (o + f(b, c), 0))
    gtspec = lambda f: pl.BlockSpec((LANES, tc), lambda b, c: (0, blk0 + f(b, c)))
    ng = N_DIRS * N_HEADS
    in_specs = [tokspec(MLSTM_W, fwd, blk0)] * 3 + [tokspec(LANES, fwd, blk0), gtspec(fwd)] \
        + [tokspec(MLSTM_W, bwd, blk0)] * 3 + [tokspec(LANES, bwd, blk0), gtspec(bwd)] \
        + [pl.BlockSpec((None, ng, DH, 2 * DH), lambda b, c: (b, 0, 0, 0)),
           pl.BlockSpec((None, ng, LANES), lambda b, c: (b, 0, 0))]
    out_specs = [tokspec(MLSTM_W, fwd, 0), tokspec(MLSTM_W, bwd, 0)]
    out_shape = [jax.ShapeDtypeStruct((nseq * seq_len, MLSTM_W), F32)] * 2
    if write_state:
        out_specs += [pl.BlockSpec((None, ng, DH, 2 * DH), lambda b, c: (b, 0, 0, 0)),
                      pl.BlockSpec((None, ng, LANES), lambda b, c: (b, 0, 0))]
        out_shape += [jax.ShapeDtypeStruct((nseq, ng, DH, 2 * DH), F32),
                      jax.ShapeDtypeStruct((nseq, ng, LANES), F32)]
    return pl.pallas_call(
        functools.partial(_mlstm_kernel, tc=tc, write_state=write_state),
        grid=(nseq, nc),
        in_specs=in_specs,
        out_specs=out_specs,
        out_shape=out_shape,
        scratch_shapes=[pltpu.VMEM((ng, DH, 2 * DH), F32), pltpu.VMEM((ng, LANES), F32)],
        compiler_params=_cparams(("arbitrary", "arbitrary")),
        name="mlstm_ctx" if write_state else "mlstm_lat",
    )(q, k, v, gates, gates_t, q, k, v, gates, gates_t, c0ext, m0)


def _mix_kernel(x_ref, conv_ref, hf_ref, hb_ref, og_ref, mod_ref, gm_ref, wout_ref, g2_ref, wq_ref,
                k1_ref, k2_ref, x1_ref, h2_ref, sc_ref, *, tm, n_ctx_tok, lat_len):
    i = pl.program_id(0)
    _, row = _mod_row(i, tm, n_ctx_tok, lat_len)
    mrow = lambda j: mod_ref[pl.ds(row, 1), j * D_MODEL:(j + 1) * D_MODEL]
    hs = hf_ref[...] + hb_ref[...]
    hn = jnp.concatenate([_rms(hs[:, hd * DH:(hd + 1) * DH]) for hd in range(N_HEADS)], axis=1)
    og = og_ref[...]
    ml = (hn * gm_ref[...] * (1.0 / (1.0 + jnp.exp(-og)))).astype(BF16)
    mix = (jnp.dot(conv_ref[...], wout_ref[0:CONV_W, :], preferred_element_type=F32)
           + jnp.dot(ml, wout_ref[CONV_W:, :], preferred_element_type=F32))
    x1 = x_ref[...] + mrow(2) * mix
    x1_ref[...] = x1
    h2 = (_rms(x1) * g2_ref[...] * (1.0 + mrow(4)) + mrow(3)).astype(BF16)
    h2_ref[...] = h2
    nt = (((1,), (1,)), ((), ()))
    for hd in range(PEER_HEADS):
        qh = jnp.dot(h2, wq_ref[:, hd * 2 * NKEYS:(hd + 1) * 2 * NKEYS], preferred_element_type=F32).astype(BF16)
        sc_ref[2 * hd] = lax.dot_general(k1_ref[...], qh[:, :NKEYS], nt, preferred_element_type=F32)
        sc_ref[2 * hd + 1] = lax.dot_general(k2_ref[...], qh[:, NKEYS:], nt, preferred_element_type=F32)


def _mix(x_all, conv, hf, hb, og, mod, g_mlstm, w_out, g2, wq, k1, k2, *, tm, n_ctx_tok, lat_len):
    T = x_all.shape[0]
    tok = lambda w: pl.BlockSpec((tm, w), lambda i: (i, 0))
    full = lambda a: pl.BlockSpec(a.shape, lambda i: (0,) * a.ndim)
    consts = (mod, g_mlstm.reshape(1, -1), w_out.astype(BF16), g2.reshape(1, -1), wq.astype(BF16),
              k1.astype(BF16), k2.astype(BF16))
    return pl.pallas_call(
        functools.partial(_mix_kernel, tm=tm, n_ctx_tok=n_ctx_tok, lat_len=lat_len),
        grid=(T // tm,),
        in_specs=[tok(D_MODEL), tok(CONV_W), tok(MLSTM_W), tok(MLSTM_W), tok(MLSTM_W)] + [full(a) for a in consts],
        out_specs=[tok(D_MODEL), tok(D_MODEL), pl.BlockSpec((2 * PEER_HEADS, NKEYS, tm), lambda i: (0, 0, i))],
        out_shape=[jax.ShapeDtypeStruct((T, D_MODEL), F32),
                   jax.ShapeDtypeStruct((T, D_MODEL), BF16),
                   jax.ShapeDtypeStruct((2 * PEER_HEADS, NKEYS, T), F32)],
        compiler_params=_cparams(("arbitrary",)),
        name="mix",
    )(x_all, conv, hf, hb, og, *consts)


def _sort16_pairs():
    def merge(lo, hi, r):
        step = r * 2
        if step < hi - lo:
            yield from merge(lo, hi, step)
            yield from merge(lo + r, hi, step)
            yield from ((i, i + r) for i in range(lo + r, hi - r, step))
        else:
            yield (lo, lo + r)

    def sort(lo, hi):
        if hi - lo >= 1:
            mid = lo + (hi - lo) // 2
            yield from sort(lo, mid)
            yield from sort(mid + 1, hi)
            yield from merge(lo, hi, 1)

    return tuple(sort(0, TOPK - 1))


_SORT16 = _sort16_pairs()
_CAND_PAIRS = tuple((r1, r2) for r1 in range(TOPK) for r2 in range(TOPK) if (r1 + 1) * (r2 + 1) <= TOPK)


def _cmpx(x, i, j):
    a, b = x[i], x[j]
    if b is None:
        return
    if a is None:
        x[i], x[j] = b, None
        return
    x[i], x[j] = jnp.maximum(a, b), jnp.minimum(a, b)


def _sort16(x):
    x = list(x)
    for i, j in _SORT16:
        _cmpx(x, i, j)
    return x


def _merge_top16(a, b):
    x = []
    for k in range(TOPK):
        p, q = a[k], b[TOPK - 1 - k]
        x.append(q if p is None else p if q is None else jnp.maximum(p, q))
    d = TOPK // 2
    while d:
        for k in range(TOPK):
            if not k & d:
                _cmpx(x, k, k + d)
        d //= 2
    return x


def _top16_values(groups):
    groups = [_sort16(g) for g in groups]
    while len(groups) > 1:
        groups = [_merge_top16(groups[k], groups[k + 1]) for k in range(0, len(groups), 2)]
    return groups[0]


def _count_gt(vs, s):
    g = jnp.where(vs[0] > s, 1.0, 0.0)
    for r in range(1, len(vs)):
        g = jnp.where(vs[r] > s, float(r + 1), g)
    return g


def _select_kernel(s_ref, r2_ref, p2_ref, c_ref, p1_ref, t_scr):
    one = lambda m: jnp.where(m, 1.0, 0.0)
    v = [_top16_values([[s_ref[h, g * TOPK + k] for k in range(TOPK)] for g in range(NKEYS // TOPK)])
         for h in range(2)]
    v1, v2 = v

    tie = jnp.zeros((8, LANES), F32)
    for h in range(2):
        n_ge = one(s_ref[h, 0] >= v[h][TOPK - 1])
        for j in range(1, NKEYS):
            n_ge = n_ge + one(s_ref[h, j] >= v[h][TOPK - 1])
        tie = jnp.maximum(tie, one(n_ge > float(TOPK)))
        for r in range(TOPK - 1):
            tie = jnp.maximum(tie, one(v[h][r] == v[h][r + 1]))
    t_scr[...] = jnp.zeros_like(t_scr)

    @pl.when(jnp.max(tie) > 0.0)
    def _():
        def lower_equal(jp, carry):
            for h in range(2):
                sv = s_ref[h, jp]
                for j in range(1, NKEYS):
                    inc = jnp.where(jp < j, 1.0, 0.0)
                    t_scr[h, j] = t_scr[h, j] + jnp.where(s_ref[h, j] == sv, inc, 0.0)
            return carry
        lax.fori_loop(0, NKEYS - 1, lower_equal, 0)

    cand = {p: v1[p[0]] + v2[p[1]] for p in _CAND_PAIRS}
    rest = [cand[p] for p in _CAND_PAIRS if p[0] > 0]
    rest += [None] * (-len(rest) % TOPK)
    groups = [[cand[(0, r2)] for r2 in range(TOPK)]] + [rest[k:k + TOPK] for k in range(0, len(rest), TOPK)]
    while len(groups) & (len(groups) - 1):
        groups.append([None] * TOPK)
    w = _top16_values(groups)
    tau = w[TOPK - 1]
    n_gt = sum(one(cand[p] > tau) for p in _CAND_PAIRS)
    need = float(TOPK) - n_gt
    eq_seen = jnp.zeros((8, LANES), F32)
    counts = [jnp.zeros((8, LANES), F32) for _ in range(TOPK)]
    for p in _CAND_PAIRS:
        eq = cand[p] == tau
        counts[p[0]] = counts[p[0]] + one((cand[p] > tau) | (eq & (eq_seen < need)))
        eq_seen = eq_seen + one(eq)
    z = jnp.ones((8, LANES), F32)
    for r in range(1, TOPK):
        z = z + jnp.exp(w[r] - w[0])
    inv_z = 1.0 / z

    for j in range(NKEYS):
        s1 = s_ref[0, j]
        s2 = s_ref[1, j]
        rank1 = _count_gt(v1, s1) + t_scr[0, j]
        ckey = jnp.zeros((8, LANES), F32)
        for r in range(TOPK):
            ckey = jnp.where(rank1 == float(r), counts[r], ckey)
        c_ref[j] = ckey
        p1_ref[j] = jnp.exp(s1 - v1[0]) * inv_z
        r2_ref[j] = _count_gt(v2, s2) + t_scr[1, j]
        p2_ref[j] = jnp.exp(s2 - v2[0])


def _select(scores):
    T = scores.shape[-1]
    rows = T // LANES
    tab = pl.BlockSpec((None, NKEYS, 8, LANES), lambda i, hd: (hd, 0, i, 0))
    out = pl.pallas_call(
        _select_kernel,
        grid=(rows // 8, PEER_HEADS),
        in_specs=[pl.BlockSpec((2, NKEYS, 8, LANES), lambda i, hd: (hd, 0, i, 0))],
        out_specs=[tab] * 4,
        out_shape=[jax.ShapeDtypeStruct((PEER_HEADS, NKEYS, rows, LANES), F32)] * 4,
        scratch_shapes=[pltpu.VMEM((2, NKEYS, 8, LANES), F32)],
        compiler_params=_cparams(("arbitrary", "arbitrary")),
        name="select",
    )(scores.reshape(2 * PEER_HEADS, NKEYS, rows, LANES))
    return [t.reshape(PEER_HEADS, NKEYS, T) for t in out]


_SQRT_HALF = 0.7071067811865476
_JROWS = 16
_PEER_NB = 8 * NKEYS
_PEER_STEPS = N_EXPERTS // _PEER_NB


def _peer_kernel(h2t_ref, u_ref, vt_ref, r2_ref, p2_ref, c_ref, p1_ref, x1_ref, mod_ref, gf_ref, y_ref,
                 a_scr, w_scr, acc_scr, tab_scr, *, tb, n_ctx_tok, lat_len):
    L = pl.program_id(0)
    n_lg = tb // LANES
    vpu_blk = L - 1
    acc_blk = L - 2

    @pl.when(L == 0)
    def _():
        a_scr[...] = jnp.zeros_like(a_scr)
        w_scr[...] = jnp.zeros_like(w_scr)
        acc_scr[...] = jnp.zeros_like(acc_scr)

    @pl.when((L == 0) | (vpu_blk % _PEER_STEPS == 0))
    def _():
        for jg in range(NKEYS // _JROWS):
            rows = slice(jg * _JROWS, (jg + 1) * _JROWS)
            for lg in range(n_lg):
                lanes = slice(lg * LANES, (lg + 1) * LANES)
                for hd in range(PEER_HEADS):
                    tab_scr[jg, lg, hd * _JROWS:(hd + 1) * _JROWS, :] = r2_ref[hd, rows, lanes].astype(BF16)
                    tab_scr[jg, lg, (PEER_HEADS + hd) * _JROWS:(PEER_HEADS + hd + 1) * _JROWS, :] = \
                        p2_ref[hd, rows, lanes].astype(BF16)

    @pl.when((acc_blk >= 0) & (acc_blk % _PEER_STEPS == 0))
    def _():
        acc_scr[...] = jnp.zeros_like(acc_scr)

    def stages(cur):
        prv = 1 - cur
        a_scr[cur] = jnp.dot(u_ref[...], h2t_ref[...], preferred_element_type=F32)
        acc_scr[...] += jnp.dot(vt_ref[...], w_scr[cur], preferred_element_type=F32)
        for lg in range(n_lg):
            lanes = slice(lg * LANES, (lg + 1) * LANES)
            for i8 in range(_PEER_NB // NKEYS):
                bcast = lambda ref, hd: jnp.broadcast_to(ref[hd, i8:i8 + 1, lanes], (_JROWS, LANES)).astype(BF16)
                crow = [bcast(c_ref, hd) for hd in range(PEER_HEADS)]
                p1row = [bcast(p1_ref, hd) for hd in range(PEER_HEADS)]
                for jg in range(NKEYS // _JROWS):
                    rows = slice(jg * _JROWS, (jg + 1) * _JROWS)
                    arow = slice(i8 * NKEYS + jg * _JROWS, i8 * NKEYS + (jg + 1) * _JROWS)
                    a = a_scr[prv, arow, lanes]
                    g = jnp.zeros((_JROWS, LANES), BF16)
                    for hd in range(PEER_HEADS):
                        r2 = tab_scr[jg, lg, hd * _JROWS:(hd + 1) * _JROWS, :]
                        p2 = tab_scr[jg, lg, (PEER_HEADS + hd) * _JROWS:(PEER_HEADS + hd + 1) * _JROWS, :]
                        g = g + jnp.where(r2 < crow[hd], p2, jnp.zeros_like(p2)) * p1row[hd]
                    gelu = 0.5 * a * (1.0 + lax.erf(a * _SQRT_HALF))
                    w_scr[prv, arow, lanes] = g * gelu.astype(BF16)

    for slot in range(2):
        pl.when(L % 2 == slot)(functools.partial(stages, slot))

    @pl.when((acc_blk >= 0) & (acc_blk % _PEER_STEPS == _PEER_STEPS - 1))
    def _():
        _, row = _mod_row(acc_blk // _PEER_STEPS, tb, n_ctx_tok, lat_len)
        ga2 = mod_ref[pl.ds(row, 1), 5 * D_MODEL:6 * D_MODEL]
        x2 = x1_ref[...] + ga2 * acc_scr[...].T
        y_ref[...] = _rms(x2) * gf_ref[...]


def _peer(h2t, u_bf, vt_bf, r2, p2, cc, p1, x1, mod, g_final, *, tb, n_ctx_tok, lat_len):
    T = h2t.shape[1]
    nb = _PEER_NB
    n_tiles = T // tb
    n_blocks = n_tiles * _PEER_STEPS
    blk = lambda L, lag: jnp.clip(L - lag, 0, n_blocks - 1)
    tile = lambda L, lag: blk(L, lag) // _PEER_STEPS
    step = lambda L, lag: blk(L, lag) % _PEER_STEPS
    tab = pl.BlockSpec((PEER_HEADS, NKEYS, tb), lambda L: (0, 0, tile(L, 1)))
    rowtab = pl.BlockSpec((PEER_HEADS, nb // NKEYS, tb), lambda L: (0, step(L, 1), tile(L, 1)))
    return pl.pallas_call(
        functools.partial(_peer_kernel, tb=tb, n_ctx_tok=n_ctx_tok, lat_len=lat_len),
        grid=(n_blocks + 2,),
        in_specs=[pl.BlockSpec((D_MODEL, tb), lambda L: (0, tile(L, 0))),
                  pl.BlockSpec((nb, D_MODEL), lambda L: (step(L, 0), 0)),
                  pl.BlockSpec((D_MODEL, nb), lambda L: (0, step(L, 2))),
                  tab, tab, rowtab, rowtab,
                  pl.BlockSpec((tb, D_MODEL), lambda L: (tile(L, 2), 0)),
                  pl.BlockSpec(mod.shape, lambda L: (0, 0)),
                  pl.BlockSpec((1, D_MODEL), lambda L: (0, 0))],
        out_specs=pl.BlockSpec((tb, D_MODEL), lambda L: (tile(L, 2), 0)),
        out_shape=jax.ShapeDtypeStruct((T, D_MODEL), F32),
        scratch_shapes=[pltpu.VMEM((2, nb, tb), F32), pltpu.VMEM((2, nb, tb), BF16), pltpu.VMEM((D_MODEL, tb), F32),
                        pltpu.VMEM((NKEYS // _JROWS, tb // LANES, 2 * PEER_HEADS * _JROWS, LANES), BF16)],
        compiler_params=_cparams(("arbitrary",)),
        name="peer",
    )(h2t, u_bf, vt_bf, r2, p2, cc, p1, x1, mod, g_final.reshape(1, -1))


def _layer(x_prompt, x_sample, state_C, state_n, state_m, c, c_ctx, w_ada, b_ada, g_norm1, w_in, b_igate, b_fgate,
           conv_w, conv_b, g_mlstm, w_out, g_norm2, peer_wq, peer_k1, peer_k2, peer_u, peer_v, g_final,
           *, tm, tc_lat, tb):
    B, S, _ = x_prompt.shape
    DB, DS, _ = x_sample.shape
    n_ctx_tok = B * S
    ng = N_DIRS * N_HEADS
    x_all = jnp.concatenate([x_prompt.reshape(n_ctx_tok, D_MODEL), x_sample.reshape(DB * DS, D_MODEL)], axis=0)
    cv8 = jnp.zeros((8, D_MODEL), F32).at[0].set(c_ctx).at[1:1 + DB].set(c)
    mod = _modulation(cv8, w_ada, b_ada)

    conv, q, k, v, og, gates, gates_t = _in_proj(
        x_all, mod, g_norm1, w_in, b_igate, b_fgate, conv_w, conv_b,
        tm=tm, n_ctx_tok=n_ctx_tok, ctx_len=S, lat_len=DS)

    def ext_state(C, n_, m_):
        nb_ = jnp.broadcast_to(n_[..., None], n_.shape + (DH,))
        cext = jnp.concatenate([C, nb_], axis=-1).reshape(-1, ng, DH, 2 * DH)
        mext = jnp.broadcast_to(m_.reshape(-1, ng, 1), (C.shape[0], ng, LANES))
        return cext, mext

    z_c, z_m = ext_state(jnp.zeros((B, N_DIRS, N_HEADS, DH, DH), F32), jnp.zeros((B, N_DIRS, N_HEADS, DH), F32),
                         jnp.zeros((B, N_DIRS, N_HEADS), F32))
    hf_c, hb_c, c_fin, m_fin = _mlstm(q, k, v, gates, gates_t, z_c, z_m, tok0=0, nseq=B, seq_len=S, tc=S,
                                      write_state=True)
    l_c, l_m = ext_state(state_C, state_n, state_m)
    hf_l, hb_l = _mlstm(q, k, v, gates, gates_t, l_c, l_m, tok0=n_ctx_tok, nseq=DB, seq_len=DS, tc=tc_lat,
                        write_state=False)
    hf = jnp.concatenate([hf_c, hf_l], axis=0)
    hb = jnp.concatenate([hb_c, hb_l], axis=0)

    x1, h2, scores = _mix(x_all, conv, hf, hb, og, mod, g_mlstm, w_out, g_norm2, peer_wq, peer_k1, peer_k2,
                          tm=tm, n_ctx_tok=n_ctx_tok, lat_len=DS)
    r2, p2, cc, p1 = _select(scores)
    y = _peer(h2.T, peer_u.astype(BF16), peer_v.astype(BF16).T, r2, p2, cc, p1, x1, mod, g_final,
              tb=tb, n_ctx_tok=n_ctx_tok, lat_len=DS)

    y_prompt = y[:n_ctx_tok].reshape(B, S, D_MODEL)
    y_sample = y[n_ctx_tok:].reshape(DB, DS, D_MODEL)
    new_C = c_fin[..., :DH].reshape(B, 1, N_DIRS, N_HEADS, DH, DH)
    new_n = c_fin[..., DH].reshape(B, 1, N_DIRS, N_HEADS, DH)
    new_m = m_fin[..., 0].reshape(B, 1, N_DIRS, N_HEADS)
    return y_prompt, y_sample, new_C, new_n, new_m


def kernel(x_prompt, x_sample, state_C, state_n, state_m, c, c_ctx, w_ada, b_ada, g_norm1, w_in, b_igate, b_fgate,
           conv_w, conv_b, g_mlstm, w_out, g_norm2, peer_wq, peer_k1, peer_k2, peer_u, peer_v, g_final):
    return _layer(x_prompt, x_sample, state_C[:, 0], state_n[:, 0], state_m[:, 0], c, c_ctx, w_ada[0], b_ada[0],
                  g_norm1[0], w_in[0], b_igate[0], b_fgate[0], conv_w[0], conv_b[0], g_mlstm[0], w_out[0],
                  g_norm2[0], peer_wq[0], peer_k1[0], peer_k2[0], peer_u[0], peer_v[0], g_final,
                  tm=512, tc_lat=256, tb=512)
```

```python
import functools

import numpy as np
import jax
import jax.numpy as jnp
from jax import lax
from jax.experimental import pallas as pl
from jax.experimental.pallas import tpu as pltpu

F32 = jnp.float32
BF16 = jnp.bfloat16

D_MODEL = 1024
CONV_W = 512
N_HEADS = 4
DH = 128
MLSTM_W = N_HEADS * DH
N_DIRS = 2
GRID_W = 64
N_GROUPS = 7
GATE_COL0 = N_GROUPS * 512
N_GATES = 2 * N_DIRS * N_HEADS
PEER_HEADS = 8
NKEYS = 128
TOPK = 16
N_EXPERTS = NKEYS * NKEYS
EPS = 1e-6
LANES = 128
NEG_INF = float("-inf")

VMEM_LIMIT = 56 * 1024 * 1024


def _cparams(sem):
    return pltpu.CompilerParams(dimension_semantics=sem, vmem_limit_bytes=VMEM_LIMIT)


def _mod_kernel(cv_ref, w_ref, b_ref, o_ref):
    cv = cv_ref[...]
    s = cv * (1.0 / (1.0 + jnp.exp(-cv)))
    o_ref[...] = jnp.dot(s, w_ref[...], preferred_element_type=F32) + b_ref[...]


def _modulation(cv8, w_ada, b_ada):
    n = w_ada.shape[1]
    tn = 1024
    return pl.pallas_call(
        _mod_kernel,
        grid=(n // tn,),
        in_specs=[pl.BlockSpec((8, D_MODEL), lambda j: (0, 0)),
                  pl.BlockSpec((D_MODEL, tn), lambda j: (0, j)),
                  pl.BlockSpec((1, tn), lambda j: (0, j))],
        out_specs=pl.BlockSpec((8, tn), lambda j: (0, j)),
        out_shape=jax.ShapeDtypeStruct((8, n), F32),
        compiler_params=_cparams(("arbitrary",)),
        name="modulation",
    )(cv8, w_ada, b_ada.reshape(1, n))


def _mod_row(i, tm, n_ctx_tok, lat_len):
    n_ctx_tiles = n_ctx_tok // tm
    tiles_per_seq = lat_len // tm
    is_ctx = i < n_ctx_tiles
    row = jnp.where(is_ctx, 0, 1 + (i - n_ctx_tiles) // tiles_per_seq)
    return is_ctx, row


def _group_specs(tm, width, n_ctx_tok):
    nct = n_ctx_tok // tm
    return [pl.BlockSpec((tm, width), lambda i: (jnp.minimum(i, nct - 1), 0)),
            pl.BlockSpec((tm, width), lambda i: (jnp.maximum(i - nct, 0), 0))]


def _rms(x):
    return x * lax.rsqrt(jnp.mean(x * x, axis=-1, keepdims=True) + EPS)


def _log_sigmoid(z):
    return jnp.minimum(z, 0.0) - jnp.log(1.0 + jnp.exp(-jnp.abs(z)))


def _inproj_kernel(xc_ref, xl_ref, mod_ref, g1_ref, win_ref, wg_ref, wgt_ref, bg_ref, bgt_ref, cw_ref, cb_ref,
                   conv_ref, q_ref, k_ref, v_ref, o_ref, gates_ref, gatest_ref,
                   *, tm, n_ctx_tok, ctx_len, lat_len):
    i = pl.program_id(0)
    is_ctx, row = _mod_row(i, tm, n_ctx_tok, lat_len)
    sh1 = mod_ref[pl.ds(row, 1), 0:D_MODEL]
    sc1 = mod_ref[pl.ds(row, 1), D_MODEL:2 * D_MODEL]
    x = jnp.where(is_ctx, xc_ref[...], xl_ref[...])
    h = _rms(x) * g1_ref[...] * (1.0 + sc1) + sh1
    hb = h.astype(BF16)

    def grp(g):
        return jnp.dot(hb, win_ref[:, g * 512:(g + 1) * 512], preferred_element_type=F32)

    u = grp(1) * grp(2)
    pm = jnp.where(is_ctx, ctx_len - 1, GRID_W - 1)
    pos = lax.broadcasted_iota(jnp.int32, (tm, 1), 0) & pm
    u_prev = jnp.where(pos == 0, 0.0, pltpu.roll(u, 1, axis=0))
    u_next = jnp.where(pos == pm, 0.0, pltpu.roll(u, tm - 1, axis=0))
    y = u_prev * cw_ref[0:1, :] + u * cw_ref[1:2, :] + u_next * cw_ref[2:3, :] + cb_ref[...]
    conv_ref[...] = (grp(0) * y).astype(BF16)

    q_ref[...] = (grp(3) * (DH ** -0.5)).astype(BF16)
    k_ref[...] = grp(4).astype(BF16)
    v_ref[...] = grp(5).astype(BF16)
    o_ref[...] = grp(6)

    z = jnp.dot(hb, wg_ref[...], preferred_element_type=F32) + bg_ref[...]
    lane = lax.broadcasted_iota(jnp.int32, z.shape, 1)
    gates_ref[...] = jnp.where(lane >= N_DIRS * N_HEADS, _log_sigmoid(z), z)
    zt = lax.dot_general(wgt_ref[...], hb, (((1,), (1,)), ((), ())), preferred_element_type=F32) + bgt_ref[...]
    sub = lax.broadcasted_iota(jnp.int32, zt.shape, 0)
    gatest_ref[...] = jnp.where(sub >= N_DIRS * N_HEADS, _log_sigmoid(zt), zt)


def _in_proj(x_ctx, x_lat, mod, g1, w_in, b_igate, b_fgate, conv_w, conv_b, *, tm, n_ctx_tok, ctx_len, lat_len):
    T = x_ctx.shape[0] + x_lat.shape[0]
    win = w_in[:, :GATE_COL0].astype(BF16)
    wg = jnp.zeros((D_MODEL, LANES), F32).at[:, :N_GATES].set(w_in[:, GATE_COL0:]).astype(BF16)
    bgate = jnp.zeros((1, LANES), F32).at[0, :N_GATES].set(
        jnp.concatenate([b_igate.reshape(-1), b_fgate.reshape(-1)]))
    tok = lambda w: pl.BlockSpec((tm, w), lambda i: (i, 0))
    full = lambda a: pl.BlockSpec(a.shape, lambda i: (0,) * a.ndim)
    args = (x_ctx, x_lat, mod, g1.reshape(1, -1), win, wg, wg.T, bgate, bgate.T, conv_w, conv_b.reshape(1, -1))
    return pl.pallas_call(
        functools.partial(_inproj_kernel, tm=tm, n_ctx_tok=n_ctx_tok, ctx_len=ctx_len, lat_len=lat_len),
        grid=(T // tm,),
        in_specs=_group_specs(tm, D_MODEL, n_ctx_tok) + [full(a) for a in args[2:]],
        out_specs=[tok(CONV_W), tok(MLSTM_W), tok(MLSTM_W), tok(MLSTM_W), tok(MLSTM_W), tok(LANES),
                   pl.BlockSpec((LANES, tm), lambda i: (0, i))],
        out_shape=[jax.ShapeDtypeStruct((T, CONV_W), BF16),
                   jax.ShapeDtypeStruct((T, MLSTM_W), BF16),
                   jax.ShapeDtypeStruct((T, MLSTM_W), BF16),
                   jax.ShapeDtypeStruct((T, MLSTM_W), BF16),
                   jax.ShapeDtypeStruct((T, MLSTM_W), F32),
                   jax.ShapeDtypeStruct((T, LANES), F32),
                   jax.ShapeDtypeStruct((LANES, T), F32)],
        compiler_params=_cparams(("arbitrary",)),
        name="in_proj",
    )(*args)


def _split_bf16(x):
    hi = x.astype(BF16)
    return hi, (x - hi.astype(F32)).astype(BF16)


def _mlstm_kernel(qf_ref, kf_ref, vf_ref, gf_ref, gtf_ref, qb_ref, kb_ref, vb_ref, gb_ref, gtb_ref,
                  c0_ref, m0_ref, hf_ref, hb_ref, *rest, tc, write_state):
    if write_state:
        cout_ref, mout_ref, c_scr, m_scr = rest
    else:
        c_scr, m_scr = rest
    c = pl.program_id(1)

    @pl.when(c == 0)
    def _():
        c_scr[...] = c0_ref[...]
        m_scr[...] = m0_ref[...]

    r_io = lax.broadcasted_iota(jnp.int32, (tc, tc), 0)
    c_io = lax.broadcasted_iota(jnp.int32, (tc, tc), 1)
    ones = jnp.ones((tc, DH), BF16)

    for d in range(N_DIRS):
        q_ref, k_ref, v_ref, g_ref, gt_ref, h_ref = (
            (qf_ref, kf_ref, vf_ref, gf_ref, gtf_ref, hf_ref) if d == 0 else
            (qb_ref, kb_ref, vb_ref, gb_ref, gtb_ref, hb_ref))
        causal = (c_io <= r_io) if d == 0 else (c_io >= r_io)
        last = tc - 1 if d == 0 else 0
        tri = causal.astype(BF16)
        tri_t = ((r_io <= c_io) if d == 0 else (r_io >= c_io)).astype(BF16)
        g = g_ref[...]
        gt = gt_ref[...]
        g_hi, g_lo = _split_bf16(g)
        gt_hi, gt_lo = _split_bf16(gt)
        bcum = (jnp.dot(tri, g_hi, preferred_element_type=F32) + jnp.dot(tri, g_lo, preferred_element_type=F32))
        bcum_t = (jnp.dot(gt_hi, tri_t, preferred_element_type=F32)
                  + jnp.dot(gt_lo, tri_t, preferred_element_type=F32))
        for hd in range(N_HEADS):
            gi = d * N_HEADS + hd
            gfi = N_DIRS * N_HEADS + gi
            sl = slice(hd * DH, (hd + 1) * DH)
            q = q_ref[:, sl]
            k = k_ref[:, sl]
            v = v_ref[:, sl]
            ig_c = g[:, gi:gi + 1]
            ig_r = gt[gi:gi + 1, :]
            b_c = bcum[:, gfi:gfi + 1]
            b_r = bcum_t[gfi:gfi + 1, :]
            m_prev = m_scr[gi:gi + 1, 0:1]
            dmat = jnp.where(causal, b_c - b_r + ig_r, NEG_INF)
            inter = b_c + m_prev
            m_t = jnp.maximum(inter, jnp.max(dmat, axis=-1, keepdims=True))
            s = lax.dot_general(q, k, (((1,), (1,)), ((), ())), preferred_element_type=F32) * jnp.exp(dmat - m_t)
            a = jnp.exp(inter - m_t)
            cext = c_scr[gi]
            vext = jnp.concatenate([v, ones], axis=1)
            nd = (a * jnp.dot(q, cext.astype(BF16), preferred_element_type=F32)
                  + jnp.dot(s.astype(BF16), vext, preferred_element_type=F32))
            num = nd[:, :DH]
            den = nd[:, DH:]
            h_ref[:, sl] = num / jnp.maximum(jnp.abs(den), jnp.exp(-m_t))
            m_new = m_t[last:last + 1, :]
            b_last = b_c[last:last + 1, :]
            w_c = jnp.exp(b_last - b_c + ig_c - m_new)
            decay = jnp.exp(b_last + m_prev - m_new)
            kw_t = (k.astype(F32) * w_c).T.astype(BF16)
            c_scr[gi] = decay * cext + jnp.dot(kw_t, vext, preferred_element_type=F32)
            m_scr[gi:gi + 1, :] = jnp.broadcast_to(m_new, (1, LANES))

    if write_state:
        @pl.when(c == pl.num_programs(1) - 1)
        def _():
            cout_ref[...] = c_scr[...]
            mout_ref[...] = m_scr[...]


def _mlstm(q, k, v, gates, gates_t, c0ext, m0, *, tok0, nseq, seq_len, tc, write_state):
    nc = seq_len // tc
    blk0 = tok0 // tc
    fwd = lambda b, c: b * nc + c
    bwd = lambda b, c: b * nc + nc - 1 - c
    tokspec = lambda w, f, o: pl.BlockSpec((tc, w), lambda b, c: (o + f(b, c), 0))
    gtspec = lambda f: pl.BlockSpec((LANES, tc), lambda b, c: (0, blk0 + f(b, c)))
    ng = N_DIRS * N_HEADS
    in_specs = [tokspec(MLSTM_W, fwd, blk0)] * 3 + [tokspec(LANES, fwd, blk0), gtspec(fwd)] \
        + [tokspec(MLSTM_W, bwd, blk0)] * 3 + [tokspec(LANES, bwd, blk0), gtspec(bwd)] \
        + [pl.BlockSpec((None, ng, DH, 2 * DH), lambda b, c: (b, 0, 0, 0)),
           pl.BlockSpec((None, ng, LANES), lambda b, c: (b, 0, 0))]
    out_specs = [tokspec(MLSTM_W, fwd, 0), tokspec(MLSTM_W, bwd, 0)]
    out_shape = [jax.ShapeDtypeStruct((nseq * seq_len, MLSTM_W), F32)] * 2
    if write_state:
        out_specs += [pl.BlockSpec((None, ng, DH, 2 * DH), lambda b, c: (b, 0, 0, 0)),
                      pl.BlockSpec((None, ng, LANES), lambda b, c: (b, 0, 0))]
        out_shape += [jax.ShapeDtypeStruct((nseq, ng, DH, 2 * DH), F32),
                      jax.ShapeDtypeStruct((nseq, ng, LANES), F32)]
    return pl.pallas_call(
        functools.partial(_mlstm_kernel, tc=tc, write_state=write_state),
        grid=(nseq, nc),
        in_specs=in_specs,
        out_specs=out_specs,
        out_shape=out_shape,
        scratch_shapes=[pltpu.VMEM((ng, DH, 2 * DH), F32), pltpu.VMEM((ng, LANES), F32)],
        compiler_params=_cparams(("arbitrary", "arbitrary")),
        name="mlstm_ctx" if write_state else "mlstm_lat",
    )(q, k, v, gates, gates_t, q, k, v, gates, gates_t, c0ext, m0)


def _mix_kernel(xc_ref, xl_ref, conv_ref, hfc_ref, hfl_ref, hbc_ref, hbl_ref, og_ref, mod_ref, gm_ref, wout_ref,
                g2_ref, wq_ref, k1_ref, k2_ref, x1_ref, h2t_ref, sc_ref, *, tm, n_ctx_tok, lat_len):
    i = pl.program_id(0)
    is_ctx, row = _mod_row(i, tm, n_ctx_tok, lat_len)
    mrow = lambda j: mod_ref[pl.ds(row, 1), j * D_MODEL:(j + 1) * D_MODEL]
    hs = jnp.where(is_ctx, hfc_ref[...] + hbc_ref[...], hfl_ref[...] + hbl_ref[...])
    hn = jnp.concatenate([_rms(hs[:, hd * DH:(hd + 1) * DH]) for hd in range(N_HEADS)], axis=1)
    og = og_ref[...]
    ml = (hn * gm_ref[...] * (1.0 / (1.0 + jnp.exp(-og)))).astype(BF16)
    mix = (jnp.dot(conv_ref[...], wout_ref[0:CONV_W, :], preferred_element_type=F32)
           + jnp.dot(ml, wout_ref[CONV_W:, :], preferred_element_type=F32))
    x1 = jnp.where(is_ctx, xc_ref[...], xl_ref[...]) + mrow(2) * mix
    x1_ref[...] = x1
    h2f = _rms(x1) * g2_ref[...] * (1.0 + mrow(4)) + mrow(3)
    h2 = h2f.astype(BF16)
    h2t_ref[...] = h2f.T.astype(BF16)
    nt = (((1,), (1,)), ((), ()))
    for hd in range(PEER_HEADS):
        qh = jnp.dot(h2, wq_ref[:, hd * 2 * NKEYS:(hd + 1) * 2 * NKEYS], preferred_element_type=F32).astype(BF16)
        sc_ref[2 * hd] = lax.dot_general(k1_ref[...], qh[:, :NKEYS], nt, preferred_element_type=F32)
        sc_ref[2 * hd + 1] = lax.dot_general(k2_ref[...], qh[:, NKEYS:], nt, preferred_element_type=F32)


def _mix(x_ctx, x_lat, conv, hf_ctx, hf_lat, hb_ctx, hb_lat, og, mod, g_mlstm, w_out, g2, wq, k1, k2,
         *, tm, n_ctx_tok, lat_len):
    T = x_ctx.shape[0] + x_lat.shape[0]
    tok = lambda w: pl.BlockSpec((tm, w), lambda i: (i, 0))
    full = lambda a: pl.BlockSpec(a.shape, lambda i: (0,) * a.ndim)
    consts = (mod, g_mlstm.reshape(1, -1), w_out.astype(BF16), g2.reshape(1, -1), wq.astype(BF16),
              k1.astype(BF16), k2.astype(BF16))
    return pl.pallas_call(
        functools.partial(_mix_kernel, tm=tm, n_ctx_tok=n_ctx_tok, lat_len=lat_len),
        grid=(T // tm,),
        in_specs=_group_specs(tm, D_MODEL, n_ctx_tok) + [tok(CONV_W)] + _group_specs(tm, MLSTM_W, n_ctx_tok) * 2
        + [tok(MLSTM_W)] + [full(a) for a in consts],
        out_specs=[tok(D_MODEL), pl.BlockSpec((D_MODEL, tm), lambda i: (0, i)),
                   pl.BlockSpec((2 * PEER_HEADS, NKEYS, tm), lambda i: (0, 0, i))],
        out_shape=[jax.ShapeDtypeStruct((T, D_MODEL), F32),
                   jax.ShapeDtypeStruct((D_MODEL, T), BF16),
                   jax.ShapeDtypeStruct((2 * PEER_HEADS, NKEYS, T), F32)],
        compiler_params=_cparams(("arbitrary",)),
        name="mix",
    )(x_ctx, x_lat, conv, hf_ctx, hf_lat, hb_ctx, hb_lat, og, *consts)


def _sort16_pairs():
    def merge(lo, hi, r):
        step = r * 2
        if step < hi - lo:
            yield from merge(lo, hi, step)
            yield from merge(lo + r, hi, step)
            yield from ((i, i + r) for i in range(lo + r, hi - r, step))
        else:
            yield (lo, lo + r)

    def sort(lo, hi):
        if hi - lo >= 1:
            mid = lo + (hi - lo) // 2
            yield from sort(lo, mid)
            yield from sort(mid + 1, hi)
            yield from merge(lo, hi, 1)

    return tuple(sort(0, TOPK - 1))


_SORT16 = _sort16_pairs()
_CAND_PAIRS = tuple((r1, r2) for r1 in range(TOPK) for r2 in range(TOPK) if (r1 + 1) * (r2 + 1) <= TOPK)


def _cmpx(x, i, j):
    a, b = x[i], x[j]
    if b is None:
        return
    if a is None:
        x[i], x[j] = b, None
        return
    x[i], x[j] = jnp.maximum(a, b), jnp.minimum(a, b)


def _sort16(x):
    x = list(x)
    for i, j in _SORT16:
        _cmpx(x, i, j)
    return x


def _merge_top16(a, b):
    x = []
    for k in range(TOPK):
        p, q = a[k], b[TOPK - 1 - k]
        x.append(q if p is None else p if q is None else jnp.maximum(p, q))
    d = TOPK // 2
    while d:
        for k in range(TOPK):
            if not k & d:
                _cmpx(x, k, k + d)
        d //= 2
    return x


def _top16_values(groups):
    groups = [_sort16(g) for g in groups]
    while len(groups) > 1:
        groups = [_merge_top16(groups[k], groups[k + 1]) for k in range(0, len(groups), 2)]
    return groups[0]


def _count_gt(vs, s):
    g = jnp.where(vs[0] > s, 1.0, 0.0)
    for r in range(1, len(vs)):
        g = jnp.where(vs[r] > s, float(r + 1), g)
    return g


def _xpose8(x):
    x = list(x)
    sub = lax.broadcasted_iota(jnp.int32, (8, LANES), 0)
    for d in (4, 2, 1):
        keep = (sub & d) == 0
        for a in range(8):
            if not a & d:
                xa, xb = x[a], x[a + d]
                x[a] = jnp.where(keep, xa, pltpu.roll(xb, d, axis=0))
                x[a + d] = jnp.where(keep, pltpu.roll(xa, 8 - d, axis=0), xb)
    return x


def _select_kernel(sc_ref, r2_ref, p2_ref, c_ref, p1_ref, s_ref, t_scr):
    one = lambda m: jnp.where(m, 1.0, 0.0)
    lane_tile = lambda r: slice(r * LANES, (r + 1) * LANES)
    for h in range(2):
        for kg in range(NKEYS // 8):
            tiles = _xpose8([sc_ref[h, kg * 8:(kg + 1) * 8, lane_tile(r)] for r in range(8)])
            for k in range(8):
                s_ref[h, kg * 8 + k] = tiles[k]
    v = [_top16_values([[s_ref[h, g * TOPK + k] for k in range(TOPK)] for g in range(NKEYS // TOPK)])
         for h in range(2)]
    v1, v2 = v

    tie = jnp.zeros((8, LANES), F32)
    for h in range(2):
        n_ge = one(s_ref[h, 0] >= v[h][TOPK - 1])
        for j in range(1, NKEYS):
            n_ge = n_ge + one(s_ref[h, j] >= v[h][TOPK - 1])
        tie = jnp.maximum(tie, one(n_ge > float(TOPK)))
        for r in range(TOPK - 1):
            tie = jnp.maximum(tie, one(v[h][r] == v[h][r + 1]))
    t_scr[...] = jnp.zeros_like(t_scr)

    @pl.when(jnp.max(tie) > 0.0)
    def _():
        def lower_equal(jp, carry):
            for h in range(2):
                sv = s_ref[h, jp]
                for j in range(1, NKEYS):
                    inc = jnp.where(jp < j, 1.0, 0.0)
                    t_scr[h, j] = t_scr[h, j] + jnp.where(s_ref[h, j] == sv, inc, 0.0)
            return carry
        lax.fori_loop(0, NKEYS - 1, lower_equal, 0)

    cand = {p: v1[p[0]] + v2[p[1]] for p in _CAND_PAIRS}
    rest = [cand[p] for p in _CAND_PAIRS if p[0] > 0]
    rest += [None] * (-len(rest) % TOPK)
    groups = [[cand[(0, r2)] for r2 in range(TOPK)]] + [rest[k:k + TOPK] for k in range(0, len(rest), TOPK)]
    while len(groups) & (len(groups) - 1):
        groups.append([None] * TOPK)
    w = _top16_values(groups)
    tau = w[TOPK - 1]
    n_gt = sum(one(cand[p] > tau) for p in _CAND_PAIRS)
    need = float(TOPK) - n_gt
    eq_seen = jnp.zeros((8, LANES), F32)
    counts = [jnp.zeros((8, LANES), F32) for _ in range(TOPK)]
    for p in _CAND_PAIRS:
        eq = cand[p] == tau
        counts[p[0]] = counts[p[0]] + one((cand[p] > tau) | (eq & (eq_seen < need)))
        eq_seen = eq_seen + one(eq)
    z = jnp.ones((8, LANES), F32)
    for r in range(1, TOPK):
        z = z + jnp.exp(w[r] - w[0])
    inv_z = 1.0 / z

    for kg in range(NKEYS // 8):
        tabs = [[], [], [], []]
        for j in range(kg * 8, (kg + 1) * 8):
            s1 = s_ref[0, j]
            s2 = s_ref[1, j]
            rank1 = _count_gt(v1, s1) + t_scr[0, j]
            ckey = jnp.zeros((8, LANES), F32)
            for r in range(TOPK):
                ckey = jnp.where(rank1 == float(r), counts[r], ckey)
            tabs[0].append(_count_gt(v2, s2) + t_scr[1, j])
            tabs[1].append(jnp.exp(s2 - v2[0]))
            tabs[2].append(ckey)
            tabs[3].append(jnp.exp(s1 - v1[0]) * inv_z)
        for ref, tab in zip((r2_ref, p2_ref, c_ref, p1_ref), tabs):
            for r, tile in enumerate(_xpose8(tab)):
                ref[kg * 8:(kg + 1) * 8, lane_tile(r)] = tile


def _select(scores):
    T = scores.shape[-1]
    tt = 8 * LANES
    tab = pl.BlockSpec((None, NKEYS, tt), lambda i, hd: (hd, 0, i))
    return pl.pallas_call(
        _select_kernel,
        grid=(T // tt, PEER_HEADS),
        in_specs=[pl.BlockSpec((2, NKEYS, tt), lambda i, hd: (hd, 0, i))],
        out_specs=[tab] * 4,
        out_shape=[jax.ShapeDtypeStruct((PEER_HEADS, NKEYS, T), F32)] * 4,
        scratch_shapes=[pltpu.VMEM((2, NKEYS, 8, LANES), F32), pltpu.VMEM((2, NKEYS, 8, LANES), F32)],
        compiler_params=_cparams(("arbitrary", "arbitrary")),
        name="select",
    )(scores)


_SQRT_HALF = 0.7071067811865476
_JROWS = 16
_PEER_NB = 8 * NKEYS
_PEER_STEPS = N_EXPERTS // _PEER_NB


def _peer_kernel(h2t_ref, u_ref, vt_ref, r2_ref, p2_ref, c_ref, p1_ref, x1_ref, mod_ref, gf_ref, yc_ref, yl_ref,
                 a_scr, w_scr, acc_scr, tab_scr, *, tb, n_ctx_tok, lat_len):
    L = pl.program_id(0)
    n_lg = tb // LANES
    vpu_blk = L - 1
    acc_blk = L - 2

    @pl.when(L == 0)
    def _():
        a_scr[...] = jnp.zeros_like(a_scr)
        w_scr[...] = jnp.zeros_like(w_scr)
        acc_scr[...] = jnp.zeros_like(acc_scr)

    @pl.when((L == 0) | (vpu_blk % _PEER_STEPS == 0))
    def _():
        for jg in range(NKEYS // _JROWS):
            rows = slice(jg * _JROWS, (jg + 1) * _JROWS)
            for lg in range(n_lg):
                lanes = slice(lg * LANES, (lg + 1) * LANES)
                for hd in range(PEER_HEADS):
                    tab_scr[jg, lg, hd * _JROWS:(hd + 1) * _JROWS, :] = r2_ref[hd, rows, lanes].astype(BF16)
                    tab_scr[jg, lg, (PEER_HEADS + hd) * _JROWS:(PEER_HEADS + hd + 1) * _JROWS, :] = \
                        p2_ref[hd, rows, lanes].astype(BF16)

    @pl.when((acc_blk >= 0) & (acc_blk % _PEER_STEPS == 0))
    def _():
        acc_scr[...] = jnp.zeros_like(acc_scr)

    def stages(cur):
        prv = 1 - cur
        a_scr[cur] = jnp.dot(u_ref[...], h2t_ref[...], preferred_element_type=F32)
        acc_scr[...] += jnp.dot(vt_ref[...], w_scr[cur], preferred_element_type=F32)
        for lg in range(n_lg):
            lanes = slice(lg * LANES, (lg + 1) * LANES)
            for i8 in range(_PEER_NB // NKEYS):
                bcast = lambda ref, hd: jnp.broadcast_to(ref[hd, i8:i8 + 1, lanes], (_JROWS, LANES)).astype(BF16)
                crow = [bcast(c_ref, hd) for hd in range(PEER_HEADS)]
                p1row = [bcast(p1_ref, hd) for hd in range(PEER_HEADS)]
                for jg in range(NKEYS // _JROWS):
                    rows = slice(jg * _JROWS, (jg + 1) * _JROWS)
                    arow = slice(i8 * NKEYS + jg * _JROWS, i8 * NKEYS + (jg + 1) * _JROWS)
                    a = a_scr[prv, arow, lanes]
                    g = jnp.zeros((_JROWS, LANES), BF16)
                    for hd in range(PEER_HEADS):
                        r2 = tab_scr[jg, lg, hd * _JROWS:(hd + 1) * _JROWS, :]
                        p2 = tab_scr[jg, lg, (PEER_HEADS + hd) * _JROWS:(PEER_HEADS + hd + 1) * _JROWS, :]
                        g = g + jnp.where(r2 < crow[hd], p2, jnp.zeros_like(p2)) * p1row[hd]
                    gelu = 0.5 * a * (1.0 + lax.erf(a * _SQRT_HALF))
                    w_scr[prv, arow, lanes] = g * gelu.astype(BF16)

    for slot in range(2):
        pl.when(L % 2 == slot)(functools.partial(stages, slot))

    @pl.when((acc_blk >= 0) & (acc_blk % _PEER_STEPS == _PEER_STEPS - 1))
    def _():
        is_ctx, row = _mod_row(acc_blk // _PEER_STEPS, tb, n_ctx_tok, lat_len)
        ga2 = mod_ref[pl.ds(row, 1), 5 * D_MODEL:6 * D_MODEL]
        x2 = x1_ref[...] + ga2 * acc_scr[...].T
        y = _rms(x2) * gf_ref[...]

        @pl.when(is_ctx)
        def _():
            yc_ref[...] = y

        @pl.when(jnp.logical_not(is_ctx))
        def _():
            yl_ref[...] = y


def _peer(h2t, u_bf, vt_bf, r2, p2, cc, p1, x1, mod, g_final, *, tb, n_ctx_tok, lat_len):
    T = h2t.shape[1]
    nb = _PEER_NB
    n_tiles = T // tb
    nct = n_ctx_tok // tb
    n_blocks = n_tiles * _PEER_STEPS
    blk = lambda L, lag: jnp.clip(L - lag, 0, n_blocks - 1)
    tile = lambda L, lag: blk(L, lag) // _PEER_STEPS
    step = lambda L, lag: blk(L, lag) % _PEER_STEPS
    tab = pl.BlockSpec((PEER_HEADS, NKEYS, tb), lambda L: (0, 0, tile(L, 1)))
    rowtab = pl.BlockSpec((PEER_HEADS, nb // NKEYS, tb), lambda L: (0, step(L, 1), tile(L, 1)))
    return pl.pallas_call(
        functools.partial(_peer_kernel, tb=tb, n_ctx_tok=n_ctx_tok, lat_len=lat_len),
        grid=(n_blocks + 2,),
        in_specs=[pl.BlockSpec((D_MODEL, tb), lambda L: (0, tile(L, 0))),
                  pl.BlockSpec((nb, D_MODEL), lambda L: (step(L, 0), 0)),
                  pl.BlockSpec((D_MODEL, nb), lambda L: (0, step(L, 2))),
                  tab, tab, rowtab, rowtab,
                  pl.BlockSpec((tb, D_MODEL), lambda L: (tile(L, 2), 0)),
                  pl.BlockSpec(mod.shape, lambda L: (0, 0)),
                  pl.BlockSpec((1, D_MODEL), lambda L: (0, 0))],
        out_specs=[pl.BlockSpec((tb, D_MODEL), lambda L: (jnp.minimum(tile(L, 2), nct - 1), 0)),
                   pl.BlockSpec((tb, D_MODEL), lambda L: (jnp.maximum(tile(L, 2) - nct, 0), 0))],
        out_shape=[jax.ShapeDtypeStruct((n_ctx_tok, D_MODEL), F32),
                   jax.ShapeDtypeStruct((T - n_ctx_tok, D_MODEL), F32)],
        scratch_shapes=[pltpu.VMEM((2, nb, tb), F32), pltpu.VMEM((2, nb, tb), BF16), pltpu.VMEM((D_MODEL, tb), F32),
                        pltpu.VMEM((NKEYS // _JROWS, tb // LANES, 2 * PEER_HEADS * _JROWS, LANES), BF16)],
        compiler_params=_cparams(("arbitrary",)),
        name="peer",
    )(h2t, u_bf, vt_bf, r2, p2, cc, p1, x1, mod, g_final.reshape(1, -1))


def _layer(x_prompt, x_sample, state_C, state_n, state_m, c, c_ctx, w_ada, b_ada, g_norm1, w_in, b_igate, b_fgate,
           conv_w, conv_b, g_mlstm, w_out, g_norm2, peer_wq, peer_k1, peer_k2, peer_u, peer_v, g_final,
           *, tm, tc_lat, tb):
    B, S, _ = x_prompt.shape
    DB, DS, _ = x_sample.shape
    n_ctx_tok = B * S
    ng = N_DIRS * N_HEADS
    x_ctx = x_prompt.reshape(n_ctx_tok, D_MODEL)
    x_lat = x_sample.reshape(DB * DS, D_MODEL)
    cv8 = jnp.zeros((8, D_MODEL), F32).at[0].set(c_ctx).at[1:1 + DB].set(c)
    mod = _modulation(cv8, w_ada, b_ada)

    conv, q, k, v, og, gates, gates_t = _in_proj(
        x_ctx, x_lat, mod, g_norm1, w_in, b_igate, b_fgate, conv_w, conv_b,
        tm=tm, n_ctx_tok=n_ctx_tok, ctx_len=S, lat_len=DS)

    def ext_state(C, n_, m_):
        nb_ = jnp.broadcast_to(n_[..., None], n_.shape + (DH,))
        cext = jnp.concatenate([C, nb_], axis=-1).reshape(-1, ng, DH, 2 * DH)
        mext = jnp.broadcast_to(m_.reshape(-1, ng, 1), (C.shape[0], ng, LANES))
        return cext, mext

    z_c, z_m = ext_state(jnp.zeros((B, N_DIRS, N_HEADS, DH, DH), F32), jnp.zeros((B, N_DIRS, N_HEADS, DH), F32),
                         jnp.zeros((B, N_DIRS, N_HEADS), F32))
    hf_c, hb_c, c_fin, m_fin = _mlstm(q, k, v, gates, gates_t, z_c, z_m, tok0=0, nseq=B, seq_len=S, tc=S,
                                      write_state=True)
    l_c, l_m = ext_state(state_C, state_n, state_m)
    hf_l, hb_l = _mlstm(q, k, v, gates, gates_t, l_c, l_m, tok0=n_ctx_tok, nseq=DB, seq_len=DS, tc=tc_lat,
                        write_state=False)

    x1, h2t, scores = _mix(x_ctx, x_lat, conv, hf_c, hf_l, hb_c, hb_l, og, mod, g_mlstm, w_out, g_norm2,
                           peer_wq, peer_k1, peer_k2, tm=tm, n_ctx_tok=n_ctx_tok, lat_len=DS)
    r2, p2, cc, p1 = _select(scores)
    y_ctx, y_lat = _peer(h2t, peer_u.astype(BF16), peer_v.astype(BF16).T, r2, p2, cc, p1, x1, mod, g_final,
                         tb=tb, n_ctx_tok=n_ctx_tok, lat_len=DS)

    y_prompt = y_ctx.reshape(B, S, D_MODEL)
    y_sample = y_lat.reshape(DB, DS, D_MODEL)
    new_C = c_fin[..., :DH].reshape(B, 1, N_DIRS, N_HEADS, DH, DH)
    new_n = c_fin[..., DH].reshape(B, 1, N_DIRS, N_HEADS, DH)
    new_m = m_fin[..., 0].reshape(B, 1, N_DIRS, N_HEADS)
    return y_prompt, y_sample, new_C, new_n, new_m


def kernel(x_prompt, x_sample, state_C, state_n, state_m, c, c_ctx, w_ada, b_ada, g_norm1, w_in, b_igate, b_fgate,
           conv_w, conv_b, g_mlstm, w_out, g_norm2, peer_wq, peer_k1, peer_k2, peer_u, peer_v, g_final):
    return _layer(x_prompt, x_sample, state_C[:, 0], state_n[:, 0], state_m[:, 0], c, c_ctx, w_ada[0], b_ada[0],
                  g_norm1[0], w_in[0], b_igate[0], b_fgate[0], conv_w[0], conv_b[0], g_mlstm[0], w_out[0],
                  g_norm2[0], peer_wq[0], peer_k1[0], peer_k2[0], peer_u[0], peer_v[0], g_final,
                  tm=512, tc_lat=256, tb=512)
```

```python
import functools

import numpy as np
import jax
import jax.numpy as jnp
from jax import lax
from jax.experimental import pallas as pl
from jax.experimental.pallas import tpu as pltpu

F32 = jnp.float32
BF16 = jnp.bfloat16

D_MODEL = 1024
CONV_W = 512
N_HEADS = 4
DH = 128
MLSTM_W = N_HEADS * DH
N_DIRS = 2
GRID_W = 64
N_GROUPS = 7
GATE_COL0 = N_GROUPS * 512
N_GATES = 2 * N_DIRS * N_HEADS
PEER_HEADS = 8
NKEYS = 128
TOPK = 16
N_EXPERTS = NKEYS * NKEYS
EPS = 1e-6
LANES = 128
NEG_INF = float("-inf")

VMEM_LIMIT = 56 * 1024 * 1024


def _cparams(sem):
    return pltpu.CompilerParams(dimension_semantics=sem, vmem_limit_bytes=VMEM_LIMIT)


def _mod_kernel(cv_ref, w_ref, b_ref, o_ref):
    cv = cv_ref[...]
    s = cv * (1.0 / (1.0 + jnp.exp(-cv)))
    o_ref[...] = jnp.dot(s, w_ref[...], preferred_element_type=F32) + b_ref[...]


def _modulation(cv8, w_ada, b_ada):
    n = w_ada.shape[1]
    tn = 1024
    return pl.pallas_call(
        _mod_kernel,
        grid=(n // tn,),
        in_specs=[pl.BlockSpec((8, D_MODEL), lambda j: (0, 0)),
                  pl.BlockSpec((D_MODEL, tn), lambda j: (0, j)),
                  pl.BlockSpec((1, tn), lambda j: (0, j))],
        out_specs=pl.BlockSpec((8, tn), lambda j: (0, j)),
        out_shape=jax.ShapeDtypeStruct((8, n), F32),
        compiler_params=_cparams(("arbitrary",)),
        name="modulation",
    )(cv8, w_ada, b_ada.reshape(1, n))


def _mod_row(i, tm, n_ctx_tok, lat_len):
    n_ctx_tiles = n_ctx_tok // tm
    tiles_per_seq = lat_len // tm
    is_ctx = i < n_ctx_tiles
    row = jnp.where(is_ctx, 0, 1 + (i - n_ctx_tiles) // tiles_per_seq)
    return is_ctx, row


def _group_specs(tm, width, n_ctx_tok):
    nct = n_ctx_tok // tm
    return [pl.BlockSpec((tm, width), lambda i: (jnp.minimum(i, nct - 1), 0)),
            pl.BlockSpec((tm, width), lambda i: (jnp.maximum(i - nct, 0), 0))]


def _rms(x):
    return x * lax.rsqrt(jnp.mean(x * x, axis=-1, keepdims=True) + EPS)


def _log_sigmoid(z):
    return jnp.minimum(z, 0.0) - jnp.log(1.0 + jnp.exp(-jnp.abs(z)))


def _inproj_kernel(xc_ref, xl_ref, mod_ref, g1_ref, win_ref, wg_ref, wgt_ref, bg_ref, bgt_ref, cw_ref, cb_ref,
                   conv_ref, q_ref, k_ref, v_ref, o_ref, gates_ref, gatest_ref,
                   *, tm, n_ctx_tok, ctx_len, lat_len):
    i = pl.program_id(0)
    is_ctx, row = _mod_row(i, tm, n_ctx_tok, lat_len)
    sh1 = mod_ref[pl.ds(row, 1), 0:D_MODEL]
    sc1 = mod_ref[pl.ds(row, 1), D_MODEL:2 * D_MODEL]
    x = jnp.where(is_ctx, xc_ref[...], xl_ref[...])
    h = _rms(x) * g1_ref[...] * (1.0 + sc1) + sh1
    hb = h.astype(BF16)

    def grp(g):
        return jnp.dot(hb, win_ref[:, g * 512:(g + 1) * 512], preferred_element_type=F32)

    u = grp(1) * grp(2)
    pm = jnp.where(is_ctx, ctx_len - 1, GRID_W - 1)
    pos = lax.broadcasted_iota(jnp.int32, (tm, 1), 0) & pm
    u_prev = jnp.where(pos == 0, 0.0, pltpu.roll(u, 1, axis=0))
    u_next = jnp.where(pos == pm, 0.0, pltpu.roll(u, tm - 1, axis=0))
    y = u_prev * cw_ref[0:1, :] + u * cw_ref[1:2, :] + u_next * cw_ref[2:3, :] + cb_ref[...]
    conv_ref[...] = (grp(0) * y).astype(BF16)

    q_ref[...] = (grp(3) * (DH ** -0.5)).astype(BF16)
    k_ref[...] = grp(4).astype(BF16)
    v_ref[...] = grp(5).astype(BF16)
    o_ref[...] = grp(6)

    z = jnp.dot(hb, wg_ref[...], preferred_element_type=F32) + bg_ref[...]
    lane = lax.broadcasted_iota(jnp.int32, z.shape, 1)
    gates_ref[...] = jnp.where(lane >= N_DIRS * N_HEADS, _log_sigmoid(z), z)
    zt = lax.dot_general(wgt_ref[...], hb, (((1,), (1,)), ((), ())), preferred_element_type=F32) + bgt_ref[...]
    sub = lax.broadcasted_iota(jnp.int32, zt.shape, 0)
    gatest_ref[...] = jnp.where(sub >= N_DIRS * N_HEADS, _log_sigmoid(zt), zt)


def _in_proj(x_ctx, x_lat, mod, g1, w_in, b_igate, b_fgate, conv_w, conv_b, *, tm, n_ctx_tok, ctx_len, lat_len):
    T = x_ctx.shape[0] + x_lat.shape[0]
    win = w_in[:, :GATE_COL0].astype(BF16)
    wg = jnp.zeros((D_MODEL, LANES), F32).at[:, :N_GATES].set(w_in[:, GATE_COL0:]).astype(BF16)
    bgate = jnp.zeros((1, LANES), F32).at[0, :N_GATES].set(
        jnp.concatenate([b_igate.reshape(-1), b_fgate.reshape(-1)]))
    tok = lambda w: pl.BlockSpec((tm, w), lambda i: (i, 0))
    full = lambda a: pl.BlockSpec(a.shape, lambda i: (0,) * a.ndim)
    args = (x_ctx, x_lat, mod, g1.reshape(1, -1), win, wg, wg.T, bgate, bgate.T, conv_w, conv_b.reshape(1, -1))
    return pl.pallas_call(
        functools.partial(_inproj_kernel, tm=tm, n_ctx_tok=n_ctx_tok, ctx_len=ctx_len, lat_len=lat_len),
        grid=(T // tm,),
        in_specs=_group_specs(tm, D_MODEL, n_ctx_tok) + [full(a) for a in args[2:]],
        out_specs=[tok(CONV_W), tok(MLSTM_W), tok(MLSTM_W), tok(MLSTM_W), tok(MLSTM_W), tok(LANES),
                   pl.BlockSpec((LANES, tm), lambda i: (0, i))],
        out_shape=[jax.ShapeDtypeStruct((T, CONV_W), BF16),
                   jax.ShapeDtypeStruct((T, MLSTM_W), BF16),
                   jax.ShapeDtypeStruct((T, MLSTM_W), BF16),
                   jax.ShapeDtypeStruct((T, MLSTM_W), BF16),
                   jax.ShapeDtypeStruct((T, MLSTM_W), F32),
                   jax.ShapeDtypeStruct((T, LANES), F32),
                   jax.ShapeDtypeStruct((LANES, T), F32)],
        compiler_params=_cparams(("arbitrary",)),
        name="in_proj",
    )(*args)


def _split_bf16(x):
    hi = x.astype(BF16)
    return hi, (x - hi.astype(F32)).astype(BF16)


def _mlstm_kernel(qf_ref, kf_ref, vf_ref, gf_ref, gtf_ref, qb_ref, kb_ref, vb_ref, gb_ref, gtb_ref,
                  c0_ref, m0_ref, hf_ref, hb_ref, *rest, tc, write_state):
    if write_state:
        cout_ref, mout_ref, c_scr, m_scr = rest
    else:
        c_scr, m_scr = rest
    c = pl.program_id(1)

    @pl.when(c == 0)
    def _():
        c_scr[...] = c0_ref[...]
        m_scr[...] = m0_ref[...]

    r_io = lax.broadcasted_iota(jnp.int32, (tc, tc), 0)
    c_io = lax.broadcasted_iota(jnp.int32, (tc, tc), 1)
    ones = jnp.ones((tc, DH), BF16)

    for d in range(N_DIRS):
        q_ref, k_ref, v_ref, g_ref, gt_ref, h_ref = (
            (qf_ref, kf_ref, vf_ref, gf_ref, gtf_ref, hf_ref) if d == 0 else
            (qb_ref, kb_ref, vb_ref, gb_ref, gtb_ref, hb_ref))
        causal = (c_io <= r_io) if d == 0 else (c_io >= r_io)
        last = tc - 1 if d == 0 else 0
        tri = causal.astype(BF16)
        tri_t = ((r_io <= c_io) if d == 0 else (r_io >= c_io)).astype(BF16)
        g = g_ref[...]
        gt = gt_ref[...]
        g_hi, g_lo = _split_bf16(g)
        gt_hi, gt_lo = _split_bf16(gt)
        bcum = (jnp.dot(tri, g_hi, preferred_element_type=F32) + jnp.dot(tri, g_lo, preferred_element_type=F32))
        bcum_t = (jnp.dot(gt_hi, tri_t, preferred_element_type=F32)
                  + jnp.dot(gt_lo, tri_t, preferred_element_type=F32))
        for hd in range(N_HEADS):
            gi = d * N_HEADS + hd
            gfi = N_DIRS * N_HEADS + gi
            sl = slice(hd * DH, (hd + 1) * DH)
            q = q_ref[:, sl]
            k = k_ref[:, sl]
            v = v_ref[:, sl]
            ig_c = g[:, gi:gi + 1]
            ig_r = gt[gi:gi + 1, :]
            b_c = bcum[:, gfi:gfi + 1]
            b_r = bcum_t[gfi:gfi + 1, :]
            m_prev = m_scr[gi:gi + 1, 0:1]
            dmat = jnp.where(causal, b_c - b_r + ig_r, NEG_INF)
            inter = b_c + m_prev
            m_t = jnp.maximum(inter, jnp.max(dmat, axis=-1, keepdims=True))
            s = lax.dot_general(q, k, (((1,), (1,)), ((), ())), preferred_element_type=F32) * jnp.exp(dmat - m_t)
            a = jnp.exp(inter - m_t)
            cext = c_scr[gi]
            vext = jnp.concatenate([v, ones], axis=1)
            nd = (a * jnp.dot(q, cext.astype(BF16), preferred_element_type=F32)
                  + jnp.dot(s.astype(BF16), vext, preferred_element_type=F32))
            num = nd[:, :DH]
            den = nd[:, DH:]
            h_ref[:, sl] = num / jnp.maximum(jnp.abs(den), jnp.exp(-m_t))
            m_new = m_t[last:last + 1, :]
            b_last = b_c[last:last + 1, :]
            w_c = jnp.exp(b_last - b_c + ig_c - m_new)
            decay = jnp.exp(b_last + m_prev - m_new)
            kw_t = (k.astype(F32) * w_c).T.astype(BF16)
            c_scr[gi] = decay * cext + jnp.dot(kw_t, vext, preferred_element_type=F32)
            m_scr[gi:gi + 1, :] = jnp.broadcast_to(m_new, (1, LANES))

    if write_state:
        @pl.when(c == pl.num_programs(1) - 1)
        def _():
            cout_ref[...] = c_scr[...]
            mout_ref[...] = m_scr[...]


def _mlstm(q, k, v, gates, gates_t, c0ext, m0, *, tok0, nseq, seq_len, tc, write_state):
    nc = seq_len // tc
    blk0 = tok0 // tc
    fwd = lambda b, c: b * nc + c
    bwd = lambda b, c: b * nc + nc - 1 - c
    tokspec = lambda w, f, o: pl.BlockSpec((tc, w), lambda b, c: (o + f(b, c), 0))
    gtspec = lambda f: pl.BlockSpec((LANES, tc), lambda b, c: (0, blk0 + f(b, c)))
    ng = N_DIRS * N_HEADS
    in_specs = [tokspec(MLSTM_W, fwd, blk0)] * 3 + [tokspec(LANES, fwd, blk0), gtspec(fwd)] \
        + [tokspec(MLSTM_W, bwd, blk0)] * 3 + [tokspec(LANES, bwd, blk0), gtspec(bwd)] \
        + [pl.BlockSpec((None, ng, DH, 2 * DH), lambda b, c: (b, 0, 0, 0)),
           pl.BlockSpec((None, ng, LANES), lambda b, c: (b, 0, 0))]
    out_specs = [tokspec(MLSTM_W, fwd, 0), tokspec(MLSTM_W, bwd, 0)]
    out_shape = [jax.ShapeDtypeStruct((nseq * seq_len, MLSTM_W), F32)] * 2
    if write_state:
        out_specs += [pl.BlockSpec((None, ng, DH, 2 * DH), lambda b, c: (b, 0, 0, 0)),
                      pl.BlockSpec((None, ng, LANES), lambda b, c: (b, 0, 0))]
        out_shape += [jax.ShapeDtypeStruct((nseq, ng, DH, 2 * DH), F32),
                      jax.ShapeDtypeStruct((nseq, ng, LANES), F32)]
    return pl.pallas_call(
        functools.partial(_mlstm_kernel, tc=tc, write_state=write_state),
        grid=(nseq, nc),
        in_specs=in_specs,
        out_specs=out_specs,
        out_shape=out_shape,
        scratch_shapes=[pltpu.VMEM((ng, DH, 2 * DH), F32), pltpu.VMEM((ng, LANES), F32)],
        compiler_params=_cparams(("arbitrary", "arbitrary")),
        name="mlstm_ctx" if write_state else "mlstm_lat",
    )(q, k, v, gates, gates_t, q, k, v, gates, gates_t, c0ext, m0)


def _mix_kernel(xc_ref, xl_ref, conv_ref, hfc_ref, hfl_ref, hbc_ref, hbl_ref, og_ref, mod_ref, gm_ref, wout_ref,
                g2_ref, wq_ref, k1_ref, k2_ref, x1_ref, h2t_ref, sc_ref, *, tm, n_ctx_tok, lat_len):
    i = pl.program_id(0)
    is_ctx, row = _mod_row(i, tm, n_ctx_tok, lat_len)
    mrow = lambda j: mod_ref[pl.ds(row, 1), j * D_MODEL:(j + 1) * D_MODEL]
    hs = jnp.where(is_ctx, hfc_ref[...] + hbc_ref[...], hfl_ref[...] + hbl_ref[...])
    hn = jnp.concatenate([_rms(hs[:, hd * DH:(hd + 1) * DH]) for hd in range(N_HEADS)], axis=1)
    og = og_ref[...]
    ml = (hn * gm_ref[...] * (1.0 / (1.0 + jnp.exp(-og)))).astype(BF16)
    mix = (jnp.dot(conv_ref[...], wout_ref[0:CONV_W, :], preferred_element_type=F32)
           + jnp.dot(ml, wout_ref[CONV_W:, :], preferred_element_type=F32))
    x1 = jnp.where(is_ctx, xc_ref[...], xl_ref[...]) + mrow(2) * mix
    x1_ref[...] = x1
    h2f = _rms(x1) * g2_ref[...] * (1.0 + mrow(4)) + mrow(3)
    h2 = h2f.astype(BF16)
    h2t_ref[...] = h2f.T.astype(BF16)
    nt = (((1,), (1,)), ((), ()))
    for hd in range(PEER_HEADS):
        qh = jnp.dot(h2, wq_ref[:, hd * 2 * NKEYS:(hd + 1) * 2 * NKEYS], preferred_element_type=F32).astype(BF16)
        sc_ref[2 * hd] = lax.dot_general(k1_ref[...], qh[:, :NKEYS], nt, preferred_element_type=F32)
        sc_ref[2 * hd + 1] = lax.dot_general(k2_ref[...], qh[:, NKEYS:], nt, preferred_element_type=F32)


def _mix(x_ctx, x_lat, conv, hf_ctx, hf_lat, hb_ctx, hb_lat, og, mod, g_mlstm, w_out, g2, wq, k1, k2,
         *, tm, n_ctx_tok, lat_len):
    T = x_ctx.shape[0] + x_lat.shape[0]
    tok = lambda w: pl.BlockSpec((tm, w), lambda i: (i, 0))
    full = lambda a: pl.BlockSpec(a.shape, lambda i: (0,) * a.ndim)
    consts = (mod, g_mlstm.reshape(1, -1), w_out.astype(BF16), g2.reshape(1, -1), wq.astype(BF16),
              k1.astype(BF16), k2.astype(BF16))
    return pl.pallas_call(
        functools.partial(_mix_kernel, tm=tm, n_ctx_tok=n_ctx_tok, lat_len=lat_len),
        grid=(T // tm,),
        in_specs=_group_specs(tm, D_MODEL, n_ctx_tok) + [tok(CONV_W)] + _group_specs(tm, MLSTM_W, n_ctx_tok) * 2
        + [tok(MLSTM_W)] + [full(a) for a in consts],
        out_specs=[tok(D_MODEL), pl.BlockSpec((D_MODEL, tm), lambda i: (0, i)),
                   pl.BlockSpec((2 * PEER_HEADS, NKEYS, tm), lambda i: (0, 0, i))],
        out_shape=[jax.ShapeDtypeStruct((T, D_MODEL), F32),
                   jax.ShapeDtypeStruct((D_MODEL, T), BF16),
                   jax.ShapeDtypeStruct((2 * PEER_HEADS, NKEYS, T), F32)],
        compiler_params=_cparams(("arbitrary",)),
        name="mix",
    )(x_ctx, x_lat, conv, hf_ctx, hf_lat, hb_ctx, hb_lat, og, *consts)


def _sort16_pairs():
    def merge(lo, hi, r):
        step = r * 2
        if step < hi - lo:
            yield from merge(lo, hi, step)
            yield from merge(lo + r, hi, step)
            yield from ((i, i + r) for i in range(lo + r, hi - r, step))
        else:
            yield (lo, lo + r)

    def sort(lo, hi):
        if hi - lo >= 1:
            mid = lo + (hi - lo) // 2
            yield from sort(lo, mid)
            yield from sort(mid + 1, hi)
            yield from merge(lo, hi, 1)

    return tuple(sort(0, TOPK - 1))


_SORT16 = _sort16_pairs()
_CAND_PAIRS = tuple((r1, r2) for r1 in range(TOPK) for r2 in range(TOPK) if (r1 + 1) * (r2 + 1) <= TOPK)


def _cmpx(x, i, j):
    a, b = x[i], x[j]
    if b is None:
        return
    if a is None:
        x[i], x[j] = b, None
        return
    x[i], x[j] = jnp.maximum(a, b), jnp.minimum(a, b)


def _sort16(x):
    x = list(x)
    for i, j in _SORT16:
        _cmpx(x, i, j)
    return x


def _merge_top16(a, b):
    x = []
    for k in range(TOPK):
        p, q = a[k], b[TOPK - 1 - k]
        x.append(q if p is None else p if q is None else jnp.maximum(p, q))
    d = TOPK // 2
    while d:
        for k in range(TOPK):
            if not k & d:
                _cmpx(x, k, k + d)
        d //= 2
    return x


def _top16_values(groups):
    groups = [_sort16(g) for g in groups]
    while len(groups) > 1:
        groups = [_merge_top16(groups[k], groups[k + 1]) for k in range(0, len(groups), 2)]
    return groups[0]


def _count_gt(vs, s):
    g = jnp.where(vs[0] > s, 1.0, 0.0)
    for r in range(1, len(vs)):
        g = jnp.where(vs[r] > s, float(r + 1), g)
    return g


def _xpose8(x):
    x = list(x)
    sub = lax.broadcasted_iota(jnp.int32, (8, LANES), 0)
    for d in (4, 2, 1):
        keep = (sub & d) == 0
        for a in range(8):
            if not a & d:
                xa, xb = x[a], x[a + d]
                x[a] = jnp.where(keep, xa, pltpu.roll(xb, d, axis=0))
                x[a + d] = jnp.where(keep, pltpu.roll(xa, 8 - d, axis=0), xb)
    return x


def _select_kernel(sc_ref, r2_ref, p2_ref, c_ref, p1_ref, s_ref, t_scr):
    one = lambda m: jnp.where(m, 1.0, 0.0)
    lane_tile = lambda r: slice(r * LANES, (r + 1) * LANES)
    for h in range(2):
        for kg in range(NKEYS // 8):
            tiles = _xpose8([sc_ref[h, kg * 8:(kg + 1) * 8, lane_tile(r)] for r in range(8)])
            for k in range(8):
                s_ref[h, kg * 8 + k] = tiles[k]
    v = [_top16_values([[s_ref[h, g * TOPK + k] for k in range(TOPK)] for g in range(NKEYS // TOPK)])
         for h in range(2)]
    v1, v2 = v

    tie = jnp.zeros((8, LANES), F32)
    for h in range(2):
        n_ge = one(s_ref[h, 0] >= v[h][TOPK - 1])
        for j in range(1, NKEYS):
            n_ge = n_ge + one(s_ref[h, j] >= v[h][TOPK - 1])
        tie = jnp.maximum(tie, one(n_ge > float(TOPK)))
        for r in range(TOPK - 1):
            tie = jnp.maximum(tie, one(v[h][r] == v[h][r + 1]))
    t_scr[...] = jnp.zeros_like(t_scr)

    @pl.when(jnp.max(tie) > 0.0)
    def _():
        def lower_equal(jp, carry):
            for h in range(2):
                sv = s_ref[h, jp]
                for j in range(1, NKEYS):
                    inc = jnp.where(jp < j, 1.0, 0.0)
                    t_scr[h, j] = t_scr[h, j] + jnp.where(s_ref[h, j] == sv, inc, 0.0)
            return carry
        lax.fori_loop(0, NKEYS - 1, lower_equal, 0)

    cand = {p: v1[p[0]] + v2[p[1]] for p in _CAND_PAIRS}
    rest = [cand[p] for p in _CAND_PAIRS if p[0] > 0]
    rest += [None] * (-len(rest) % TOPK)
    groups = [[cand[(0, r2)] for r2 in range(TOPK)]] + [rest[k:k + TOPK] for k in range(0, len(rest), TOPK)]
    while len(groups) & (len(groups) - 1):
        groups.append([None] * TOPK)
    w = _top16_values(groups)
    tau = w[TOPK - 1]
    n_gt = sum(one(cand[p] > tau) for p in _CAND_PAIRS)
    need = float(TOPK) - n_gt
    eq_seen = jnp.zeros((8, LANES), F32)
    counts = [jnp.zeros((8, LANES), F32) for _ in range(TOPK)]
    for p in _CAND_PAIRS:
        eq = cand[p] == tau
        counts[p[0]] = counts[p[0]] + one((cand[p] > tau) | (eq & (eq_seen < need)))
        eq_seen = eq_seen + one(eq)
    z = jnp.ones((8, LANES), F32)
    for r in range(1, TOPK):
        z = z + jnp.exp(w[r] - w[0])
    inv_z = 1.0 / z

    for kg in range(NKEYS // 8):
        tabs = [[], [], [], []]
        for j in range(kg * 8, (kg + 1) * 8):
            s1 = s_ref[0, j]
            s2 = s_ref[1, j]
            rank1 = _count_gt(v1, s1) + t_scr[0, j]
            ckey = jnp.zeros((8, LANES), F32)
            for r in range(TOPK):
                ckey = jnp.where(rank1 == float(r), counts[r], ckey)
            tabs[0].append(_count_gt(v2, s2) + t_scr[1, j])
            tabs[1].append(jnp.exp(s2 - v2[0]))
            tabs[2].append(ckey)
            tabs[3].append(jnp.exp(s1 - v1[0]) * inv_z)
        for ref, tab in zip((r2_ref, p2_ref, c_ref, p1_ref), tabs):
            for r, tile in enumerate(_xpose8(tab)):
                ref[kg * 8:(kg + 1) * 8, lane_tile(r)] = tile


def _select(scores):
    T = scores.shape[-1]
    tt = 8 * LANES
    tab = pl.BlockSpec((None, NKEYS, tt), lambda i, hd: (hd, 0, i))
    return pl.pallas_call(
        _select_kernel,
        grid=(T // tt, PEER_HEADS),
        in_specs=[pl.BlockSpec((2, NKEYS, tt), lambda i, hd: (hd, 0, i))],
        out_specs=[tab] * 4,
        out_shape=[jax.ShapeDtypeStruct((PEER_HEADS, NKEYS, T), F32)] * 4,
        scratch_shapes=[pltpu.VMEM((2, NKEYS, 8, LANES), F32), pltpu.VMEM((2, NKEYS, 8, LANES), F32)],
        compiler_params=_cparams(("arbitrary", "arbitrary")),
        name="select",
    )(scores)


_SQRT_HALF = 0.7071067811865476
_JROWS = 16
_PEER_NB = 8 * NKEYS
_PEER_STEPS = N_EXPERTS // _PEER_NB
_GATE_CHAINS = 1


def _zero_after(x):
    u = pltpu.bitcast(x, jnp.uint32)
    u = lax.shift_right_logical(lax.shift_right_logical(u, jnp.uint32(16)), jnp.uint32(16))
    return pltpu.bitcast(u, BF16)


def _peer_kernel(h2t_ref, u_ref, vt_ref, r2_ref, p2_ref, c_ref, p1_ref, x1_ref, mod_ref, gf_ref, yc_ref, yl_ref,
                 a_scr, w_scr, acc_scr, tab_scr, *, tb, n_ctx_tok, lat_len):
    L = pl.program_id(0)
    n_lg = tb // LANES
    vpu_blk = L - 1
    acc_blk = L - 2

    @pl.when(L == 0)
    def _():
        a_scr[...] = jnp.zeros_like(a_scr)
        w_scr[...] = jnp.zeros_like(w_scr)
        acc_scr[...] = jnp.zeros_like(acc_scr)

    @pl.when((L == 0) | (vpu_blk % _PEER_STEPS == 0))
    def _():
        for jg in range(NKEYS // _JROWS):
            rows = slice(jg * _JROWS, (jg + 1) * _JROWS)
            for lg in range(n_lg):
                lanes = slice(lg * LANES, (lg + 1) * LANES)
                for hd in range(PEER_HEADS):
                    tab_scr[jg, lg, hd * _JROWS:(hd + 1) * _JROWS, :] = r2_ref[hd, rows, lanes].astype(BF16)
                    tab_scr[jg, lg, (PEER_HEADS + hd) * _JROWS:(PEER_HEADS + hd + 1) * _JROWS, :] = \
                        p2_ref[hd, rows, lanes].astype(BF16)

    @pl.when((acc_blk >= 0) & (acc_blk % _PEER_STEPS == 0))
    def _():
        acc_scr[...] = jnp.zeros_like(acc_scr)

    def stages(cur):
        prv = 1 - cur
        a_scr[cur] = jnp.dot(u_ref[...], h2t_ref[...], preferred_element_type=F32)
        acc_scr[...] += jnp.dot(vt_ref[...], w_scr[cur], preferred_element_type=F32)
        link = [None] * _GATE_CHAINS
        for lg in range(n_lg):
            lanes = slice(lg * LANES, (lg + 1) * LANES)
            chain = lg * _GATE_CHAINS // n_lg
            for i8 in range(_PEER_NB // NKEYS):
                bcast = lambda ref, hd: jnp.broadcast_to(ref[hd, i8:i8 + 1, lanes], (_JROWS, LANES)).astype(BF16)
                crow = [bcast(c_ref, hd) for hd in range(PEER_HEADS)]
                p1row = [bcast(p1_ref, hd) for hd in range(PEER_HEADS)]
                for jg in range(NKEYS // _JROWS):
                    rows = slice(jg * _JROWS, (jg + 1) * _JROWS)
                    arow = slice(i8 * NKEYS + jg * _JROWS, i8 * NKEYS + (jg + 1) * _JROWS)
                    a = a_scr[prv, arow, lanes]
                    terms = []
                    for hd in range(PEER_HEADS):
                        r2 = tab_scr[jg, lg, hd * _JROWS:(hd + 1) * _JROWS, :]
                        p2 = tab_scr[jg, lg, (PEER_HEADS + hd) * _JROWS:(PEER_HEADS + hd + 1) * _JROWS, :]
                        terms.append(jnp.where(r2 < crow[hd], p2, jnp.zeros_like(p2)) * p1row[hd])
                    if link[chain] is not None:
                        terms[0] = terms[0] + _zero_after(link[chain])
                    while len(terms) > 1:
                        terms = [terms[k] + terms[k + 1] for k in range(0, len(terms), 2)]
                    g = terms[0]
                    gelu = 0.5 * a * (1.0 + lax.erf(a * _SQRT_HALF))
                    w = g * gelu.astype(BF16)
                    w_scr[prv, arow, lanes] = w
                    link[chain] = w

    for slot in range(2):
        pl.when(L % 2 == slot)(functools.partial(stages, slot))

    @pl.when((acc_blk >= 0) & (acc_blk % _PEER_STEPS == _PEER_STEPS - 1))
    def _():
        is_ctx, row = _mod_row(acc_blk // _PEER_STEPS, tb, n_ctx_tok, lat_len)
        ga2 = mod_ref[pl.ds(row, 1), 5 * D_MODEL:6 * D_MODEL]
        x2 = x1_ref[...] + ga2 * acc_scr[...].T
        y = _rms(x2) * gf_ref[...]

        @pl.when(is_ctx)
        def _():
            yc_ref[...] = y

        @pl.when(jnp.logical_not(is_ctx))
        def _():
            yl_ref[...] = y


def _peer(h2t, u_bf, vt_bf, r2, p2, cc, p1, x1, mod, g_final, *, tb, n_ctx_tok, lat_len):
    T = h2t.shape[1]
    nb = _PEER_NB
    n_tiles = T // tb
    nct = n_ctx_tok // tb
    n_blocks = n_tiles * _PEER_STEPS
    blk = lambda L, lag: jnp.clip(L - lag, 0, n_blocks - 1)
    tile = lambda L, lag: blk(L, lag) // _PEER_STEPS
    step = lambda L, lag: blk(L, lag) % _PEER_STEPS
    tab = pl.BlockSpec((PEER_HEADS, NKEYS, tb), lambda L: (0, 0, tile(L, 1)))
    rowtab = pl.BlockSpec((PEER_HEADS, nb // NKEYS, tb), lambda L: (0, step(L, 1), tile(L, 1)))
    return pl.pallas_call(
        functools.partial(_peer_kernel, tb=tb, n_ctx_tok=n_ctx_tok, lat_len=lat_len),
        grid=(n_blocks + 2,),
        in_specs=[pl.BlockSpec((D_MODEL, tb), lambda L: (0, tile(L, 0))),
                  pl.BlockSpec((nb, D_MODEL), lambda L: (step(L, 0), 0)),
                  pl.BlockSpec((D_MODEL, nb), lambda L: (0, step(L, 2))),
                  tab, tab, rowtab, rowtab,
                  pl.BlockSpec((tb, D_MODEL), lambda L: (tile(L, 2), 0)),
                  pl.BlockSpec(mod.shape, lambda L: (0, 0)),
                  pl.BlockSpec((1, D_MODEL), lambda L: (0, 0))],
        out_specs=[pl.BlockSpec((tb, D_MODEL), lambda L: (jnp.minimum(tile(L, 2), nct - 1), 0)),
                   pl.BlockSpec((tb, D_MODEL), lambda L: (jnp.maximum(tile(L, 2) - nct, 0), 0))],
        out_shape=[jax.ShapeDtypeStruct((n_ctx_tok, D_MODEL), F32),
                   jax.ShapeDtypeStruct((T - n_ctx_tok, D_MODEL), F32)],
        scratch_shapes=[pltpu.VMEM((2, nb, tb), F32), pltpu.VMEM((2, nb, tb), BF16), pltpu.VMEM((D_MODEL, tb), F32),
                        pltpu.VMEM((NKEYS // _JROWS, tb // LANES, 2 * PEER_HEADS * _JROWS, LANES), BF16)],
        compiler_params=_cparams(("arbitrary",)),
        name="peer",
    )(h2t, u_bf, vt_bf, r2, p2, cc, p1, x1, mod, g_final.reshape(1, -1))


def _layer(x_prompt, x_sample, state_C, state_n, state_m, c, c_ctx, w_ada, b_ada, g_norm1, w_in, b_igate, b_fgate,
           conv_w, conv_b, g_mlstm, w_out, g_norm2, peer_wq, peer_k1, peer_k2, peer_u, peer_v, g_final,
           *, tm, tc_lat, tb):
    B, S, _ = x_prompt.shape
    DB, DS, _ = x_sample.shape
    n_ctx_tok = B * S
    ng = N_DIRS * N_HEADS
    x_ctx = x_prompt.reshape(n_ctx_tok, D_MODEL)
    x_lat = x_sample.reshape(DB * DS, D_MODEL)
    cv8 = jnp.zeros((8, D_MODEL), F32).at[0].set(c_ctx).at[1:1 + DB].set(c)
    mod = _modulation(cv8, w_ada, b_ada)

    conv, q, k, v, og, gates, gates_t = _in_proj(
        x_ctx, x_lat, mod, g_norm1, w_in, b_igate, b_fgate, conv_w, conv_b,
        tm=tm, n_ctx_tok=n_ctx_tok, ctx_len=S, lat_len=DS)

    def ext_state(C, n_, m_):
        nb_ = jnp.broadcast_to(n_[..., None], n_.shape + (DH,))
        cext = jnp.concatenate([C, nb_], axis=-1).reshape(-1, ng, DH, 2 * DH)
        mext = jnp.broadcast_to(m_.reshape(-1, ng, 1), (C.shape[0], ng, LANES))
        return cext, mext

    z_c, z_m = ext_state(jnp.zeros((B, N_DIRS, N_HEADS, DH, DH), F32), jnp.zeros((B, N_DIRS, N_HEADS, DH), F32),
                         jnp.zeros((B, N_DIRS, N_HEADS), F32))
    hf_c, hb_c, c_fin, m_fin = _mlstm(q, k, v, gates, gates_t, z_c, z_m, tok0=0, nseq=B, seq_len=S, tc=S,
                                      write_state=True)
    l_c, l_m = ext_state(state_C, state_n, state_m)
    hf_l, hb_l = _mlstm(q, k, v, gates, gates_t, l_c, l_m, tok0=n_ctx_tok, nseq=DB, seq_len=DS, tc=tc_lat,
                        write_state=False)

    x1, h2t, scores = _mix(x_ctx, x_lat, conv, hf_c, hf_l, hb_c, hb_l, og, mod, g_mlstm, w_out, g_norm2,
                           peer_wq, peer_k1, peer_k2, tm=tm, n_ctx_tok=n_ctx_tok, lat_len=DS)
    r2, p2, cc, p1 = _select(scores)
    y_ctx, y_lat = _peer(h2t, peer_u.astype(BF16), peer_v.astype(BF16).T, r2, p2, cc, p1, x1, mod, g_final,
                         tb=tb, n_ctx_tok=n_ctx_tok, lat_len=DS)

    y_prompt = y_ctx.reshape(B, S, D_MODEL)
    y_sample = y_lat.reshape(DB, DS, D_MODEL)
    new_C = c_fin[..., :DH].reshape(B, 1, N_DIRS, N_HEADS, DH, DH)
    new_n = c_fin[..., DH].reshape(B, 1, N_DIRS, N_HEADS, DH)
    new_m = m_fin[..., 0].reshape(B, 1, N_DIRS, N_HEADS)
    return y_prompt, y_sample, new_C, new_n, new_m


def kernel(x_prompt, x_sample, state_C, state_n, state_m, c, c_ctx, w_ada, b_ada, g_norm1, w_in, b_igate, b_fgate,
           conv_w, conv_b, g_mlstm, w_out, g_norm2, peer_wq, peer_k1, peer_k2, peer_u, peer_v, g_final):
    return _layer(x_prompt, x_sample, state_C[:, 0], state_n[:, 0], state_m[:, 0], c, c_ctx, w_ada[0], b_ada[0],
                  g_norm1[0], w_in[0], b_igate[0], b_fgate[0], conv_w[0], conv_b[0], g_mlstm[0], w_out[0],
                  g_norm2[0], peer_wq[0], peer_k1[0], peer_k2[0], peer_u[0], peer_v[0], g_final,
                  tm=512, tc_lat=256, tb=512)
```

```python
import functools

import numpy as np
import jax
import jax.numpy as jnp
from jax import lax
from jax.experimental import pallas as pl
from jax.experimental.pallas import tpu as pltpu

F32 = jnp.float32
BF16 = jnp.bfloat16

D_MODEL = 1024
CONV_W = 512
N_HEADS = 4
DH = 128
MLSTM_W = N_HEADS * DH
N_DIRS = 2
GRID_W = 64
N_GROUPS = 7
GATE_COL0 = N_GROUPS * 512
N_GATES = 2 * N_DIRS * N_HEADS
PEER_HEADS = 8
NKEYS = 128
TOPK = 16
N_EXPERTS = NKEYS * NKEYS
EPS = 1e-6
LANES = 128
NEG_INF = float("-inf")

VMEM_LIMIT = 56 * 1024 * 1024


def _cparams(sem):
    return pltpu.CompilerParams(dimension_semantics=sem, vmem_limit_bytes=VMEM_LIMIT)


def _mod_kernel(cv_ref, w_ref, b_ref, o_ref):
    cv = cv_ref[...]
    s = cv * (1.0 / (1.0 + jnp.exp(-cv)))
    o_ref[...] = jnp.dot(s, w_ref[...], preferred_element_type=F32) + b_ref[...]


def _modulation(cv8, w_ada, b_ada):
    n = w_ada.shape[1]
    tn = 1024
    return pl.pallas_call(
        _mod_kernel,
        grid=(n // tn,),
        in_specs=[pl.BlockSpec((8, D_MODEL), lambda j: (0, 0)),
                  pl.BlockSpec((D_MODEL, tn), lambda j: (0, j)),
                  pl.BlockSpec((1, tn), lambda j: (0, j))],
        out_specs=pl.BlockSpec((8, tn), lambda j: (0, j)),
        out_shape=jax.ShapeDtypeStruct((8, n), F32),
        compiler_params=_cparams(("arbitrary",)),
        name="modulation",
    )(cv8, w_ada, b_ada.reshape(1, n))


def _mod_row(i, tm, n_ctx_tok, lat_len):
    n_ctx_tiles = n_ctx_tok // tm
    tiles_per_seq = lat_len // tm
    is_ctx = i < n_ctx_tiles
    row = jnp.where(is_ctx, 0, 1 + (i - n_ctx_tiles) // tiles_per_seq)
    return is_ctx, row


def _group_specs(tm, width, n_ctx_tok):
    nct = n_ctx_tok // tm
    return [pl.BlockSpec((tm, width), lambda i: (jnp.minimum(i, nct - 1), 0)),
            pl.BlockSpec((tm, width), lambda i: (jnp.maximum(i - nct, 0), 0))]


def _rms(x):
    return x * lax.rsqrt(jnp.mean(x * x, axis=-1, keepdims=True) + EPS)


def _log_sigmoid(z):
    return jnp.minimum(z, 0.0) - jnp.log(1.0 + jnp.exp(-jnp.abs(z)))


def _inproj_kernel(xc_ref, xl_ref, mod_ref, g1_ref, win_ref, wg_ref, wgt_ref, bg_ref, bgt_ref, cw_ref, cb_ref,
                   conv_ref, q_ref, k_ref, v_ref, o_ref, gates_ref, gatest_ref,
                   *, tm, n_ctx_tok, ctx_len, lat_len):
    i = pl.program_id(0)
    is_ctx, row = _mod_row(i, tm, n_ctx_tok, lat_len)
    sh1 = mod_ref[pl.ds(row, 1), 0:D_MODEL]
    sc1 = mod_ref[pl.ds(row, 1), D_MODEL:2 * D_MODEL]
    x = jnp.where(is_ctx, xc_ref[...], xl_ref[...])
    h = _rms(x) * g1_ref[...] * (1.0 + sc1) + sh1
    hb = h.astype(BF16)

    def grp(g):
        return jnp.dot(hb, win_ref[:, g * 512:(g + 1) * 512], preferred_element_type=F32)

    u = grp(1) * grp(2)
    pm = jnp.where(is_ctx, ctx_len - 1, GRID_W - 1)
    pos = lax.broadcasted_iota(jnp.int32, (tm, 1), 0) & pm
    u_prev = jnp.where(pos == 0, 0.0, pltpu.roll(u, 1, axis=0))
    u_next = jnp.where(pos == pm, 0.0, pltpu.roll(u, tm - 1, axis=0))
    y = u_prev * cw_ref[0:1, :] + u * cw_ref[1:2, :] + u_next * cw_ref[2:3, :] + cb_ref[...]
    conv_ref[...] = (grp(0) * y).astype(BF16)

    q_ref[...] = (grp(3) * (DH ** -0.5)).astype(BF16)
    k_ref[...] = grp(4).astype(BF16)
    v_ref[...] = grp(5).astype(BF16)
    o_ref[...] = grp(6)

    z = jnp.dot(hb, wg_ref[...], preferred_element_type=F32) + bg_ref[...]
    lane = lax.broadcasted_iota(jnp.int32, z.shape, 1)
    gates_ref[...] = jnp.where(lane >= N_DIRS * N_HEADS, _log_sigmoid(z), z)
    zt = lax.dot_general(wgt_ref[...], hb, (((1,), (1,)), ((), ())), preferred_element_type=F32) + bgt_ref[...]
    sub = lax.broadcasted_iota(jnp.int32, zt.shape, 0)
    gatest_ref[...] = jnp.where(sub >= N_DIRS * N_HEADS, _log_sigmoid(zt), zt)


def _in_proj(x_ctx, x_lat, mod, g1, w_in, b_igate, b_fgate, conv_w, conv_b, *, tm, n_ctx_tok, ctx_len, lat_len):
    T = x_ctx.shape[0] + x_lat.shape[0]
    win = w_in[:, :GATE_COL0].astype(BF16)
    wg = jnp.zeros((D_MODEL, LANES), F32).at[:, :N_GATES].set(w_in[:, GATE_COL0:]).astype(BF16)
    bgate = jnp.zeros((1, LANES), F32).at[0, :N_GATES].set(
        jnp.concatenate([b_igate.reshape(-1), b_fgate.reshape(-1)]))
    tok = lambda w: pl.BlockSpec((tm, w), lambda i: (i, 0))
    full = lambda a: pl.BlockSpec(a.shape, lambda i: (0,) * a.ndim)
    args = (x_ctx, x_lat, mod, g1.reshape(1, -1), win, wg, wg.T, bgate, bgate.T, conv_w, conv_b.reshape(1, -1))
    return pl.pallas_call(
        functools.partial(_inproj_kernel, tm=tm, n_ctx_tok=n_ctx_tok, ctx_len=ctx_len, lat_len=lat_len),
        grid=(T // tm,),
        in_specs=_group_specs(tm, D_MODEL, n_ctx_tok) + [full(a) for a in args[2:]],
        out_specs=[tok(CONV_W), tok(MLSTM_W), tok(MLSTM_W), tok(MLSTM_W), tok(MLSTM_W), tok(LANES),
                   pl.BlockSpec((LANES, tm), lambda i: (0, i))],
        out_shape=[jax.ShapeDtypeStruct((T, CONV_W), BF16),
                   jax.ShapeDtypeStruct((T, MLSTM_W), BF16),
                   jax.ShapeDtypeStruct((T, MLSTM_W), BF16),
                   jax.ShapeDtypeStruct((T, MLSTM_W), BF16),
                   jax.ShapeDtypeStruct((T, MLSTM_W), F32),
                   jax.ShapeDtypeStruct((T, LANES), F32),
                   jax.ShapeDtypeStruct((LANES, T), F32)],
        compiler_params=_cparams(("arbitrary",)),
        name="in_proj",
    )(*args)


def _split_bf16(x):
    hi = x.astype(BF16)
    return hi, (x - hi.astype(F32)).astype(BF16)


def _mlstm_kernel(qf_ref, kf_ref, vf_ref, gf_ref, gtf_ref, qb_ref, kb_ref, vb_ref, gb_ref, gtb_ref,
                  c0_ref, m0_ref, hf_ref, hb_ref, *rest, tc, write_state):
    if write_state:
        cout_ref, mout_ref, c_scr, m_scr = rest
    else:
        c_scr, m_scr = rest
    c = pl.program_id(1)

    @pl.when(c == 0)
    def _():
        c_scr[...] = c0_ref[...]
        m_scr[...] = m0_ref[...]

    r_io = lax.broadcasted_iota(jnp.int32, (tc, tc), 0)
    c_io = lax.broadcasted_iota(jnp.int32, (tc, tc), 1)
    ones = jnp.ones((tc, DH), BF16)
    wide = lambda x, width: jnp.concatenate([x] * (width // DH), axis=1)

    for d in range(N_DIRS):
        q_ref, k_ref, v_ref, g_ref, gt_ref, h_ref = (
            (qf_ref, kf_ref, vf_ref, gf_ref, gtf_ref, hf_ref) if d == 0 else
            (qb_ref, kb_ref, vb_ref, gb_ref, gtb_ref, hb_ref))
        causal = (c_io <= r_io) if d == 0 else (c_io >= r_io)
        last = tc - 1 if d == 0 else 0
        tri = causal.astype(BF16)
        tri_t = ((r_io <= c_io) if d == 0 else (r_io >= c_io)).astype(BF16)
        g = g_ref[...]
        gt = gt_ref[...]
        g_hi, g_lo = _split_bf16(g)
        gt_hi, gt_lo = _split_bf16(gt)
        bcum = (jnp.dot(tri, g_hi, preferred_element_type=F32) + jnp.dot(tri, g_lo, preferred_element_type=F32))
        bcum_t = (jnp.dot(gt_hi, tri_t, preferred_element_type=F32)
                  + jnp.dot(gt_lo, tri_t, preferred_element_type=F32))
        for hd in range(N_HEADS):
            gi = d * N_HEADS + hd
            gfi = N_DIRS * N_HEADS + gi
            sl = slice(hd * DH, (hd + 1) * DH)
            q = q_ref[:, sl]
            k = k_ref[:, sl]
            v = v_ref[:, sl]
            ig_c = jnp.broadcast_to(g[:, gi:gi + 1], (tc, DH))
            b_c = jnp.broadcast_to(bcum[:, gfi:gfi + 1], (tc, DH))
            ig_r = gt[gi:gi + 1, :]
            b_r = bcum_t[gfi:gfi + 1, :]
            m_prev = m_scr[gi:gi + 1, :]
            dmat = jnp.where(causal, wide(b_c, tc) - b_r + ig_r, NEG_INF)
            inter = b_c + m_prev
            m_t = jnp.maximum(inter, jnp.max(dmat, axis=-1, keepdims=True))
            s = (lax.dot_general(q, k, (((1,), (1,)), ((), ())), preferred_element_type=F32)
                 * jnp.exp(dmat - wide(m_t, tc)))
            a = jnp.exp(inter - m_t)
            cext = c_scr[gi]
            vext = jnp.concatenate([v, ones], axis=1)
            nd = (wide(a, 2 * DH) * jnp.dot(q, cext.astype(BF16), preferred_element_type=F32)
                  + jnp.dot(s.astype(BF16), vext, preferred_element_type=F32))
            num = nd[:, :DH]
            den = nd[:, DH:]
            h_ref[:, sl] = num / jnp.maximum(jnp.abs(den), jnp.exp(-m_t))
            m_new = m_t[last:last + 1, :]
            b_last = b_c[last:last + 1, :]
            w_c = jnp.exp(b_last - b_c + ig_c - m_new)
            decay = jnp.exp(b_last + m_prev - m_new)
            kw_t = (k.astype(F32) * w_c).T.astype(BF16)
            c_scr[gi] = wide(decay, 2 * DH) * cext + jnp.dot(kw_t, vext, preferred_element_type=F32)
            m_scr[gi:gi + 1, :] = m_new

    if write_state:
        @pl.when(c == pl.num_programs(1) - 1)
        def _():
            cout_ref[...] = c_scr[...]
            mout_ref[...] = m_scr[...]


def _mlstm(q, k, v, gates, gates_t, c0ext, m0, *, tok0, nseq, seq_len, tc, write_state):
    nc = seq_len // tc
    blk0 = tok0 // tc
    fwd = lambda b, c: b * nc + c
    bwd = lambda b, c: b * nc + nc - 1 - c
    tokspec = lambda w, f, o: pl.BlockSpec((tc, w), lambda b, c: (o + f(b, c), 0))
    gtspec = lambda f: pl.BlockSpec((LANES, tc), lambda b, c: (0, blk0 + f(b, c)))
    ng = N_DIRS * N_HEADS
    in_specs = [tokspec(MLSTM_W, fwd, blk0)] * 3 + [tokspec(LANES, fwd, blk0), gtspec(fwd)] \
        + [tokspec(MLSTM_W, bwd, blk0)] * 3 + [tokspec(LANES, bwd, blk0), gtspec(bwd)] \
        + [pl.BlockSpec((None, ng, DH, 2 * DH), lambda b, c: (b, 0, 0, 0)),
           pl.BlockSpec((None, ng, LANES), lambda b, c: (b, 0, 0))]
    out_specs = [tokspec(MLSTM_W, fwd, 0), tokspec(MLSTM_W, bwd, 0)]
    out_shape = [jax.ShapeDtypeStruct((nseq * seq_len, MLSTM_W), F32)] * 2
    if write_state:
        out_specs += [pl.BlockSpec((None, ng, DH, 2 * DH), lambda b, c: (b, 0, 0, 0)),
                      pl.BlockSpec((None, ng, LANES), lambda b, c: (b, 0, 0))]
        out_shape += [jax.ShapeDtypeStruct((nseq, ng, DH, 2 * DH), F32),
                      jax.ShapeDtypeStruct((nseq, ng, LANES), F32)]
    return pl.pallas_call(
        functools.partial(_mlstm_kernel, tc=tc, write_state=write_state),
        grid=(nseq, nc),
        in_specs=in_specs,
        out_specs=out_specs,
        out_shape=out_shape,
        scratch_shapes=[pltpu.VMEM((ng, DH, 2 * DH), F32), pltpu.VMEM((ng, LANES), F32)],
        compiler_params=_cparams(("arbitrary", "arbitrary")),
        name="mlstm_ctx" if write_state else "mlstm_lat",
    )(q, k, v, gates, gates_t, q, k, v, gates, gates_t, c0ext, m0)


def _mix_kernel(xc_ref, xl_ref, conv_ref, hfc_ref, hfl_ref, hbc_ref, hbl_ref, og_ref, mod_ref, gm_ref, wout_ref,
                g2_ref, wq_ref, k1_ref, k2_ref, x1_ref, h2t_ref, sc_ref, *, tm, n_ctx_tok, lat_len):
    i = pl.program_id(0)
    is_ctx, row = _mod_row(i, tm, n_ctx_tok, lat_len)
    mrow = lambda j: mod_ref[pl.ds(row, 1), j * D_MODEL:(j + 1) * D_MODEL]
    hs = jnp.where(is_ctx, hfc_ref[...] + hbc_ref[...], hfl_ref[...] + hbl_ref[...])
    hn = jnp.concatenate([_rms(hs[:, hd * DH:(hd + 1) * DH]) for hd in range(N_HEADS)], axis=1)
    og = og_ref[...]
    ml = (hn * gm_ref[...] * (1.0 / (1.0 + jnp.exp(-og)))).astype(BF16)
    mix = (jnp.dot(conv_ref[...], wout_ref[0:CONV_W, :], preferred_element_type=F32)
           + jnp.dot(ml, wout_ref[CONV_W:, :], preferred_element_type=F32))
    x1 = jnp.where(is_ctx, xc_ref[...], xl_ref[...]) + mrow(2) * mix
    x1_ref[...] = x1
    h2f = _rms(x1) * g2_ref[...] * (1.0 + mrow(4)) + mrow(3)
    h2 = h2f.astype(BF16)
    h2t_ref[...] = h2f.T.astype(BF16)
    nt = (((1,), (1,)), ((), ()))
    for hd in range(PEER_HEADS):
        qh = jnp.dot(h2, wq_ref[:, hd * 2 * NKEYS:(hd + 1) * 2 * NKEYS], preferred_element_type=F32).astype(BF16)
        sc_ref[2 * hd] = lax.dot_general(k1_ref[...], qh[:, :NKEYS], nt, preferred_element_type=F32)
        sc_ref[2 * hd + 1] = lax.dot_general(k2_ref[...], qh[:, NKEYS:], nt, preferred_element_type=F32)


def _mix(x_ctx, x_lat, conv, hf_ctx, hf_lat, hb_ctx, hb_lat, og, mod, g_mlstm, w_out, g2, wq, k1, k2,
         *, tm, n_ctx_tok, lat_len):
    T = x_ctx.shape[0] + x_lat.shape[0]
    tok = lambda w: pl.BlockSpec((tm, w), lambda i: (i, 0))
    full = lambda a: pl.BlockSpec(a.shape, lambda i: (0,) * a.ndim)
    consts = (mod, g_mlstm.reshape(1, -1), w_out.astype(BF16), g2.reshape(1, -1), wq.astype(BF16),
              k1.astype(BF16), k2.astype(BF16))
    return pl.pallas_call(
        functools.partial(_mix_kernel, tm=tm, n_ctx_tok=n_ctx_tok, lat_len=lat_len),
        grid=(T // tm,),
        in_specs=_group_specs(tm, D_MODEL, n_ctx_tok) + [tok(CONV_W)] + _group_specs(tm, MLSTM_W, n_ctx_tok) * 2
        + [tok(MLSTM_W)] + [full(a) for a in consts],
        out_specs=[tok(D_MODEL), pl.BlockSpec((D_MODEL, tm), lambda i: (0, i)),
                   pl.BlockSpec((2 * PEER_HEADS, NKEYS, tm), lambda i: (0, 0, i))],
        out_shape=[jax.ShapeDtypeStruct((T, D_MODEL), F32),
                   jax.ShapeDtypeStruct((D_MODEL, T), BF16),
                   jax.ShapeDtypeStruct((2 * PEER_HEADS, NKEYS, T), F32)],
        compiler_params=_cparams(("arbitrary",)),
        name="mix",
    )(x_ctx, x_lat, conv, hf_ctx, hf_lat, hb_ctx, hb_lat, og, *consts)


def _sort16_pairs():
    def merge(lo, hi, r):
        step = r * 2
        if step < hi - lo:
            yield from merge(lo, hi, step)
            yield from merge(lo + r, hi, step)
            yield from ((i, i + r) for i in range(lo + r, hi - r, step))
        else:
            yield (lo, lo + r)

    def sort(lo, hi):
        if hi - lo >= 1:
            mid = lo + (hi - lo) // 2
            yield from sort(lo, mid)
            yield from sort(mid + 1, hi)
            yield from merge(lo, hi, 1)

    return tuple(sort(0, TOPK - 1))


_SORT16 = _sort16_pairs()
_CAND_PAIRS = tuple((r1, r2) for r1 in range(TOPK) for r2 in range(TOPK) if (r1 + 1) * (r2 + 1) <= TOPK)


def _cmpx(x, i, j):
    a, b = x[i], x[j]
    if b is None:
        return
    if a is None:
        x[i], x[j] = b, None
        return
    x[i], x[j] = jnp.maximum(a, b), jnp.minimum(a, b)


def _sort16(x):
    x = list(x)
    for i, j in _SORT16:
        _cmpx(x, i, j)
    return x


def _merge_top16(a, b):
    x = []
    for k in range(TOPK):
        p, q = a[k], b[TOPK - 1 - k]
        x.append(q if p is None else p if q is None else jnp.maximum(p, q))
    d = TOPK // 2
    while d:
        for k in range(TOPK):
            if not k & d:
                _cmpx(x, k, k + d)
        d //= 2
    return x


def _top16_values(groups):
    groups = [_sort16(g) for g in groups]
    while len(groups) > 1:
        groups = [_merge_top16(groups[k], groups[k + 1]) for k in range(0, len(groups), 2)]
    return groups[0]


def _count_gt(vs, s):
    g = jnp.where(vs[0] > s, 1.0, 0.0)
    for r in range(1, len(vs)):
        g = jnp.where(vs[r] > s, float(r + 1), g)
    return g


def _xpose8(x):
    x = list(x)
    sub = lax.broadcasted_iota(jnp.int32, (8, LANES), 0)
    for d in (4, 2, 1):
        keep = (sub & d) == 0
        for a in range(8):
            if not a & d:
                xa, xb = x[a], x[a + d]
                x[a] = jnp.where(keep, xa, pltpu.roll(xb, d, axis=0))
                x[a + d] = jnp.where(keep, pltpu.roll(xa, 8 - d, axis=0), xb)
    return x


def _select_kernel(sc_ref, r2_ref, p2_ref, c_ref, p1_ref, s_ref, t_scr):
    one = lambda m: jnp.where(m, 1.0, 0.0)
    lane_tile = lambda r: slice(r * LANES, (r + 1) * LANES)
    for h in range(2):
        for kg in range(NKEYS // 8):
            tiles = _xpose8([sc_ref[h, kg * 8:(kg + 1) * 8, lane_tile(r)] for r in range(8)])
            for k in range(8):
                s_ref[h, kg * 8 + k] = tiles[k]
    v = [_top16_values([[s_ref[h, g * TOPK + k] for k in range(TOPK)] for g in range(NKEYS // TOPK)])
         for h in range(2)]
    v1, v2 = v

    tie = jnp.zeros((8, LANES), F32)
    for h in range(2):
        n_ge = one(s_ref[h, 0] >= v[h][TOPK - 1])
        for j in range(1, NKEYS):
            n_ge = n_ge + one(s_ref[h, j] >= v[h][TOPK - 1])
        tie = jnp.maximum(tie, one(n_ge > float(TOPK)))
        for r in range(TOPK - 1):
            tie = jnp.maximum(tie, one(v[h][r] == v[h][r + 1]))
    has_tie = jnp.max(tie) > 0.0

    cand = {p: v1[p[0]] + v2[p[1]] for p in _CAND_PAIRS}
    rest = [cand[p] for p in _CAND_PAIRS if p[0] > 0]
    rest += [None] * (-len(rest) % TOPK)
    groups = [[cand[(0, r2)] for r2 in range(TOPK)]] + [rest[k:k + TOPK] for k in range(0, len(rest), TOPK)]
    while len(groups) & (len(groups) - 1):
        groups.append([None] * TOPK)
    w = _top16_values(groups)
    tau = w[TOPK - 1]
    n_gt = sum(one(cand[p] > tau) for p in _CAND_PAIRS)
    need = float(TOPK) - n_gt
    eq_seen = jnp.zeros((8, LANES), F32)
    counts = [jnp.zeros((8, LANES), F32) for _ in range(TOPK)]
    for p in _CAND_PAIRS:
        eq = cand[p] == tau
        counts[p[0]] = counts[p[0]] + one((cand[p] > tau) | (eq & (eq_seen < need)))
        eq_seen = eq_seen + one(eq)
    z = jnp.ones((8, LANES), F32)
    for r in range(1, TOPK):
        z = z + jnp.exp(w[r] - w[0])
    inv_z = 1.0 / z

    def tables(with_ties):
        for kg in range(NKEYS // 8):
            tabs = [[], [], [], []]
            for j in range(kg * 8, (kg + 1) * 8):
                s1 = s_ref[0, j]
                s2 = s_ref[1, j]
                if with_ties:
                    rank1 = _count_gt(v1, s1) + t_scr[0, j]
                    ckey = jnp.zeros((8, LANES), F32)
                    for r in range(TOPK):
                        ckey = jnp.where(rank1 == float(r), counts[r], ckey)
                    rank2 = _count_gt(v2, s2) + t_scr[1, j]
                else:
                    ckey = jnp.where(v1[0] > s1, counts[1], counts[0])
                    for r in range(1, TOPK - 1):
                        ckey = jnp.where(v1[r] > s1, counts[r + 1], ckey)
                    ckey = jnp.where(v1[TOPK - 1] > s1, 0.0, ckey)
                    rank2 = _count_gt(v2, s2)
                tabs[0].append(rank2)
                tabs[1].append(jnp.exp(s2 - v2[0]))
                tabs[2].append(ckey)
                tabs[3].append(jnp.exp(s1 - v1[0]) * inv_z)
            for ref, tab in zip((r2_ref, p2_ref, c_ref, p1_ref), tabs):
                for r, tile in enumerate(_xpose8(tab)):
                    ref[kg * 8:(kg + 1) * 8, lane_tile(r)] = tile

    @pl.when(has_tie)
    def _():
        t_scr[...] = jnp.zeros_like(t_scr)

        def lower_equal(jp, carry):
            for h in range(2):
                sv = s_ref[h, jp]
                for j in range(1, NKEYS):
                    inc = jnp.where(jp < j, 1.0, 0.0)
                    t_scr[h, j] = t_scr[h, j] + jnp.where(s_ref[h, j] == sv, inc, 0.0)
            return carry
        lax.fori_loop(0, NKEYS - 1, lower_equal, 0)
        tables(True)

    @pl.when(jnp.logical_not(has_tie))
    def _():
        tables(False)


def _select(scores):
    T = scores.shape[-1]
    tt = 8 * LANES
    tab = pl.BlockSpec((None, NKEYS, tt), lambda i, hd: (hd, 0, i))
    return pl.pallas_call(
        _select_kernel,
        grid=(T // tt, PEER_HEADS),
        in_specs=[pl.BlockSpec((2, NKEYS, tt), lambda i, hd: (hd, 0, i))],
        out_specs=[tab] * 4,
        out_shape=[jax.ShapeDtypeStruct((PEER_HEADS, NKEYS, T), F32)] * 4,
        scratch_shapes=[pltpu.VMEM((2, NKEYS, 8, LANES), F32), pltpu.VMEM((2, NKEYS, 8, LANES), F32)],
        compiler_params=_cparams(("arbitrary", "arbitrary")),
        name="select",
    )(scores)


_SQRT_HALF = 0.7071067811865476
_JROWS = 16
_PEER_NB = 8 * NKEYS
_PEER_STEPS = N_EXPERTS // _PEER_NB
_GATE_CHAINS = 1


def _zero_after(x):
    u = pltpu.bitcast(x, jnp.uint32)
    u = lax.shift_right_logical(lax.shift_right_logical(u, jnp.uint32(16)), jnp.uint32(16))
    return pltpu.bitcast(u, BF16)


def _peer_kernel(h2t_ref, u_ref, vt_ref, r2_ref, p2_ref, c_ref, p1_ref, x1_ref, mod_ref, gf_ref, yc_ref, yl_ref,
                 a_scr, w_scr, acc_scr, tab_scr, *, tb, n_ctx_tok, lat_len):
    L = pl.program_id(0)
    n_lg = tb // LANES
    vpu_blk = L - 1
    acc_blk = L - 2

    @pl.when(L == 0)
    def _():
        a_scr[...] = jnp.zeros_like(a_scr)
        w_scr[...] = jnp.zeros_like(w_scr)
        acc_scr[...] = jnp.zeros_like(acc_scr)

    @pl.when((L == 0) | (vpu_blk % _PEER_STEPS == 0))
    def _():
        for jg in range(NKEYS // _JROWS):
            rows = slice(jg * _JROWS, (jg + 1) * _JROWS)
            for lg in range(n_lg):
                lanes = slice(lg * LANES, (lg + 1) * LANES)
                for hd in range(PEER_HEADS):
                    tab_scr[jg, lg, hd * _JROWS:(hd + 1) * _JROWS, :] = r2_ref[hd, rows, lanes].astype(BF16)
                    tab_scr[jg, lg, (PEER_HEADS + hd) * _JROWS:(PEER_HEADS + hd + 1) * _JROWS, :] = \
                        p2_ref[hd, rows, lanes].astype(BF16)

    @pl.when((acc_blk >= 0) & (acc_blk % _PEER_STEPS == 0))
    def _():
        acc_scr[...] = jnp.zeros_like(acc_scr)

    def stages(cur):
        prv = 1 - cur
        a_scr[cur] = jnp.dot(u_ref[...], h2t_ref[...], preferred_element_type=F32)
        acc_scr[...] += jnp.dot(vt_ref[...], w_scr[cur], preferred_element_type=F32)
        link = [None] * _GATE_CHAINS
        for lg in range(n_lg):
            lanes = slice(lg * LANES, (lg + 1) * LANES)
            chain = lg * _GATE_CHAINS // n_lg
            for i8 in range(_PEER_NB // NKEYS):
                bcast = lambda ref, hd: jnp.broadcast_to(ref[hd, i8:i8 + 1, lanes], (_JROWS, LANES)).astype(BF16)
                crow = [bcast(c_ref, hd) for hd in range(PEER_HEADS)]
                p1row = [bcast(p1_ref, hd) for hd in range(PEER_HEADS)]
                for jg in range(NKEYS // _JROWS):
                    rows = slice(jg * _JROWS, (jg + 1) * _JROWS)
                    arow = slice(i8 * NKEYS + jg * _JROWS, i8 * NKEYS + (jg + 1) * _JROWS)
                    a = a_scr[prv, arow, lanes]
                    terms = []
                    for hd in range(PEER_HEADS):
                        r2 = tab_scr[jg, lg, hd * _JROWS:(hd + 1) * _JROWS, :]
                        p2 = tab_scr[jg, lg, (PEER_HEADS + hd) * _JROWS:(PEER_HEADS + hd + 1) * _JROWS, :]
                        terms.append(jnp.where(r2 < crow[hd], p2, jnp.zeros_like(p2)) * p1row[hd])
                    if link[chain] is not None:
                        terms[0] = terms[0] + _zero_after(link[chain])
                    while len(terms) > 1:
                        terms = [terms[k] + terms[k + 1] for k in range(0, len(terms), 2)]
                    g = terms[0]
                    gelu = 0.5 * a * (1.0 + lax.erf(a * _SQRT_HALF))
                    w = g * gelu.astype(BF16)
                    w_scr[prv, arow, lanes] = w
                    link[chain] = w

    for slot in range(2):
        pl.when(L % 2 == slot)(functools.partial(stages, slot))

    @pl.when((acc_blk >= 0) & (acc_blk % _PEER_STEPS == _PEER_STEPS - 1))
    def _():
        is_ctx, row = _mod_row(acc_blk // _PEER_STEPS, tb, n_ctx_tok, lat_len)
        ga2 = mod_ref[pl.ds(row, 1), 5 * D_MODEL:6 * D_MODEL]
        x2 = x1_ref[...] + ga2 * acc_scr[...].T
        y = _rms(x2) * gf_ref[...]

        @pl.when(is_ctx)
        def _():
            yc_ref[...] = y

        @pl.when(jnp.logical_not(is_ctx))
        def _():
            yl_ref[...] = y


def _peer(h2t, u_bf, vt_bf, r2, p2, cc, p1, x1, mod, g_final, *, tb, n_ctx_tok, lat_len):
    T = h2t.shape[1]
    nb = _PEER_NB
    n_tiles = T // tb
    nct = n_ctx_tok // tb
    n_blocks = n_tiles * _PEER_STEPS
    blk = lambda L, lag: jnp.clip(L - lag, 0, n_blocks - 1)
    tile = lambda L, lag: blk(L, lag) // _PEER_STEPS
    step = lambda L, lag: blk(L, lag) % _PEER_STEPS
    tab = pl.BlockSpec((PEER_HEADS, NKEYS, tb), lambda L: (0, 0, tile(L, 1)))
    rowtab = pl.BlockSpec((PEER_HEADS, nb // NKEYS, tb), lambda L: (0, step(L, 1), tile(L, 1)))
    return pl.pallas_call(
        functools.partial(_peer_kernel, tb=tb, n_ctx_tok=n_ctx_tok, lat_len=lat_len),
        grid=(n_blocks + 2,),
        in_specs=[pl.BlockSpec((D_MODEL, tb), lambda L: (0, tile(L, 0))),
                  pl.BlockSpec((nb, D_MODEL), lambda L: (step(L, 0), 0)),
                  pl.BlockSpec((D_MODEL, nb), lambda L: (0, step(L, 2))),
                  tab, tab, rowtab, rowtab,
                  pl.BlockSpec((tb, D_MODEL), lambda L: (tile(L, 2), 0)),
                  pl.BlockSpec(mod.shape, lambda L: (0, 0)),
                  pl.BlockSpec((1, D_MODEL), lambda L: (0, 0))],
        out_specs=[pl.BlockSpec((tb, D_MODEL), lambda L: (jnp.minimum(tile(L, 2), nct - 1), 0)),
                   pl.BlockSpec((tb, D_MODEL), lambda L: (jnp.maximum(tile(L, 2) - nct, 0), 0))],
        out_shape=[jax.ShapeDtypeStruct((n_ctx_tok, D_MODEL), F32),
                   jax.ShapeDtypeStruct((T - n_ctx_tok, D_MODEL), F32)],
        scratch_shapes=[pltpu.VMEM((2, nb, tb), F32), pltpu.VMEM((2, nb, tb), BF16), pltpu.VMEM((D_MODEL, tb), F32),
                        pltpu.VMEM((NKEYS // _JROWS, tb // LANES, 2 * PEER_HEADS * _JROWS, LANES), BF16)],
        compiler_params=_cparams(("arbitrary",)),
        name="peer",
    )(h2t, u_bf, vt_bf, r2, p2, cc, p1, x1, mod, g_final.reshape(1, -1))


def _layer(x_prompt, x_sample, state_C, state_n, state_m, c, c_ctx, w_ada, b_ada, g_norm1, w_in, b_igate, b_fgate,
           conv_w, conv_b, g_mlstm, w_out, g_norm2, peer_wq, peer_k1, peer_k2, peer_u, peer_v, g_final,
           *, tm, tc_lat, tb):
    B, S, _ = x_prompt.shape
    DB, DS, _ = x_sample.shape
    n_ctx_tok = B * S
    ng = N_DIRS * N_HEADS
    x_ctx = x_prompt.reshape(n_ctx_tok, D_MODEL)
    x_lat = x_sample.reshape(DB * DS, D_MODEL)
    cv8 = jnp.zeros((8, D_MODEL), F32).at[0].set(c_ctx).at[1:1 + DB].set(c)
    mod = _modulation(cv8, w_ada, b_ada)

    conv, q, k, v, og, gates, gates_t = _in_proj(
        x_ctx, x_lat, mod, g_norm1, w_in, b_igate, b_fgate, conv_w, conv_b,
        tm=tm, n_ctx_tok=n_ctx_tok, ctx_len=S, lat_len=DS)

    def ext_state(C, n_, m_):
        nb_ = jnp.broadcast_to(n_[..., None], n_.shape + (DH,))
        cext = jnp.concatenate([C, nb_], axis=-1).reshape(-1, ng, DH, 2 * DH)
        mext = jnp.broadcast_to(m_.reshape(-1, ng, 1), (C.shape[0], ng, LANES))
        return cext, mext

    z_c, z_m = ext_state(jnp.zeros((B, N_DIRS, N_HEADS, DH, DH), F32), jnp.zeros((B, N_DIRS, N_HEADS, DH), F32),
                         jnp.zeros((B, N_DIRS, N_HEADS), F32))
    hf_c, hb_c, c_fin, m_fin = _mlstm(q, k, v, gates, gates_t, z_c, z_m, tok0=0, nseq=B, seq_len=S, tc=S,
                                      write_state=True)
    l_c, l_m = ext_state(state_C, state_n, state_m)
    hf_l, hb_l = _mlstm(q, k, v, gates, gates_t, l_c, l_m, tok0=n_ctx_tok, nseq=DB, seq_len=DS, tc=tc_lat,
                        write_state=False)

    x1, h2t, scores = _mix(x_ctx, x_lat, conv, hf_c, hf_l, hb_c, hb_l, og, mod, g_mlstm, w_out, g_norm2,
                           peer_wq, peer_k1, peer_k2, tm=tm, n_ctx_tok=n_ctx_tok, lat_len=DS)
    r2, p2, cc, p1 = _select(scores)
    y_ctx, y_lat = _peer(h2t, peer_u.astype(BF16), peer_v.astype(BF16).T, r2, p2, cc, p1, x1, mod, g_final,
                         tb=tb, n_ctx_tok=n_ctx_tok, lat_len=DS)

    y_prompt = y_ctx.reshape(B, S, D_MODEL)
    y_sample = y_lat.reshape(DB, DS, D_MODEL)
    new_C = c_fin[..., :DH].reshape(B, 1, N_DIRS, N_HEADS, DH, DH)
    new_n = c_fin[..., DH].reshape(B, 1, N_DIRS, N_HEADS, DH)
    new_m = m_fin[..., 0].reshape(B, 1, N_DIRS, N_HEADS)
    return y_prompt, y_sample, new_C, new_n, new_m


def kernel(x_prompt, x_sample, state_C, state_n, state_m, c, c_ctx, w_ada, b_ada, g_norm1, w_in, b_igate, b_fgate,
           conv_w, conv_b, g_mlstm, w_out, g_norm2, peer_wq, peer_k1, peer_k2, peer_u, peer_v, g_final):
    return _layer(x_prompt, x_sample, state_C[:, 0], state_n[:, 0], state_m[:, 0], c, c_ctx, w_ada[0], b_ada[0],
                  g_norm1[0], w_in[0], b_igate[0], b_fgate[0], conv_w[0], conv_b[0], g_mlstm[0], w_out[0],
                  g_norm2[0], peer_wq[0], peer_k1[0], peer_k2[0], peer_u[0], peer_v[0], g_final,
                  tm=512, tc_lat=256, tb=512)
```

```python
import functools

import numpy as np
import jax
import jax.numpy as jnp
from jax import lax
from jax.experimental import pallas as pl
from jax.experimental.pallas import tpu as pltpu

F32 = jnp.float32
BF16 = jnp.bfloat16

D_MODEL = 1024
CONV_W = 512
N_HEADS = 4
DH = 128
MLSTM_W = N_HEADS * DH
N_DIRS = 2
GRID_W = 64
N_GROUPS = 7
GATE_COL0 = N_GROUPS * 512
N_GATES = 2 * N_DIRS * N_HEADS
PEER_HEADS = 8
NKEYS = 128
TOPK = 16
N_EXPERTS = NKEYS * NKEYS
EPS = 1e-6
LANES = 128
NEG_INF = float("-inf")

VMEM_LIMIT = 56 * 1024 * 1024


def _cparams(sem):
    return pltpu.CompilerParams(dimension_semantics=sem, vmem_limit_bytes=VMEM_LIMIT)


def _mod_kernel(cv_ref, w_ref, b_ref, o_ref):
    cv = cv_ref[...]
    s = cv * (1.0 / (1.0 + jnp.exp(-cv)))
    o_ref[...] = jnp.dot(s, w_ref[...], preferred_element_type=F32) + b_ref[...]


def _modulation(cv8, w_ada, b_ada):
    n = w_ada.shape[1]
    tn = 1024
    return pl.pallas_call(
        _mod_kernel,
        grid=(n // tn,),
        in_specs=[pl.BlockSpec((8, D_MODEL), lambda j: (0, 0)),
                  pl.BlockSpec((D_MODEL, tn), lambda j: (0, j)),
                  pl.BlockSpec((1, tn), lambda j: (0, j))],
        out_specs=pl.BlockSpec((8, tn), lambda j: (0, j)),
        out_shape=jax.ShapeDtypeStruct((8, n), F32),
        compiler_params=_cparams(("arbitrary",)),
        name="modulation",
    )(cv8, w_ada, b_ada.reshape(1, n))


def _mod_row(i, tm, n_ctx_tok, lat_len):
    n_ctx_tiles = n_ctx_tok // tm
    tiles_per_seq = lat_len // tm
    is_ctx = i < n_ctx_tiles
    row = jnp.where(is_ctx, 0, 1 + (i - n_ctx_tiles) // tiles_per_seq)
    return is_ctx, row


def _group_specs(tm, width, n_ctx_tok):
    nct = n_ctx_tok // tm
    return [pl.BlockSpec((tm, width), lambda i: (jnp.minimum(i, nct - 1), 0)),
            pl.BlockSpec((tm, width), lambda i: (jnp.maximum(i - nct, 0), 0))]


def _rms(x):
    return x * lax.rsqrt(jnp.mean(x * x, axis=-1, keepdims=True) + EPS)


def _log_sigmoid(z):
    return jnp.minimum(z, 0.0) - jnp.log(1.0 + jnp.exp(-jnp.abs(z)))


def _inproj_kernel(xc_ref, xl_ref, mod_ref, g1_ref, win_ref, wg_ref, wgt_ref, bg_ref, bgt_ref, cw_ref, cb_ref,
                   conv_ref, q_ref, k_ref, v_ref, o_ref, gates_ref, gatest_ref,
                   *, tm, n_ctx_tok, ctx_len, lat_len):
    i = pl.program_id(0)
    is_ctx, row = _mod_row(i, tm, n_ctx_tok, lat_len)
    sh1 = mod_ref[pl.ds(row, 1), 0:D_MODEL]
    sc1 = mod_ref[pl.ds(row, 1), D_MODEL:2 * D_MODEL]
    x = jnp.where(is_ctx, xc_ref[...], xl_ref[...])
    h = _rms(x) * g1_ref[...] * (1.0 + sc1) + sh1
    hb = h.astype(BF16)

    def grp(g):
        return jnp.dot(hb, win_ref[:, g * 512:(g + 1) * 512], preferred_element_type=F32)

    u = grp(1) * grp(2)
    pm = jnp.where(is_ctx, ctx_len - 1, GRID_W - 1)
    pos = lax.broadcasted_iota(jnp.int32, (tm, 1), 0) & pm
    u_prev = jnp.where(pos == 0, 0.0, pltpu.roll(u, 1, axis=0))
    u_next = jnp.where(pos == pm, 0.0, pltpu.roll(u, tm - 1, axis=0))
    y = u_prev * cw_ref[0:1, :] + u * cw_ref[1:2, :] + u_next * cw_ref[2:3, :] + cb_ref[...]
    conv_ref[...] = (grp(0) * y).astype(BF16)

    q_ref[...] = (grp(3) * (DH ** -0.5)).astype(BF16)
    k_ref[...] = grp(4).astype(BF16)
    v_ref[...] = grp(5).astype(BF16)
    o_ref[...] = grp(6)

    z = jnp.dot(hb, wg_ref[...], preferred_element_type=F32) + bg_ref[...]
    lane = lax.broadcasted_iota(jnp.int32, z.shape, 1)
    gates_ref[...] = jnp.where(lane >= N_DIRS * N_HEADS, _log_sigmoid(z), z)
    zt = lax.dot_general(wgt_ref[...], hb, (((1,), (1,)), ((), ())), preferred_element_type=F32) + bgt_ref[...]
    sub = lax.broadcasted_iota(jnp.int32, zt.shape, 0)
    gatest_ref[...] = jnp.where(sub >= N_DIRS * N_HEADS, _log_sigmoid(zt), zt)


def _in_proj(x_ctx, x_lat, mod, g1, w_in, b_igate, b_fgate, conv_w, conv_b, *, tm, n_ctx_tok, ctx_len, lat_len):
    T = x_ctx.shape[0] + x_lat.shape[0]
    win = w_in[:, :GATE_COL0].astype(BF16)
    wg = jnp.zeros((D_MODEL, LANES), F32).at[:, :N_GATES].set(w_in[:, GATE_COL0:]).astype(BF16)
    bgate = jnp.zeros((1, LANES), F32).at[0, :N_GATES].set(
        jnp.concatenate([b_igate.reshape(-1), b_fgate.reshape(-1)]))
    tok = lambda w: pl.BlockSpec((tm, w), lambda i: (i, 0))
    full = lambda a: pl.BlockSpec(a.shape, lambda i: (0,) * a.ndim)
    args = (x_ctx, x_lat, mod, g1.reshape(1, -1), win, wg, wg.T, bgate, bgate.T, conv_w, conv_b.reshape(1, -1))
    return pl.pallas_call(
        functools.partial(_inproj_kernel, tm=tm, n_ctx_tok=n_ctx_tok, ctx_len=ctx_len, lat_len=lat_len),
        grid=(T // tm,),
        in_specs=_group_specs(tm, D_MODEL, n_ctx_tok) + [full(a) for a in args[2:]],
        out_specs=[tok(CONV_W), tok(MLSTM_W), tok(MLSTM_W), tok(MLSTM_W), tok(MLSTM_W), tok(LANES),
                   pl.BlockSpec((LANES, tm), lambda i: (0, i))],
        out_shape=[jax.ShapeDtypeStruct((T, CONV_W), BF16),
                   jax.ShapeDtypeStruct((T, MLSTM_W), BF16),
                   jax.ShapeDtypeStruct((T, MLSTM_W), BF16),
                   jax.ShapeDtypeStruct((T, MLSTM_W), BF16),
                   jax.ShapeDtypeStruct((T, MLSTM_W), F32),
                   jax.ShapeDtypeStruct((T, LANES), F32),
                   jax.ShapeDtypeStruct((LANES, T), F32)],
        compiler_params=_cparams(("arbitrary",)),
        name="in_proj",
    )(*args)


def _split_bf16(x):
    hi = x.astype(BF16)
    return hi, (x - hi.astype(F32)).astype(BF16)


def _mlstm_kernel(qf_ref, kf_ref, vf_ref, gf_ref, gtf_ref, qb_ref, kb_ref, vb_ref, gb_ref, gtb_ref,
                  c0_ref, m0_ref, hf_ref, hb_ref, *rest, tc, write_state):
    if write_state:
        cout_ref, mout_ref, c_scr, m_scr = rest
    else:
        c_scr, m_scr = rest
    c = pl.program_id(1)

    @pl.when(c == 0)
    def _():
        c_scr[...] = c0_ref[...]
        m_scr[...] = m0_ref[...]

    r_io = lax.broadcasted_iota(jnp.int32, (tc, tc), 0)
    c_io = lax.broadcasted_iota(jnp.int32, (tc, tc), 1)
    ones = jnp.ones((tc, DH), BF16)
    wide = lambda x, width: jnp.concatenate([x] * (width // DH), axis=1)

    for d in range(N_DIRS):
        q_ref, k_ref, v_ref, g_ref, gt_ref, h_ref = (
            (qf_ref, kf_ref, vf_ref, gf_ref, gtf_ref, hf_ref) if d == 0 else
            (qb_ref, kb_ref, vb_ref, gb_ref, gtb_ref, hb_ref))
        causal = (c_io <= r_io) if d == 0 else (c_io >= r_io)
        last = tc - 1 if d == 0 else 0
        tri = causal.astype(BF16)
        tri_t = ((r_io <= c_io) if d == 0 else (r_io >= c_io)).astype(BF16)
        g = g_ref[...]
        gt = gt_ref[...]
        g_hi, g_lo = _split_bf16(g)
        gt_hi, gt_lo = _split_bf16(gt)
        bcum = (jnp.dot(tri, g_hi, preferred_element_type=F32) + jnp.dot(tri, g_lo, preferred_element_type=F32))
        bcum_t = (jnp.dot(gt_hi, tri_t, preferred_element_type=F32)
                  + jnp.dot(gt_lo, tri_t, preferred_element_type=F32))
        for hd in range(N_HEADS):
            gi = d * N_HEADS + hd
            gfi = N_DIRS * N_HEADS + gi
            sl = slice(hd * DH, (hd + 1) * DH)
            q = q_ref[:, sl]
            k = k_ref[:, sl]
            v = v_ref[:, sl]
            ig_c = jnp.broadcast_to(g[:, gi:gi + 1], (tc, DH))
            b_c = jnp.broadcast_to(bcum[:, gfi:gfi + 1], (tc, DH))
            ig_r = gt[gi:gi + 1, :]
            b_r = bcum_t[gfi:gfi + 1, :]
            m_prev = m_scr[gi:gi + 1, :]
            dmat = jnp.where(causal, wide(b_c, tc) - b_r + ig_r, NEG_INF)
            inter = b_c + m_prev
            m_t = jnp.maximum(inter, jnp.max(dmat, axis=-1, keepdims=True))
            s = (lax.dot_general(q, k, (((1,), (1,)), ((), ())), preferred_element_type=F32)
                 * jnp.exp(dmat - wide(m_t, tc)))
            a = jnp.exp(inter - m_t)
            cext = c_scr[gi]
            vext = jnp.concatenate([v, ones], axis=1)
            nd = (wide(a, 2 * DH) * jnp.dot(q, cext.astype(BF16), preferred_element_type=F32)
                  + jnp.dot(s.astype(BF16), vext, preferred_element_type=F32))
            num = nd[:, :DH]
            den = nd[:, DH:]
            h_ref[:, sl] = num / jnp.maximum(jnp.abs(den), jnp.exp(-m_t))
            m_new = m_t[last:last + 1, :]
            b_last = b_c[last:last + 1, :]
            w_c = jnp.exp(b_last - b_c + ig_c - m_new)
            decay = jnp.exp(b_last + m_prev - m_new)
            kw_t = (k.astype(F32) * w_c).T.astype(BF16)
            c_scr[gi] = wide(decay, 2 * DH) * cext + jnp.dot(kw_t, vext, preferred_element_type=F32)
            m_scr[gi:gi + 1, :] = m_new

    if write_state:
        @pl.when(c == pl.num_programs(1) - 1)
        def _():
            cout_ref[...] = c_scr[...]
            mout_ref[...] = m_scr[...]


def _mlstm(q, k, v, gates, gates_t, c0ext, m0, *, tok0, nseq, seq_len, tc, write_state):
    nc = seq_len // tc
    blk0 = tok0 // tc
    fwd = lambda b, c: b * nc + c
    bwd = lambda b, c: b * nc + nc - 1 - c
    tokspec = lambda w, f, o: pl.BlockSpec((tc, w), lambda b, c: (o + f(b, c), 0))
    gtspec = lambda f: pl.BlockSpec((LANES, tc), lambda b, c: (0, blk0 + f(b, c)))
    ng = N_DIRS * N_HEADS
    in_specs = [tokspec(MLSTM_W, fwd, blk0)] * 3 + [tokspec(LANES, fwd, blk0), gtspec(fwd)] \
        + [tokspec(MLSTM_W, bwd, blk0)] * 3 + [tokspec(LANES, bwd, blk0), gtspec(bwd)] \
        + [pl.BlockSpec((None, ng, DH, 2 * DH), lambda b, c: (b, 0, 0, 0)),
           pl.BlockSpec((None, ng, LANES), lambda b, c: (b, 0, 0))]
    out_specs = [tokspec(MLSTM_W, fwd, 0), tokspec(MLSTM_W, bwd, 0)]
    out_shape = [jax.ShapeDtypeStruct((nseq * seq_len, MLSTM_W), F32)] * 2
    if write_state:
        out_specs += [pl.BlockSpec((None, ng, DH, 2 * DH), lambda b, c: (b, 0, 0, 0)),
                      pl.BlockSpec((None, ng, LANES), lambda b, c: (b, 0, 0))]
        out_shape += [jax.ShapeDtypeStruct((nseq, ng, DH, 2 * DH), F32),
                      jax.ShapeDtypeStruct((nseq, ng, LANES), F32)]
    return pl.pallas_call(
        functools.partial(_mlstm_kernel, tc=tc, write_state=write_state),
        grid=(nseq, nc),
        in_specs=in_specs,
        out_specs=out_specs,
        out_shape=out_shape,
        scratch_shapes=[pltpu.VMEM((ng, DH, 2 * DH), F32), pltpu.VMEM((ng, LANES), F32)],
        compiler_params=_cparams(("arbitrary", "arbitrary")),
        name="mlstm_ctx" if write_state else "mlstm_lat",
    )(q, k, v, gates, gates_t, q, k, v, gates, gates_t, c0ext, m0)


def _mix_kernel(xc_ref, xl_ref, conv_ref, hfc_ref, hfl_ref, hbc_ref, hbl_ref, og_ref, mod_ref, gm_ref, wout_ref,
                g2_ref, wst_ref, x1_ref, h2t_ref, sc_ref, *, tm, n_ctx_tok, lat_len):
    i = pl.program_id(0)
    is_ctx, row = _mod_row(i, tm, n_ctx_tok, lat_len)
    mrow = lambda j: mod_ref[pl.ds(row, 1), j * D_MODEL:(j + 1) * D_MODEL]
    hs = jnp.where(is_ctx, hfc_ref[...] + hbc_ref[...], hfl_ref[...] + hbl_ref[...])
    hn = jnp.concatenate([_rms(hs[:, hd * DH:(hd + 1) * DH]) for hd in range(N_HEADS)], axis=1)
    og = og_ref[...]
    ml = (hn * gm_ref[...] * (1.0 / (1.0 + jnp.exp(-og)))).astype(BF16)
    mix = (jnp.dot(conv_ref[...], wout_ref[0:CONV_W, :], preferred_element_type=F32)
           + jnp.dot(ml, wout_ref[CONV_W:, :], preferred_element_type=F32))
    x1 = jnp.where(is_ctx, xc_ref[...], xl_ref[...]) + mrow(2) * mix
    x1_ref[...] = x1
    h2f = _rms(x1) * g2_ref[...] * (1.0 + mrow(4)) + mrow(3)
    h2t = h2f.T.astype(BF16)
    h2t_ref[...] = h2t
    sc = jnp.dot(wst_ref[...], h2t, preferred_element_type=F32)
    sc_ref[...] = sc.reshape(2 * PEER_HEADS, NKEYS, tm)


def _fold_keys_kernel(k_ref, wq_ref, o_ref):
    o_ref[...] = lax.dot_general(k_ref[...], wq_ref[...], (((1,), (1,)), ((), ())),
                                 preferred_element_type=F32).astype(BF16)


def _fold_keys(wq, k1, k2):
    n = 2 * PEER_HEADS
    keys = jnp.stack([k1, k2])
    return pl.pallas_call(
        _fold_keys_kernel,
        grid=(n,),
        in_specs=[pl.BlockSpec((None, NKEYS, NKEYS), lambda j: (j % 2, 0, 0)),
                  pl.BlockSpec((D_MODEL, NKEYS), lambda j: (0, j))],
        out_specs=pl.BlockSpec((NKEYS, D_MODEL), lambda j: (j, 0)),
        out_shape=jax.ShapeDtypeStruct((n * NKEYS, D_MODEL), BF16),
        compiler_params=_cparams(("arbitrary",)),
        name="fold_keys",
    )(keys, wq)


def _mix(x_ctx, x_lat, conv, hf_ctx, hf_lat, hb_ctx, hb_lat, og, mod, g_mlstm, w_out, g2, wq, k1, k2,
         *, tm, n_ctx_tok, lat_len):
    T = x_ctx.shape[0] + x_lat.shape[0]
    tok = lambda w: pl.BlockSpec((tm, w), lambda i: (i, 0))
    full = lambda a: pl.BlockSpec(a.shape, lambda i: (0,) * a.ndim)
    consts = (mod, g_mlstm.reshape(1, -1), w_out.astype(BF16), g2.reshape(1, -1), _fold_keys(wq, k1, k2))
    return pl.pallas_call(
        functools.partial(_mix_kernel, tm=tm, n_ctx_tok=n_ctx_tok, lat_len=lat_len),
        grid=(T // tm,),
        in_specs=_group_specs(tm, D_MODEL, n_ctx_tok) + [tok(CONV_W)] + _group_specs(tm, MLSTM_W, n_ctx_tok) * 2
        + [tok(MLSTM_W)] + [full(a) for a in consts],
        out_specs=[tok(D_MODEL), pl.BlockSpec((D_MODEL, tm), lambda i: (0, i)),
                   pl.BlockSpec((2 * PEER_HEADS, NKEYS, tm), lambda i: (0, 0, i))],
        out_shape=[jax.ShapeDtypeStruct((T, D_MODEL), F32),
                   jax.ShapeDtypeStruct((D_MODEL, T), BF16),
                   jax.ShapeDtypeStruct((2 * PEER_HEADS, NKEYS, T), F32)],
        compiler_params=_cparams(("arbitrary",)),
        name="mix",
    )(x_ctx, x_lat, conv, hf_ctx, hf_lat, hb_ctx, hb_lat, og, *consts)


def _sort16_pairs():
    def merge(lo, hi, r):
        step = r * 2
        if step < hi - lo:
            yield from merge(lo, hi, step)
            yield from merge(lo + r, hi, step)
            yield from ((i, i + r) for i in range(lo + r, hi - r, step))
        else:
            yield (lo, lo + r)

    def sort(lo, hi):
        if hi - lo >= 1:
            mid = lo + (hi - lo) // 2
            yield from sort(lo, mid)
            yield from sort(mid + 1, hi)
            yield from merge(lo, hi, 1)

    return tuple(sort(0, TOPK - 1))


_SORT16 = _sort16_pairs()
_CAND_PAIRS = tuple((r1, r2) for r1 in range(TOPK) for r2 in range(TOPK) if (r1 + 1) * (r2 + 1) <= TOPK)


def _cmpx(x, i, j):
    a, b = x[i], x[j]
    if b is None:
        return
    if a is None:
        x[i], x[j] = b, None
        return
    x[i], x[j] = jnp.maximum(a, b), jnp.minimum(a, b)


def _sort16(x):
    x = list(x)
    for i, j in _SORT16:
        _cmpx(x, i, j)
    return x


def _merge_top16(a, b):
    x = []
    for k in range(TOPK):
        p, q = a[k], b[TOPK - 1 - k]
        x.append(q if p is None else p if q is None else jnp.maximum(p, q))
    d = TOPK // 2
    while d:
        for k in range(TOPK):
            if not k & d:
                _cmpx(x, k, k + d)
        d //= 2
    return x


def _top16_values(groups):
    groups = [_sort16(g) for g in groups]
    while len(groups) > 1:
        groups = [_merge_top16(groups[k], groups[k + 1]) for k in range(0, len(groups), 2)]
    return groups[0]


def _count_gt(vs, s):
    g = jnp.where(vs[0] > s, 1.0, 0.0)
    for r in range(1, len(vs)):
        g = jnp.where(vs[r] > s, float(r + 1), g)
    return g


def _xpose8(x):
    x = list(x)
    sub = lax.broadcasted_iota(jnp.int32, (8, LANES), 0)
    for d in (4, 2, 1):
        keep = (sub & d) == 0
        for a in range(8):
            if not a & d:
                xa, xb = x[a], x[a + d]
                x[a] = jnp.where(keep, xa, pltpu.roll(xb, d, axis=0))
                x[a + d] = jnp.where(keep, pltpu.roll(xa, 8 - d, axis=0), xb)
    return x


def _select_kernel(sc_ref, r2_ref, p2_ref, c_ref, p1_ref, s_ref, t_scr):
    one = lambda m: jnp.where(m, 1.0, 0.0)
    lane_tile = lambda r: slice(r * LANES, (r + 1) * LANES)
    for h in range(2):
        for kg in range(NKEYS // 8):
            tiles = _xpose8([sc_ref[h, kg * 8:(kg + 1) * 8, lane_tile(r)] for r in range(8)])
            for k in range(8):
                s_ref[h, kg * 8 + k] = tiles[k]
    v = [_top16_values([[s_ref[h, g * TOPK + k] for k in range(TOPK)] for g in range(NKEYS // TOPK)])
         for h in range(2)]
    v1, v2 = v

    tie = jnp.zeros((8, LANES), F32)
    for h in range(2):
        n_ge = one(s_ref[h, 0] >= v[h][TOPK - 1])
        for j in range(1, NKEYS):
            n_ge = n_ge + one(s_ref[h, j] >= v[h][TOPK - 1])
        tie = jnp.maximum(tie, one(n_ge > float(TOPK)))
        for r in range(TOPK - 1):
            tie = jnp.maximum(tie, one(v[h][r] == v[h][r + 1]))
    has_tie = jnp.max(tie) > 0.0

    cand = {p: v1[p[0]] + v2[p[1]] for p in _CAND_PAIRS}
    rest = [cand[p] for p in _CAND_PAIRS if p[0] > 0]
    rest += [None] * (-len(rest) % TOPK)
    groups = [[cand[(0, r2)] for r2 in range(TOPK)]] + [rest[k:k + TOPK] for k in range(0, len(rest), TOPK)]
    while len(groups) & (len(groups) - 1):
        groups.append([None] * TOPK)
    w = _top16_values(groups)
    tau = w[TOPK - 1]
    n_gt = sum(one(cand[p] > tau) for p in _CAND_PAIRS)
    need = float(TOPK) - n_gt
    eq_seen = jnp.zeros((8, LANES), F32)
    counts = [jnp.zeros((8, LANES), F32) for _ in range(TOPK)]
    for p in _CAND_PAIRS:
        eq = cand[p] == tau
        counts[p[0]] = counts[p[0]] + one((cand[p] > tau) | (eq & (eq_seen < need)))
        eq_seen = eq_seen + one(eq)
    z = jnp.ones((8, LANES), F32)
    for r in range(1, TOPK):
        z = z + jnp.exp(w[r] - w[0])
    inv_z = 1.0 / z

    def tables(with_ties):
        for kg in range(NKEYS // 8):
            tabs = [[], [], [], []]
            for j in range(kg * 8, (kg + 1) * 8):
                s1 = s_ref[0, j]
                s2 = s_ref[1, j]
                if with_ties:
                    rank1 = _count_gt(v1, s1) + t_scr[0, j]
                    ckey = jnp.zeros((8, LANES), F32)
                    for r in range(TOPK):
                        ckey = jnp.where(rank1 == float(r), counts[r], ckey)
                    rank2 = _count_gt(v2, s2) + t_scr[1, j]
                else:
                    ckey = jnp.where(v1[0] > s1, counts[1], counts[0])
                    for r in range(1, TOPK - 1):
                        ckey = jnp.where(v1[r] > s1, counts[r + 1], ckey)
                    ckey = jnp.where(v1[TOPK - 1] > s1, 0.0, ckey)
                    rank2 = _count_gt(v2, s2)
                tabs[0].append(rank2)
                tabs[1].append(jnp.exp(s2 - v2[0]))
                tabs[2].append(ckey)
                tabs[3].append(jnp.exp(s1 - v1[0]) * inv_z)
            for ref, tab in zip((r2_ref, p2_ref, c_ref, p1_ref), tabs):
                for r, tile in enumerate(_xpose8(tab)):
                    ref[kg * 8:(kg + 1) * 8, lane_tile(r)] = tile

    @pl.when(has_tie)
    def _():
        t_scr[...] = jnp.zeros_like(t_scr)

        def lower_equal(jp, carry):
            for h in range(2):
                sv = s_ref[h, jp]
                for j in range(1, NKEYS):
                    inc = jnp.where(jp < j, 1.0, 0.0)
                    t_scr[h, j] = t_scr[h, j] + jnp.where(s_ref[h, j] == sv, inc, 0.0)
            return carry
        lax.fori_loop(0, NKEYS - 1, lower_equal, 0)
        tables(True)

    @pl.when(jnp.logical_not(has_tie))
    def _():
        tables(False)


def _select(scores):
    T = scores.shape[-1]
    tt = 8 * LANES
    tab = pl.BlockSpec((None, NKEYS, tt), lambda i, hd: (hd, 0, i))
    return pl.pallas_call(
        _select_kernel,
        grid=(T // tt, PEER_HEADS),
        in_specs=[pl.BlockSpec((2, NKEYS, tt), lambda i, hd: (hd, 0, i))],
        out_specs=[tab] * 4,
        out_shape=[jax.ShapeDtypeStruct((PEER_HEADS, NKEYS, T), F32)] * 4,
        scratch_shapes=[pltpu.VMEM((2, NKEYS, 8, LANES), F32), pltpu.VMEM((2, NKEYS, 8, LANES), F32)],
        compiler_params=_cparams(("arbitrary", "arbitrary")),
        name="select",
    )(scores)


_SQRT_HALF = 0.7071067811865476
_JROWS = 16
_PEER_NB = 8 * NKEYS
_PEER_STEPS = N_EXPERTS // _PEER_NB
_GATE_CHAINS = 1


def _zero_after(x):
    u = pltpu.bitcast(x, jnp.uint32)
    u = lax.shift_right_logical(lax.shift_right_logical(u, jnp.uint32(16)), jnp.uint32(16))
    return pltpu.bitcast(u, BF16)


def _peer_kernel(h2t_ref, u_ref, vt_ref, r2_ref, p2_ref, c_ref, p1_ref, x1_ref, mod_ref, gf_ref, yc_ref, yl_ref,
                 a_scr, w_scr, acc_scr, tab_scr, *, tb, n_ctx_tok, lat_len):
    L = pl.program_id(0)
    n_lg = tb // LANES
    vpu_blk = L - 1
    acc_blk = L - 2

    @pl.when(L == 0)
    def _():
        a_scr[...] = jnp.zeros_like(a_scr)
        w_scr[...] = jnp.zeros_like(w_scr)
        acc_scr[...] = jnp.zeros_like(acc_scr)

    @pl.when((L == 0) | (vpu_blk % _PEER_STEPS == 0))
    def _():
        for jg in range(NKEYS // _JROWS):
            rows = slice(jg * _JROWS, (jg + 1) * _JROWS)
            for lg in range(n_lg):
                lanes = slice(lg * LANES, (lg + 1) * LANES)
                for hd in range(PEER_HEADS):
                    tab_scr[jg, lg, hd * _JROWS:(hd + 1) * _JROWS, :] = r2_ref[hd, rows, lanes].astype(BF16)
                    tab_scr[jg, lg, (PEER_HEADS + hd) * _JROWS:(PEER_HEADS + hd + 1) * _JROWS, :] = \
                        p2_ref[hd, rows, lanes].astype(BF16)

    @pl.when((acc_blk >= 0) & (acc_blk % _PEER_STEPS == 0))
    def _():
        acc_scr[...] = jnp.zeros_like(acc_scr)

    def stages(cur):
        prv = 1 - cur
        a_scr[cur] = jnp.dot(u_ref[...], h2t_ref[...], preferred_element_type=F32)
        acc_scr[...] += jnp.dot(vt_ref[...], w_scr[cur], preferred_element_type=F32)
        link = [None] * _GATE_CHAINS
        for lg in range(n_lg):
            lanes = slice(lg * LANES, (lg + 1) * LANES)
            chain = lg * _GATE_CHAINS // n_lg
            for i8 in range(_PEER_NB // NKEYS):
                bcast = lambda ref, hd: jnp.broadcast_to(ref[hd, i8:i8 + 1, lanes], (_JROWS, LANES)).astype(BF16)
                crow = [bcast(c_ref, hd) for hd in range(PEER_HEADS)]
                p1row = [bcast(p1_ref, hd) for hd in range(PEER_HEADS)]
                for jg in range(NKEYS // _JROWS):
                    rows = slice(jg * _JROWS, (jg + 1) * _JROWS)
                    arow = slice(i8 * NKEYS + jg * _JROWS, i8 * NKEYS + (jg + 1) * _JROWS)
                    a = a_scr[prv, arow, lanes]
                    terms = []
                    for hd in range(PEER_HEADS):
                        r2 = tab_scr[jg, lg, hd * _JROWS:(hd + 1) * _JROWS, :]
                        p2 = tab_scr[jg, lg, (PEER_HEADS + hd) * _JROWS:(PEER_HEADS + hd + 1) * _JROWS, :]
                        terms.append(jnp.where(r2 < crow[hd], p2, jnp.zeros_like(p2)) * p1row[hd])
                    if link[chain] is not None:
                        terms[0] = terms[0] + _zero_after(link[chain])
                    while len(terms) > 1:
                        terms = [terms[k] + terms[k + 1] for k in range(0, len(terms), 2)]
                    g = terms[0]
                    gelu = 0.5 * a * (1.0 + lax.erf(a * _SQRT_HALF))
                    w = g * gelu.astype(BF16)
                    w_scr[prv, arow, lanes] = w
                    link[chain] = w

    for slot in range(2):
        pl.when(L % 2 == slot)(functools.partial(stages, slot))

    @pl.when((acc_blk >= 0) & (acc_blk % _PEER_STEPS == _PEER_STEPS - 1))
    def _():
        is_ctx, row = _mod_row(acc_blk // _PEER_STEPS, tb, n_ctx_tok, lat_len)
        ga2 = mod_ref[pl.ds(row, 1), 5 * D_MODEL:6 * D_MODEL]
        x2 = x1_ref[...] + ga2 * acc_scr[...].T
        y = _rms(x2) * gf_ref[...]

        @pl.when(is_ctx)
        def _():
            yc_ref[...] = y

        @pl.when(jnp.logical_not(is_ctx))
        def _():
            yl_ref[...] = y


def _peer(h2t, u_bf, vt_bf, r2, p2, cc, p1, x1, mod, g_final, *, tb, n_ctx_tok, lat_len):
    T = h2t.shape[1]
    nb = _PEER_NB
    n_tiles = T // tb
    nct = n_ctx_tok // tb
    n_blocks = n_tiles * _PEER_STEPS
    blk = lambda L, lag: jnp.clip(L - lag, 0, n_blocks - 1)
    tile = lambda L, lag: blk(L, lag) // _PEER_STEPS
    step = lambda L, lag: blk(L, lag) % _PEER_STEPS
    tab = pl.BlockSpec((PEER_HEADS, NKEYS, tb), lambda L: (0, 0, tile(L, 1)))
    rowtab = pl.BlockSpec((PEER_HEADS, nb // NKEYS, tb), lambda L: (0, step(L, 1), tile(L, 1)))
    return pl.pallas_call(
        functools.partial(_peer_kernel, tb=tb, n_ctx_tok=n_ctx_tok, lat_len=lat_len),
        grid=(n_blocks + 2,),
        in_specs=[pl.BlockSpec((D_MODEL, tb), lambda L: (0, tile(L, 0))),
                  pl.BlockSpec((nb, D_MODEL), lambda L: (step(L, 0), 0)),
                  pl.BlockSpec((D_MODEL, nb), lambda L: (0, step(L, 2))),
                  tab, tab, rowtab, rowtab,
                  pl.BlockSpec((tb, D_MODEL), lambda L: (tile(L, 2), 0)),
                  pl.BlockSpec(mod.shape, lambda L: (0, 0)),
                  pl.BlockSpec((1, D_MODEL), lambda L: (0, 0))],
        out_specs=[pl.BlockSpec((tb, D_MODEL), lambda L: (jnp.minimum(tile(L, 2), nct - 1), 0)),
                   pl.BlockSpec((tb, D_MODEL), lambda L: (jnp.maximum(tile(L, 2) - nct, 0), 0))],
        out_shape=[jax.ShapeDtypeStruct((n_ctx_tok, D_MODEL), F32),
                   jax.ShapeDtypeStruct((T - n_ctx_tok, D_MODEL), F32)],
        scratch_shapes=[pltpu.VMEM((2, nb, tb), F32), pltpu.VMEM((2, nb, tb), BF16), pltpu.VMEM((D_MODEL, tb), F32),
                        pltpu.VMEM((NKEYS // _JROWS, tb // LANES, 2 * PEER_HEADS * _JROWS, LANES), BF16)],
        compiler_params=_cparams(("arbitrary",)),
        name="peer",
    )(h2t, u_bf, vt_bf, r2, p2, cc, p1, x1, mod, g_final.reshape(1, -1))


def _layer(x_prompt, x_sample, state_C, state_n, state_m, c, c_ctx, w_ada, b_ada, g_norm1, w_in, b_igate, b_fgate,
           conv_w, conv_b, g_mlstm, w_out, g_norm2, peer_wq, peer_k1, peer_k2, peer_u, peer_v, g_final,
           *, tm, tc_lat, tb):
    B, S, _ = x_prompt.shape
    DB, DS, _ = x_sample.shape
    n_ctx_tok = B * S
    ng = N_DIRS * N_HEADS
    x_ctx = x_prompt.reshape(n_ctx_tok, D_MODEL)
    x_lat = x_sample.reshape(DB * DS, D_MODEL)
    cv8 = jnp.zeros((8, D_MODEL), F32).at[0].set(c_ctx).at[1:1 + DB].set(c)
    mod = _modulation(cv8, w_ada, b_ada)

    conv, q, k, v, og, gates, gates_t = _in_proj(
        x_ctx, x_lat, mod, g_norm1, w_in, b_igate, b_fgate, conv_w, conv_b,
        tm=tm, n_ctx_tok=n_ctx_tok, ctx_len=S, lat_len=DS)

    def ext_state(C, n_, m_):
        nb_ = jnp.broadcast_to(n_[..., None], n_.shape + (DH,))
        cext = jnp.concatenate([C, nb_], axis=-1).reshape(-1, ng, DH, 2 * DH)
        mext = jnp.broadcast_to(m_.reshape(-1, ng, 1), (C.shape[0], ng, LANES))
        return cext, mext

    z_c, z_m = ext_state(jnp.zeros((B, N_DIRS, N_HEADS, DH, DH), F32), jnp.zeros((B, N_DIRS, N_HEADS, DH), F32),
                         jnp.zeros((B, N_DIRS, N_HEADS), F32))
    hf_c, hb_c, c_fin, m_fin = _mlstm(q, k, v, gates, gates_t, z_c, z_m, tok0=0, nseq=B, seq_len=S, tc=S,
                                      write_state=True)
    l_c, l_m = ext_state(state_C, state_n, state_m)
    hf_l, hb_l = _mlstm(q, k, v, gates, gates_t, l_c, l_m, tok0=n_ctx_tok, nseq=DB, seq_len=DS, tc=tc_lat,
                        write_state=False)

    x1, h2t, scores = _mix(x_ctx, x_lat, conv, hf_c, hf_l, hb_c, hb_l, og, mod, g_mlstm, w_out, g_norm2,
                           peer_wq, peer_k1, peer_k2, tm=tm, n_ctx_tok=n_ctx_tok, lat_len=DS)
    r2, p2, cc, p1 = _select(scores)
    y_ctx, y_lat = _peer(h2t, peer_u.astype(BF16), peer_v.astype(BF16).T, r2, p2, cc, p1, x1, mod, g_final,
                         tb=tb, n_ctx_tok=n_ctx_tok, lat_len=DS)

    y_prompt = y_ctx.reshape(B, S, D_MODEL)
    y_sample = y_lat.reshape(DB, DS, D_MODEL)
    new_C = c_fin[..., :DH].reshape(B, 1, N_DIRS, N_HEADS, DH, DH)
    new_n = c_fin[..., DH].reshape(B, 1, N_DIRS, N_HEADS, DH)
    new_m = m_fin[..., 0].reshape(B, 1, N_DIRS, N_HEADS)
    return y_prompt, y_sample, new_C, new_n, new_m


def kernel(x_prompt, x_sample, state_C, state_n, state_m, c, c_ctx, w_ada, b_ada, g_norm1, w_in, b_igate, b_fgate,
           conv_w, conv_b, g_mlstm, w_out, g_norm2, peer_wq, peer_k1, peer_k2, peer_u, peer_v, g_final):
    return _layer(x_prompt, x_sample, state_C[:, 0], state_n[:, 0], state_m[:, 0], c, c_ctx, w_ada[0], b_ada[0],
                  g_norm1[0], w_in[0], b_igate[0], b_fgate[0], conv_w[0], conv_b[0], g_mlstm[0], w_out[0],
                  g_norm2[0], peer_wq[0], peer_k1[0], peer_k2[0], peer_u[0], peer_v[0], g_final,
                  tm=512, tc_lat=256, tb=512)
```

```python
import functools

import numpy as np
import jax
import jax.numpy as jnp
from jax import lax
from jax.experimental import pallas as pl
from jax.experimental.pallas import tpu as pltpu

F32 = jnp.float32
BF16 = jnp.bfloat16

D_MODEL = 1024
CONV_W = 512
N_HEADS = 4
DH = 128
MLSTM_W = N_HEADS * DH
N_DIRS = 2
GRID_W = 64
N_GROUPS = 7
GATE_COL0 = N_GROUPS * 512
N_GATES = 2 * N_DIRS * N_HEADS
PEER_HEADS = 8
NKEYS = 128
TOPK = 16
N_EXPERTS = NKEYS * NKEYS
EPS = 1e-6
LANES = 128
NEG_INF = float("-inf")

VMEM_LIMIT = 56 * 1024 * 1024


def _cparams(sem):
    return pltpu.CompilerParams(dimension_semantics=sem, vmem_limit_bytes=VMEM_LIMIT)


def _mod_kernel(cv_ref, w_ref, b_ref, o_ref):
    cv = cv_ref[...]
    s = cv * (1.0 / (1.0 + jnp.exp(-cv)))
    o_ref[...] = jnp.dot(s, w_ref[...], preferred_element_type=F32) + b_ref[...]


def _modulation(cv8, w_ada, b_ada):
    n = w_ada.shape[1]
    tn = 1024
    return pl.pallas_call(
        _mod_kernel,
        grid=(n // tn,),
        in_specs=[pl.BlockSpec((8, D_MODEL), lambda j: (0, 0)),
                  pl.BlockSpec((D_MODEL, tn), lambda j: (0, j)),
                  pl.BlockSpec((1, tn), lambda j: (0, j))],
        out_specs=pl.BlockSpec((8, tn), lambda j: (0, j)),
        out_shape=jax.ShapeDtypeStruct((8, n), F32),
        compiler_params=_cparams(("arbitrary",)),
        name="modulation",
    )(cv8, w_ada, b_ada.reshape(1, n))


def _mod_row(i, tm, n_ctx_tok, lat_len):
    n_ctx_tiles = n_ctx_tok // tm
    tiles_per_seq = lat_len // tm
    is_ctx = i < n_ctx_tiles
    row = jnp.where(is_ctx, 0, 1 + (i - n_ctx_tiles) // tiles_per_seq)
    return is_ctx, row


def _group_specs(tm, width, n_ctx_tok):
    nct = n_ctx_tok // tm
    return [pl.BlockSpec((tm, width), lambda i: (jnp.minimum(i, nct - 1), 0)),
            pl.BlockSpec((tm, width), lambda i: (jnp.maximum(i - nct, 0), 0))]


def _rms(x):
    return x * lax.rsqrt(jnp.mean(x * x, axis=-1, keepdims=True) + EPS)


def _log_sigmoid(z):
    return jnp.minimum(z, 0.0) - jnp.log(1.0 + jnp.exp(-jnp.abs(z)))


def _inproj_kernel(xc_ref, xl_ref, mod_ref, g1_ref, win_ref, wg_ref, wgt_ref, bg_ref, bgt_ref, cw_ref, cb_ref,
                   conv_ref, q_ref, k_ref, v_ref, o_ref, gates_ref, gatest_ref,
                   *, tm, n_ctx_tok, ctx_len, lat_len):
    i = pl.program_id(0)
    is_ctx, row = _mod_row(i, tm, n_ctx_tok, lat_len)
    sh1 = mod_ref[pl.ds(row, 1), 0:D_MODEL]
    sc1 = mod_ref[pl.ds(row, 1), D_MODEL:2 * D_MODEL]
    x = jnp.where(is_ctx, xc_ref[...], xl_ref[...])
    h = _rms(x) * g1_ref[...] * (1.0 + sc1) + sh1
    hb = h.astype(BF16)

    def grp(g):
        return jnp.dot(hb, win_ref[:, g * 512:(g + 1) * 512], preferred_element_type=F32)

    u = grp(1) * grp(2)
    pm = jnp.where(is_ctx, ctx_len - 1, GRID_W - 1)
    pos = lax.broadcasted_iota(jnp.int32, (tm, 1), 0) & pm
    u_prev = jnp.where(pos == 0, 0.0, pltpu.roll(u, 1, axis=0))
    u_next = jnp.where(pos == pm, 0.0, pltpu.roll(u, tm - 1, axis=0))
    y = u_prev * cw_ref[0:1, :] + u * cw_ref[1:2, :] + u_next * cw_ref[2:3, :] + cb_ref[...]
    conv_ref[...] = (grp(0) * y).astype(BF16)

    q_ref[...] = (grp(3) * (DH ** -0.5)).astype(BF16)
    k_ref[...] = grp(4).astype(BF16)
    v_ref[...] = grp(5).astype(BF16)
    o_ref[...] = grp(6)

    z = jnp.dot(hb, wg_ref[...], preferred_element_type=F32) + bg_ref[...]
    lane = lax.broadcasted_iota(jnp.int32, z.shape, 1)
    gates_ref[...] = jnp.where(lane >= N_DIRS * N_HEADS, _log_sigmoid(z), z)
    zt = lax.dot_general(wgt_ref[...], hb, (((1,), (1,)), ((), ())), preferred_element_type=F32) + bgt_ref[...]
    sub = lax.broadcasted_iota(jnp.int32, zt.shape, 0)
    gatest_ref[...] = jnp.where(sub >= N_DIRS * N_HEADS, _log_sigmoid(zt), zt)


def _in_proj(x_ctx, x_lat, mod, g1, w_in, b_igate, b_fgate, conv_w, conv_b, *, tm, n_ctx_tok, ctx_len, lat_len):
    T = x_ctx.shape[0] + x_lat.shape[0]
    win = w_in[:, :GATE_COL0].astype(BF16)
    wg = jnp.zeros((D_MODEL, LANES), F32).at[:, :N_GATES].set(w_in[:, GATE_COL0:]).astype(BF16)
    bgate = jnp.zeros((1, LANES), F32).at[0, :N_GATES].set(
        jnp.concatenate([b_igate.reshape(-1), b_fgate.reshape(-1)]))
    tok = lambda w: pl.BlockSpec((tm, w), lambda i: (i, 0))
    full = lambda a: pl.BlockSpec(a.shape, lambda i: (0,) * a.ndim)
    args = (x_ctx, x_lat, mod, g1.reshape(1, -1), win, wg, wg.T, bgate, bgate.T, conv_w, conv_b.reshape(1, -1))
    return pl.pallas_call(
        functools.partial(_inproj_kernel, tm=tm, n_ctx_tok=n_ctx_tok, ctx_len=ctx_len, lat_len=lat_len),
        grid=(T // tm,),
        in_specs=_group_specs(tm, D_MODEL, n_ctx_tok) + [full(a) for a in args[2:]],
        out_specs=[tok(CONV_W), tok(MLSTM_W), tok(MLSTM_W), tok(MLSTM_W), tok(MLSTM_W), tok(LANES),
                   pl.BlockSpec((LANES, tm), lambda i: (0, i))],
        out_shape=[jax.ShapeDtypeStruct((T, CONV_W), BF16),
                   jax.ShapeDtypeStruct((T, MLSTM_W), BF16),
                   jax.ShapeDtypeStruct((T, MLSTM_W), BF16),
                   jax.ShapeDtypeStruct((T, MLSTM_W), BF16),
                   jax.ShapeDtypeStruct((T, MLSTM_W), F32),
                   jax.ShapeDtypeStruct((T, LANES), F32),
                   jax.ShapeDtypeStruct((LANES, T), F32)],
        compiler_params=_cparams(("arbitrary",)),
        name="in_proj",
    )(*args)


def _split_bf16(x):
    hi = x.astype(BF16)
    return hi, (x - hi.astype(F32)).astype(BF16)


def _mlstm_kernel(qf_ref, kf_ref, vf_ref, gf_ref, gtf_ref, qb_ref, kb_ref, vb_ref, gb_ref, gtb_ref,
                  c0_ref, m0_ref, hf_ref, hb_ref, *rest, tc, write_state):
    if write_state:
        cout_ref, mout_ref, c_scr, m_scr = rest
    else:
        c_scr, m_scr = rest
    c = pl.program_id(1)

    @pl.when(c == 0)
    def _():
        c_scr[...] = c0_ref[...]
        m_scr[...] = m0_ref[...]

    r_io = lax.broadcasted_iota(jnp.int32, (tc, tc), 0)
    c_io = lax.broadcasted_iota(jnp.int32, (tc, tc), 1)
    ones = jnp.ones((tc, DH), BF16)
    wide = lambda x, width: jnp.concatenate([x] * (width // DH), axis=1)

    for d in range(N_DIRS):
        q_ref, k_ref, v_ref, g_ref, gt_ref, h_ref = (
            (qf_ref, kf_ref, vf_ref, gf_ref, gtf_ref, hf_ref) if d == 0 else
            (qb_ref, kb_ref, vb_ref, gb_ref, gtb_ref, hb_ref))
        causal = (c_io <= r_io) if d == 0 else (c_io >= r_io)
        last = tc - 1 if d == 0 else 0
        tri = causal.astype(BF16)
        tri_t = ((r_io <= c_io) if d == 0 else (r_io >= c_io)).astype(BF16)
        g = g_ref[...]
        gt = gt_ref[...]
        g_hi, g_lo = _split_bf16(g)
        gt_hi, gt_lo = _split_bf16(gt)
        bcum = (jnp.dot(tri, g_hi, preferred_element_type=F32) + jnp.dot(tri, g_lo, preferred_element_type=F32))
        bcum_t = (jnp.dot(gt_hi, tri_t, preferred_element_type=F32)
                  + jnp.dot(gt_lo, tri_t, preferred_element_type=F32))
        for hd in range(N_HEADS):
            gi = d * N_HEADS + hd
            gfi = N_DIRS * N_HEADS + gi
            sl = slice(hd * DH, (hd + 1) * DH)
            q = q_ref[:, sl]
            k = k_ref[:, sl]
            v = v_ref[:, sl]
            ig_c = jnp.broadcast_to(g[:, gi:gi + 1], (tc, DH))
            b_c = jnp.broadcast_to(bcum[:, gfi:gfi + 1], (tc, DH))
            ig_r = gt[gi:gi + 1, :]
            b_r = bcum_t[gfi:gfi + 1, :]
            m_prev = m_scr[gi:gi + 1, :]
            dmat = jnp.where(causal, wide(b_c, tc) - b_r + ig_r, NEG_INF)
            inter = b_c + m_prev
            m_t = jnp.maximum(inter, jnp.max(dmat, axis=-1, keepdims=True))
            s = (lax.dot_general(q, k, (((1,), (1,)), ((), ())), preferred_element_type=F32)
                 * jnp.exp(dmat - wide(m_t, tc)))
            a = jnp.exp(inter - m_t)
            cext = c_scr[gi]
            vext = jnp.concatenate([v, ones], axis=1)
            nd = (wide(a, 2 * DH) * jnp.dot(q, cext.astype(BF16), preferred_element_type=F32)
                  + jnp.dot(s.astype(BF16), vext, preferred_element_type=F32))
            num = nd[:, :DH]
            den = nd[:, DH:]
            h_ref[:, sl] = num / jnp.maximum(jnp.abs(den), jnp.exp(-m_t))
            m_new = m_t[last:last + 1, :]
            b_last = b_c[last:last + 1, :]
            w_c = jnp.exp(b_last - b_c + ig_c - m_new)
            decay = jnp.exp(b_last + m_prev - m_new)
            kw_t = (k.astype(F32) * w_c).T.astype(BF16)
            c_scr[gi] = wide(decay, 2 * DH) * cext + jnp.dot(kw_t, vext, preferred_element_type=F32)
            m_scr[gi:gi + 1, :] = m_new

    if write_state:
        @pl.when(c == pl.num_programs(1) - 1)
        def _():
            cout_ref[...] = c_scr[...]
            mout_ref[...] = m_scr[...]


def _mlstm(q, k, v, gates, gates_t, c0ext, m0, *, tok0, nseq, seq_len, tc, write_state):
    nc = seq_len // tc
    blk0 = tok0 // tc
    fwd = lambda b, c: b * nc + c
    bwd = lambda b, c: b * nc + nc - 1 - c
    tokspec = lambda w, f, o: pl.BlockSpec((tc, w), lambda b, c: (o + f(b, c), 0))
    gtspec = lambda f: pl.BlockSpec((LANES, tc), lambda b, c: (0, blk0 + f(b, c)))
    ng = N_DIRS * N_HEADS
    in_specs = [tokspec(MLSTM_W, fwd, blk0)] * 3 + [tokspec(LANES, fwd, blk0), gtspec(fwd)] \
        + [tokspec(MLSTM_W, bwd, blk0)] * 3 + [tokspec(LANES, bwd, blk0), gtspec(bwd)] \
        + [pl.BlockSpec((None, ng, DH, 2 * DH), lambda b, c: (b, 0, 0, 0)),
           pl.BlockSpec((None, ng, LANES), lambda b, c: (b, 0, 0))]
    out_specs = [tokspec(MLSTM_W, fwd, 0), tokspec(MLSTM_W, bwd, 0)]
    out_shape = [jax.ShapeDtypeStruct((nseq * seq_len, MLSTM_W), F32)] * 2
    if write_state:
        out_specs += [pl.BlockSpec((None, ng, DH, 2 * DH), lambda b, c: (b, 0, 0, 0)),
                      pl.BlockSpec((None, ng, LANES), lambda b, c: (b, 0, 0))]
        out_shape += [jax.ShapeDtypeStruct((nseq, ng, DH, 2 * DH), F32),
                      jax.ShapeDtypeStruct((nseq, ng, LANES), F32)]
    return pl.pallas_call(
        functools.partial(_mlstm_kernel, tc=tc, write_state=write_state),
        grid=(nseq, nc),
        in_specs=in_specs,
        out_specs=out_specs,
        out_shape=out_shape,
        scratch_shapes=[pltpu.VMEM((ng, DH, 2 * DH), F32), pltpu.VMEM((ng, LANES), F32)],
        compiler_params=_cparams(("arbitrary", "arbitrary")),
        name="mlstm_ctx" if write_state else "mlstm_lat",
    )(q, k, v, gates, gates_t, q, k, v, gates, gates_t, c0ext, m0)


def _mix_kernel(xc_ref, xl_ref, conv_ref, hfc_ref, hfl_ref, hbc_ref, hbl_ref, og_ref, mod_ref, gm_ref, wout_ref,
                g2_ref, wst_ref, x1_ref, h2t_ref, sc_ref, *, tm, n_ctx_tok, lat_len):
    i = pl.program_id(0)
    is_ctx, row = _mod_row(i, tm, n_ctx_tok, lat_len)
    mrow = lambda j: mod_ref[pl.ds(row, 1), j * D_MODEL:(j + 1) * D_MODEL]
    hs = jnp.where(is_ctx, hfc_ref[...] + hbc_ref[...], hfl_ref[...] + hbl_ref[...])
    hn = jnp.concatenate([_rms(hs[:, hd * DH:(hd + 1) * DH]) for hd in range(N_HEADS)], axis=1)
    og = og_ref[...]
    ml = (hn * gm_ref[...] * (1.0 / (1.0 + jnp.exp(-og)))).astype(BF16)
    mix = (jnp.dot(conv_ref[...], wout_ref[0:CONV_W, :], preferred_element_type=F32)
           + jnp.dot(ml, wout_ref[CONV_W:, :], preferred_element_type=F32))
    x1 = jnp.where(is_ctx, xc_ref[...], xl_ref[...]) + mrow(2) * mix
    x1_ref[...] = x1
    h2f = _rms(x1) * g2_ref[...] * (1.0 + mrow(4)) + mrow(3)
    h2t = h2f.T.astype(BF16)
    h2t_ref[...] = h2t
    sc = jnp.dot(wst_ref[...], h2t, preferred_element_type=F32)
    sc_ref[...] = sc.reshape(2 * PEER_HEADS, NKEYS, tm)


def _fold_keys_kernel(k_ref, wq_ref, o_ref):
    o_ref[...] = lax.dot_general(k_ref[...], wq_ref[...], (((1,), (1,)), ((), ())),
                                 preferred_element_type=F32).astype(BF16)


def _fold_keys(wq, k1, k2):
    n = 2 * PEER_HEADS
    keys = jnp.stack([k1, k2])
    return pl.pallas_call(
        _fold_keys_kernel,
        grid=(n,),
        in_specs=[pl.BlockSpec((None, NKEYS, NKEYS), lambda j: (j % 2, 0, 0)),
                  pl.BlockSpec((D_MODEL, NKEYS), lambda j: (0, j))],
        out_specs=pl.BlockSpec((NKEYS, D_MODEL), lambda j: (j, 0)),
        out_shape=jax.ShapeDtypeStruct((n * NKEYS, D_MODEL), BF16),
        compiler_params=_cparams(("arbitrary",)),
        name="fold_keys",
    )(keys, wq)


def _mix(x_ctx, x_lat, conv, hf_ctx, hf_lat, hb_ctx, hb_lat, og, mod, g_mlstm, w_out, g2, wq, k1, k2,
         *, tm, n_ctx_tok, lat_len):
    T = x_ctx.shape[0] + x_lat.shape[0]
    tok = lambda w: pl.BlockSpec((tm, w), lambda i: (i, 0))
    full = lambda a: pl.BlockSpec(a.shape, lambda i: (0,) * a.ndim)
    consts = (mod, g_mlstm.reshape(1, -1), w_out.astype(BF16), g2.reshape(1, -1), _fold_keys(wq, k1, k2))
    return pl.pallas_call(
        functools.partial(_mix_kernel, tm=tm, n_ctx_tok=n_ctx_tok, lat_len=lat_len),
        grid=(T // tm,),
        in_specs=_group_specs(tm, D_MODEL, n_ctx_tok) + [tok(CONV_W)] + _group_specs(tm, MLSTM_W, n_ctx_tok) * 2
        + [tok(MLSTM_W)] + [full(a) for a in consts],
        out_specs=[tok(D_MODEL), pl.BlockSpec((D_MODEL, tm), lambda i: (0, i)),
                   pl.BlockSpec((2 * PEER_HEADS, NKEYS, tm), lambda i: (0, 0, i))],
        out_shape=[jax.ShapeDtypeStruct((T, D_MODEL), F32),
                   jax.ShapeDtypeStruct((D_MODEL, T), BF16),
                   jax.ShapeDtypeStruct((2 * PEER_HEADS, NKEYS, T), F32)],
        compiler_params=_cparams(("arbitrary",)),
        name="mix",
    )(x_ctx, x_lat, conv, hf_ctx, hf_lat, hb_ctx, hb_lat, og, *consts)


def _sort16_pairs():
    def merge(lo, hi, r):
        step = r * 2
        if step < hi - lo:
            yield from merge(lo, hi, step)
            yield from merge(lo + r, hi, step)
            yield from ((i, i + r) for i in range(lo + r, hi - r, step))
        else:
            yield (lo, lo + r)

    def sort(lo, hi):
        if hi - lo >= 1:
            mid = lo + (hi - lo) // 2
            yield from sort(lo, mid)
            yield from sort(mid + 1, hi)
            yield from merge(lo, hi, 1)

    return tuple(sort(0, TOPK - 1))


_SORT16 = _sort16_pairs()
_CAND_PAIRS = tuple((r1, r2) for r1 in range(TOPK) for r2 in range(TOPK) if (r1 + 1) * (r2 + 1) <= TOPK)


def _cmpx(x, i, j):
    a, b = x[i], x[j]
    if b is None:
        return
    if a is None:
        x[i], x[j] = b, None
        return
    x[i], x[j] = jnp.maximum(a, b), jnp.minimum(a, b)


def _sort16(x):
    x = list(x)
    for i, j in _SORT16:
        _cmpx(x, i, j)
    return x


def _merge_top16(a, b):
    x = []
    for k in range(TOPK):
        p, q = a[k], b[TOPK - 1 - k]
        x.append(q if p is None else p if q is None else jnp.maximum(p, q))
    d = TOPK // 2
    while d:
        for k in range(TOPK):
            if not k & d:
                _cmpx(x, k, k + d)
        d //= 2
    return x


def _top16_values(groups):
    groups = [_sort16(g) for g in groups]
    while len(groups) > 1:
        groups = [_merge_top16(groups[k], groups[k + 1]) for k in range(0, len(groups), 2)]
    return groups[0]


def _count_gt(vs, s):
    g = jnp.where(vs[0] > s, 1.0, 0.0)
    for r in range(1, len(vs)):
        g = jnp.where(vs[r] > s, float(r + 1), g)
    return g


def _xpose8(x):
    x = list(x)
    sub = lax.broadcasted_iota(jnp.int32, (8, LANES), 0)
    for d in (4, 2, 1):
        keep = (sub & d) == 0
        for a in range(8):
            if not a & d:
                xa, xb = x[a], x[a + d]
                x[a] = jnp.where(keep, xa, pltpu.roll(xb, d, axis=0))
                x[a + d] = jnp.where(keep, pltpu.roll(xa, 8 - d, axis=0), xb)
    return x


def _select_kernel(sc_ref, r2_ref, p2_ref, c_ref, p1_ref, s_ref, t_scr):
    one = lambda m: jnp.where(m, 1.0, 0.0)
    lane_tile = lambda r: slice(r * LANES, (r + 1) * LANES)
    for h in range(2):
        for kg in range(NKEYS // 8):
            tiles = _xpose8([sc_ref[h, kg * 8:(kg + 1) * 8, lane_tile(r)] for r in range(8)])
            for k in range(8):
                s_ref[h, kg * 8 + k] = tiles[k]
    v = [_top16_values([[s_ref[h, g * TOPK + k] for k in range(TOPK)] for g in range(NKEYS // TOPK)])
         for h in range(2)]
    v1, v2 = v

    tie = jnp.zeros((8, LANES), F32)
    for h in range(2):
        n_ge = one(s_ref[h, 0] >= v[h][TOPK - 1])
        for j in range(1, NKEYS):
            n_ge = n_ge + one(s_ref[h, j] >= v[h][TOPK - 1])
        tie = jnp.maximum(tie, one(n_ge > float(TOPK)))
        for r in range(TOPK - 1):
            tie = jnp.maximum(tie, one(v[h][r] == v[h][r + 1]))
    has_tie = jnp.max(tie) > 0.0

    cand = {p: v1[p[0]] + v2[p[1]] for p in _CAND_PAIRS}
    rest = [cand[p] for p in _CAND_PAIRS if p[0] > 0]
    rest += [None] * (-len(rest) % TOPK)
    groups = [[cand[(0, r2)] for r2 in range(TOPK)]] + [rest[k:k + TOPK] for k in range(0, len(rest), TOPK)]
    while len(groups) & (len(groups) - 1):
        groups.append([None] * TOPK)
    w = _top16_values(groups)
    tau = w[TOPK - 1]
    n_gt = sum(one(cand[p] > tau) for p in _CAND_PAIRS)
    need = float(TOPK) - n_gt
    eq_seen = jnp.zeros((8, LANES), F32)
    counts = [jnp.zeros((8, LANES), F32) for _ in range(TOPK)]
    for p in _CAND_PAIRS:
        eq = cand[p] == tau
        counts[p[0]] = counts[p[0]] + one((cand[p] > tau) | (eq & (eq_seen < need)))
        eq_seen = eq_seen + one(eq)
    z = jnp.ones((8, LANES), F32)
    for r in range(1, TOPK):
        z = z + jnp.exp(w[r] - w[0])
    inv_z = 0.5 / z

    def tables(with_ties):
        for kg in range(NKEYS // 8):
            tabs = [[], [], [], []]
            for j in range(kg * 8, (kg + 1) * 8):
                s1 = s_ref[0, j]
                s2 = s_ref[1, j]
                if with_ties:
                    rank1 = _count_gt(v1, s1) + t_scr[0, j]
                    ckey = jnp.zeros((8, LANES), F32)
                    for r in range(TOPK):
                        ckey = jnp.where(rank1 == float(r), counts[r], ckey)
                    rank2 = _count_gt(v2, s2) + t_scr[1, j]
                else:
                    ckey = jnp.where(v1[0] > s1, counts[1], counts[0])
                    for r in range(1, TOPK - 1):
                        ckey = jnp.where(v1[r] > s1, counts[r + 1], ckey)
                    ckey = jnp.where(v1[TOPK - 1] > s1, 0.0, ckey)
                    rank2 = _count_gt(v2, s2)
                tabs[0].append(rank2)
                tabs[1].append(jnp.exp(s2 - v2[0]))
                tabs[2].append(ckey)
                tabs[3].append(jnp.exp(s1 - v1[0]) * inv_z)
            for ref, tab in zip((r2_ref, p2_ref, c_ref, p1_ref), tabs):
                for r, tile in enumerate(_xpose8(tab)):
                    ref[kg * 8:(kg + 1) * 8, lane_tile(r)] = tile

    @pl.when(has_tie)
    def _():
        t_scr[...] = jnp.zeros_like(t_scr)

        def lower_equal(jp, carry):
            for h in range(2):
                sv = s_ref[h, jp]
                for j in range(1, NKEYS):
                    inc = jnp.where(jp < j, 1.0, 0.0)
                    t_scr[h, j] = t_scr[h, j] + jnp.where(s_ref[h, j] == sv, inc, 0.0)
            return carry
        lax.fori_loop(0, NKEYS - 1, lower_equal, 0)
        tables(True)

    @pl.when(jnp.logical_not(has_tie))
    def _():
        tables(False)


def _select(scores):
    T = scores.shape[-1]
    tt = 8 * LANES
    tab = pl.BlockSpec((None, NKEYS, tt), lambda i, hd: (hd, 0, i))
    return pl.pallas_call(
        _select_kernel,
        grid=(T // tt, PEER_HEADS),
        in_specs=[pl.BlockSpec((2, NKEYS, tt), lambda i, hd: (hd, 0, i))],
        out_specs=[tab] * 4,
        out_shape=[jax.ShapeDtypeStruct((PEER_HEADS, NKEYS, T), F32)] * 4,
        scratch_shapes=[pltpu.VMEM((2, NKEYS, 8, LANES), F32), pltpu.VMEM((2, NKEYS, 8, LANES), F32)],
        compiler_params=_cparams(("arbitrary", "arbitrary")),
        name="select",
    )(scores)


_SQRT_HALF = 0.7071067811865476
_JROWS = 16
_PEER_NB = 8 * NKEYS
_PEER_STEPS = N_EXPERTS // _PEER_NB
_GATE_CHAINS = 1


def _zero_after(x):
    u = pltpu.bitcast(x, jnp.uint32)
    u = lax.shift_right_logical(lax.shift_right_logical(u, jnp.uint32(16)), jnp.uint32(16))
    return pltpu.bitcast(u, BF16)


def _peer_kernel(h2t_ref, u_ref, vt_ref, r2_ref, p2_ref, c_ref, p1_ref, x1_ref, mod_ref, gf_ref, yc_ref, yl_ref,
                 a_scr, w_scr, acc_scr, tab_scr, *, tb, n_ctx_tok, lat_len):
    L = pl.program_id(0)
    n_lg = tb // LANES
    vpu_blk = L - 1
    acc_blk = L - 2

    @pl.when(L == 0)
    def _():
        a_scr[...] = jnp.zeros_like(a_scr)
        w_scr[...] = jnp.zeros_like(w_scr)
        acc_scr[...] = jnp.zeros_like(acc_scr)

    @pl.when((L == 0) | (vpu_blk % _PEER_STEPS == 0))
    def _():
        for jg in range(NKEYS // _JROWS):
            rows = slice(jg * _JROWS, (jg + 1) * _JROWS)
            for lg in range(n_lg):
                lanes = slice(lg * LANES, (lg + 1) * LANES)
                for hd in range(PEER_HEADS):
                    tab_scr[jg, lg, hd * _JROWS:(hd + 1) * _JROWS, :] = r2_ref[hd, rows, lanes].astype(BF16)
                    tab_scr[jg, lg, (PEER_HEADS + hd) * _JROWS:(PEER_HEADS + hd + 1) * _JROWS, :] = \
                        p2_ref[hd, rows, lanes].astype(BF16)

    @pl.when((acc_blk >= 0) & (acc_blk % _PEER_STEPS == 0))
    def _():
        acc_scr[...] = jnp.zeros_like(acc_scr)

    def stages(cur):
        prv = 1 - cur
        a_scr[cur] = jnp.dot(u_ref[...], h2t_ref[...], preferred_element_type=F32)
        acc_scr[...] += jnp.dot(vt_ref[...], w_scr[cur], preferred_element_type=F32)
        link = [None] * _GATE_CHAINS
        for lg in range(n_lg):
            lanes = slice(lg * LANES, (lg + 1) * LANES)
            chain = lg * _GATE_CHAINS // n_lg
            for i8 in range(_PEER_NB // NKEYS):
                bcast = lambda ref, hd: jnp.broadcast_to(ref[hd, i8:i8 + 1, lanes], (_JROWS, LANES)).astype(BF16)
                crow = [bcast(c_ref, hd) for hd in range(PEER_HEADS)]
                p1row = [bcast(p1_ref, hd) for hd in range(PEER_HEADS)]
                for jg in range(NKEYS // _JROWS):
                    rows = slice(jg * _JROWS, (jg + 1) * _JROWS)
                    arow = slice(i8 * NKEYS + jg * _JROWS, i8 * NKEYS + (jg + 1) * _JROWS)
                    a = a_scr[prv, arow, lanes]
                    terms = []
                    for hd in range(PEER_HEADS):
                        r2 = tab_scr[jg, lg, hd * _JROWS:(hd + 1) * _JROWS, :]
                        p2 = tab_scr[jg, lg, (PEER_HEADS + hd) * _JROWS:(PEER_HEADS + hd + 1) * _JROWS, :]
                        terms.append(jnp.where(r2 < crow[hd], p2, jnp.zeros_like(p2)) * p1row[hd])
                    if link[chain] is not None:
                        terms[0] = terms[0] + _zero_after(link[chain])
                    while len(terms) > 1:
                        terms = [terms[k] + terms[k + 1] for k in range(0, len(terms), 2)]
                    g = terms[0]
                    gelu = a * (1.0 + lax.erf(a * _SQRT_HALF))
                    w = g * gelu.astype(BF16)
                    w_scr[prv, arow, lanes] = w
                    link[chain] = w

    for slot in range(2):
        pl.when(L % 2 == slot)(functools.partial(stages, slot))

    @pl.when((acc_blk >= 0) & (acc_blk % _PEER_STEPS == _PEER_STEPS - 1))
    def _():
        is_ctx, row = _mod_row(acc_blk // _PEER_STEPS, tb, n_ctx_tok, lat_len)
        ga2 = mod_ref[pl.ds(row, 1), 5 * D_MODEL:6 * D_MODEL]
        x2 = x1_ref[...] + ga2 * acc_scr[...].T
        y = _rms(x2) * gf_ref[...]

        @pl.when(is_ctx)
        def _():
            yc_ref[...] = y

        @pl.when(jnp.logical_not(is_ctx))
        def _():
            yl_ref[...] = y


def _peer(h2t, u_bf, vt_bf, r2, p2, cc, p1, x1, mod, g_final, *, tb, n_ctx_tok, lat_len):
    T = h2t.shape[1]
    nb = _PEER_NB
    n_tiles = T // tb
    nct = n_ctx_tok // tb
    n_blocks = n_tiles * _PEER_STEPS
    blk = lambda L, lag: jnp.clip(L - lag, 0, n_blocks - 1)
    tile = lambda L, lag: blk(L, lag) // _PEER_STEPS
    step = lambda L, lag: blk(L, lag) % _PEER_STEPS
    tab = pl.BlockSpec((PEER_HEADS, NKEYS, tb), lambda L: (0, 0, tile(L, 1)))
    rowtab = pl.BlockSpec((PEER_HEADS, nb // NKEYS, tb), lambda L: (0, step(L, 1), tile(L, 1)))
    return pl.pallas_call(
        functools.partial(_peer_kernel, tb=tb, n_ctx_tok=n_ctx_tok, lat_len=lat_len),
        grid=(n_blocks + 2,),
        in_specs=[pl.BlockSpec((D_MODEL, tb), lambda L: (0, tile(L, 0))),
                  pl.BlockSpec((nb, D_MODEL), lambda L: (step(L, 0), 0)),
                  pl.BlockSpec((D_MODEL, nb), lambda L: (0, step(L, 2))),
                  tab, tab, rowtab, rowtab,
                  pl.BlockSpec((tb, D_MODEL), lambda L: (tile(L, 2), 0)),
                  pl.BlockSpec(mod.shape, lambda L: (0, 0)),
                  pl.BlockSpec((1, D_MODEL), lambda L: (0, 0))],
        out_specs=[pl.BlockSpec((tb, D_MODEL), lambda L: (jnp.minimum(tile(L, 2), nct - 1), 0)),
                   pl.BlockSpec((tb, D_MODEL), lambda L: (jnp.maximum(tile(L, 2) - nct, 0), 0))],
        out_shape=[jax.ShapeDtypeStruct((n_ctx_tok, D_MODEL), F32),
                   jax.ShapeDtypeStruct((T - n_ctx_tok, D_MODEL), F32)],
        scratch_shapes=[pltpu.VMEM((2, nb, tb), F32), pltpu.VMEM((2, nb, tb), BF16), pltpu.VMEM((D_MODEL, tb), F32),
                        pltpu.VMEM((NKEYS // _JROWS, tb // LANES, 2 * PEER_HEADS * _JROWS, LANES), BF16)],
        compiler_params=_cparams(("arbitrary",)),
        name="peer",
    )(h2t, u_bf, vt_bf, r2, p2, cc, p1, x1, mod, g_final.reshape(1, -1))


def _layer(x_prompt, x_sample, state_C, state_n, state_m, c, c_ctx, w_ada, b_ada, g_norm1, w_in, b_igate, b_fgate,
           conv_w, conv_b, g_mlstm, w_out, g_norm2, peer_wq, peer_k1, peer_k2, peer_u, peer_v, g_final,
           *, tm, tc_lat, tb):
    B, S, _ = x_prompt.shape
    DB, DS, _ = x_sample.shape
    n_ctx_tok = B * S
    ng = N_DIRS * N_HEADS
    x_ctx = x_prompt.reshape(n_ctx_tok, D_MODEL)
    x_lat = x_sample.reshape(DB * DS, D_MODEL)
    cv8 = jnp.zeros((8, D_MODEL), F32).at[0].set(c_ctx).at[1:1 + DB].set(c)
    mod = _modulation(cv8, w_ada, b_ada)

    conv, q, k, v, og, gates, gates_t = _in_proj(
        x_ctx, x_lat, mod, g_norm1, w_in, b_igate, b_fgate, conv_w, conv_b,
        tm=tm, n_ctx_tok=n_ctx_tok, ctx_len=S, lat_len=DS)

    def ext_state(C, n_, m_):
        nb_ = jnp.broadcast_to(n_[..., None], n_.shape + (DH,))
        cext = jnp.concatenate([C, nb_], axis=-1).reshape(-1, ng, DH, 2 * DH)
        mext = jnp.broadcast_to(m_.reshape(-1, ng, 1), (C.shape[0], ng, LANES))
        return cext, mext

    z_c, z_m = ext_state(jnp.zeros((B, N_DIRS, N_HEADS, DH, DH), F32), jnp.zeros((B, N_DIRS, N_HEADS, DH), F32),
                         jnp.zeros((B, N_DIRS, N_HEADS), F32))
    hf_c, hb_c, c_fin, m_fin = _mlstm(q, k, v, gates, gates_t, z_c, z_m, tok0=0, nseq=B, seq_len=S, tc=S,
                                      write_state=True)
    l_c, l_m = ext_state(state_C, state_n, state_m)
    hf_l, hb_l = _mlstm(q, k, v, gates, gates_t, l_c, l_m, tok0=n_ctx_tok, nseq=DB, seq_len=DS, tc=tc_lat,
                        write_state=False)

    x1, h2t, scores = _mix(x_ctx, x_lat, conv, hf_c, hf_l, hb_c, hb_l, og, mod, g_mlstm, w_out, g_norm2,
                           peer_wq, peer_k1, peer_k2, tm=tm, n_ctx_tok=n_ctx_tok, lat_len=DS)
    r2, p2, cc, p1 = _select(scores)
    y_ctx, y_lat = _peer(h2t, peer_u.astype(BF16), peer_v.astype(BF16).T, r2, p2, cc, p1, x1, mod, g_final,
                         tb=tb, n_ctx_tok=n_ctx_tok, lat_len=DS)

    y_prompt = y_ctx.reshape(B, S, D_MODEL)
    y_sample = y_lat.reshape(DB, DS, D_MODEL)
    new_C = c_fin[..., :DH].reshape(B, 1, N_DIRS, N_HEADS, DH, DH)
    new_n = c_fin[..., DH].reshape(B, 1, N_DIRS, N_HEADS, DH)
    new_m = m_fin[..., 0].reshape(B, 1, N_DIRS, N_HEADS)
    return y_prompt, y_sample, new_C, new_n, new_m


def kernel(x_prompt, x_sample, state_C, state_n, state_m, c, c_ctx, w_ada, b_ada, g_norm1, w_in, b_igate, b_fgate,
           conv_w, conv_b, g_mlstm, w_out, g_norm2, peer_wq, peer_k1, peer_k2, peer_u, peer_v, g_final):
    return _layer(x_prompt, x_sample, state_C[:, 0], state_n[:, 0], state_m[:, 0], c, c_ctx, w_ada[0], b_ada[0],
                  g_norm1[0], w_in[0], b_igate[0], b_fgate[0], conv_w[0], conv_b[0], g_mlstm[0], w_out[0],
                  g_norm2[0], peer_wq[0], peer_k1[0], peer_k2[0], peer_u[0], peer_v[0], g_final,
                  tm=512, tc_lat=256, tb=512)
```

```python
import functools

import numpy as np
import jax
import jax.numpy as jnp
from jax import lax
from jax.experimental import pallas as pl
from jax.experimental.pallas import tpu as pltpu

F32 = jnp.float32
BF16 = jnp.bfloat16

D_MODEL = 1024
CONV_W = 512
N_HEADS = 4
DH = 128
MLSTM_W = N_HEADS * DH
N_DIRS = 2
GRID_W = 64
N_GROUPS = 7
GATE_COL0 = N_GROUPS * 512
N_GATES = 2 * N_DIRS * N_HEADS
PEER_HEADS = 8
NKEYS = 128
TOPK = 16
N_EXPERTS = NKEYS * NKEYS
EPS = 1e-6
LANES = 128
NEG_INF = float("-inf")

VMEM_LIMIT = 56 * 1024 * 1024


def _cparams(sem):
    return pltpu.CompilerParams(dimension_semantics=sem, vmem_limit_bytes=VMEM_LIMIT)


def _mod_kernel(cv_ref, w_ref, b_ref, o_ref):
    cv = cv_ref[...]
    s = cv * (1.0 / (1.0 + jnp.exp(-cv)))
    o_ref[...] = jnp.dot(s, w_ref[...], preferred_element_type=F32) + b_ref[...]


def _modulation(cv8, w_ada, b_ada):
    n = w_ada.shape[1]
    tn = 1024
    return pl.pallas_call(
        _mod_kernel,
        grid=(n // tn,),
        in_specs=[pl.BlockSpec((8, D_MODEL), lambda j: (0, 0)),
                  pl.BlockSpec((D_MODEL, tn), lambda j: (0, j)),
                  pl.BlockSpec((1, tn), lambda j: (0, j))],
        out_specs=pl.BlockSpec((8, tn), lambda j: (0, j)),
        out_shape=jax.ShapeDtypeStruct((8, n), F32),
        compiler_params=_cparams(("arbitrary",)),
        name="modulation",
    )(cv8, w_ada, b_ada.reshape(1, n))


def _mod_row(i, tm, n_ctx_tok, lat_len):
    n_ctx_tiles = n_ctx_tok // tm
    tiles_per_seq = lat_len // tm
    is_ctx = i < n_ctx_tiles
    row = jnp.where(is_ctx, 0, 1 + (i - n_ctx_tiles) // tiles_per_seq)
    return is_ctx, row


def _group_specs(tm, width, n_ctx_tok):
    nct = n_ctx_tok // tm
    return [pl.BlockSpec((tm, width), lambda i: (jnp.minimum(i, nct - 1), 0)),
            pl.BlockSpec((tm, width), lambda i: (jnp.maximum(i - nct, 0), 0))]


def _rms(x):
    return x * lax.rsqrt(jnp.mean(x * x, axis=-1, keepdims=True) + EPS)


def _log_sigmoid(z):
    return jnp.minimum(z, 0.0) - jnp.log(1.0 + jnp.exp(-jnp.abs(z)))


def _inproj_kernel(xc_ref, xl_ref, mod_ref, g1_ref, win_ref, wg_ref, wgt_ref, bg_ref, bgt_ref, cw_ref, cb_ref,
                   conv_ref, q_ref, k_ref, v_ref, o_ref, gates_ref, gatest_ref,
                   *, tm, n_ctx_tok, ctx_len, lat_len):
    i = pl.program_id(0)
    is_ctx, row = _mod_row(i, tm, n_ctx_tok, lat_len)
    sh1 = mod_ref[pl.ds(row, 1), 0:D_MODEL]
    sc1 = mod_ref[pl.ds(row, 1), D_MODEL:2 * D_MODEL]
    x = jnp.where(is_ctx, xc_ref[...], xl_ref[...])
    h = _rms(x) * g1_ref[...] * (1.0 + sc1) + sh1
    hb = h.astype(BF16)

    def grp(g):
        return jnp.dot(hb, win_ref[:, g * 512:(g + 1) * 512], preferred_element_type=F32)

    u = grp(1) * grp(2)
    pm = jnp.where(is_ctx, ctx_len - 1, GRID_W - 1)
    pos = lax.broadcasted_iota(jnp.int32, (tm, 1), 0) & pm
    u_prev = jnp.where(pos == 0, 0.0, pltpu.roll(u, 1, axis=0))
    u_next = jnp.where(pos == pm, 0.0, pltpu.roll(u, tm - 1, axis=0))
    y = u_prev * cw_ref[0:1, :] + u * cw_ref[1:2, :] + u_next * cw_ref[2:3, :] + cb_ref[...]
    conv_ref[...] = (grp(0) * y).astype(BF16)

    q_ref[...] = (grp(3) * (DH ** -0.5)).astype(BF16)
    k_ref[...] = grp(4).astype(BF16)
    v_ref[...] = grp(5).astype(BF16)
    o_ref[...] = grp(6)

    z = jnp.dot(hb, wg_ref[...], preferred_element_type=F32) + bg_ref[...]
    lane = lax.broadcasted_iota(jnp.int32, z.shape, 1)
    gates_ref[...] = jnp.where(lane >= N_DIRS * N_HEADS, _log_sigmoid(z), z)
    zt = lax.dot_general(wgt_ref[...], hb, (((1,), (1,)), ((), ())), preferred_element_type=F32) + bgt_ref[...]
    sub = lax.broadcasted_iota(jnp.int32, zt.shape, 0)
    gatest_ref[...] = jnp.where(sub >= N_DIRS * N_HEADS, _log_sigmoid(zt), zt)


def _in_proj(x_ctx, x_lat, mod, g1, w_in, b_igate, b_fgate, conv_w, conv_b, *, tm, n_ctx_tok, ctx_len, lat_len):
    T = x_ctx.shape[0] + x_lat.shape[0]
    win = w_in[:, :GATE_COL0].astype(BF16)
    wg = jnp.zeros((D_MODEL, LANES), F32).at[:, :N_GATES].set(w_in[:, GATE_COL0:]).astype(BF16)
    bgate = jnp.zeros((1, LANES), F32).at[0, :N_GATES].set(
        jnp.concatenate([b_igate.reshape(-1), b_fgate.reshape(-1)]))
    tok = lambda w: pl.BlockSpec((tm, w), lambda i: (i, 0))
    full = lambda a: pl.BlockSpec(a.shape, lambda i: (0,) * a.ndim)
    args = (x_ctx, x_lat, mod, g1.reshape(1, -1), win, wg, wg.T, bgate, bgate.T, conv_w, conv_b.reshape(1, -1))
    return pl.pallas_call(
        functools.partial(_inproj_kernel, tm=tm, n_ctx_tok=n_ctx_tok, ctx_len=ctx_len, lat_len=lat_len),
        grid=(T // tm,),
        in_specs=_group_specs(tm, D_MODEL, n_ctx_tok) + [full(a) for a in args[2:]],
        out_specs=[tok(CONV_W), tok(MLSTM_W), tok(MLSTM_W), tok(MLSTM_W), tok(MLSTM_W), tok(LANES),
                   pl.BlockSpec((LANES, tm), lambda i: (0, i))],
        out_shape=[jax.ShapeDtypeStruct((T, CONV_W), BF16),
                   jax.ShapeDtypeStruct((T, MLSTM_W), BF16),
                   jax.ShapeDtypeStruct((T, MLSTM_W), BF16),
                   jax.ShapeDtypeStruct((T, MLSTM_W), BF16),
                   jax.ShapeDtypeStruct((T, MLSTM_W), F32),
                   jax.ShapeDtypeStruct((T, LANES), F32),
                   jax.ShapeDtypeStruct((LANES, T), F32)],
        compiler_params=_cparams(("arbitrary",)),
        name="in_proj",
    )(*args)


def _split_bf16(x):
    hi = x.astype(BF16)
    return hi, (x - hi.astype(F32)).astype(BF16)


def _mlstm_kernel(qf_ref, kf_ref, vf_ref, gf_ref, gtf_ref, qb_ref, kb_ref, vb_ref, gb_ref, gtb_ref,
                  c0_ref, m0_ref, hf_ref, hb_ref, *rest, tc, write_state):
    if write_state:
        cout_ref, mout_ref, c_scr, m_scr = rest
    else:
        c_scr, m_scr = rest
    c = pl.program_id(1)

    @pl.when(c == 0)
    def _():
        c_scr[...] = c0_ref[...]
        m_scr[...] = m0_ref[...]

    r_io = lax.broadcasted_iota(jnp.int32, (tc, tc), 0)
    c_io = lax.broadcasted_iota(jnp.int32, (tc, tc), 1)
    ones = jnp.ones((tc, DH), BF16)
    wide = lambda x, width: jnp.concatenate([x] * (width // DH), axis=1)

    for d in range(N_DIRS):
        q_ref, k_ref, v_ref, g_ref, gt_ref, h_ref = (
            (qf_ref, kf_ref, vf_ref, gf_ref, gtf_ref, hf_ref) if d == 0 else
            (qb_ref, kb_ref, vb_ref, gb_ref, gtb_ref, hb_ref))
        causal = (c_io <= r_io) if d == 0 else (c_io >= r_io)
        last = tc - 1 if d == 0 else 0
        tri = causal.astype(BF16)
        tri_t = ((r_io <= c_io) if d == 0 else (r_io >= c_io)).astype(BF16)
        g = g_ref[...]
        gt = gt_ref[...]
        g_hi, g_lo = _split_bf16(g)
        gt_hi, gt_lo = _split_bf16(gt)
        bcum = (jnp.dot(tri, g_hi, preferred_element_type=F32) + jnp.dot(tri, g_lo, preferred_element_type=F32))
        bcum_t = (jnp.dot(gt_hi, tri_t, preferred_element_type=F32)
                  + jnp.dot(gt_lo, tri_t, preferred_element_type=F32))
        for hd in range(N_HEADS):
            gi = d * N_HEADS + hd
            gfi = N_DIRS * N_HEADS + gi
            sl = slice(hd * DH, (hd + 1) * DH)
            q = q_ref[:, sl]
            k = k_ref[:, sl]
            v = v_ref[:, sl]
            ig_c = jnp.broadcast_to(g[:, gi:gi + 1], (tc, DH))
            b_c = jnp.broadcast_to(bcum[:, gfi:gfi + 1], (tc, DH))
            ig_r = gt[gi:gi + 1, :]
            b_r = bcum_t[gfi:gfi + 1, :]
            m_prev = m_scr[gi:gi + 1, :]
            dmat = jnp.where(causal, wide(b_c, tc) - b_r + ig_r, NEG_INF)
            inter = b_c + m_prev
            m_t = jnp.maximum(inter, jnp.max(dmat, axis=-1, keepdims=True))
            s = (lax.dot_general(q, k, (((1,), (1,)), ((), ())), preferred_element_type=F32)
                 * jnp.exp(dmat - wide(m_t, tc)))
            a = jnp.exp(inter - m_t)
            cext = c_scr[gi]
            vext = jnp.concatenate([v, ones], axis=1)
            nd = (wide(a, 2 * DH) * jnp.dot(q, cext.astype(BF16), preferred_element_type=F32)
                  + jnp.dot(s.astype(BF16), vext, preferred_element_type=F32))
            num = nd[:, :DH]
            den = nd[:, DH:]
            h_ref[:, sl] = num / jnp.maximum(jnp.abs(den), jnp.exp(-m_t))
            m_new = m_t[last:last + 1, :]
            b_last = b_c[last:last + 1, :]
            w_c = jnp.exp(b_last - b_c + ig_c - m_new)
            decay = jnp.exp(b_last + m_prev - m_new)
            kw_t = (k.astype(F32) * w_c).T.astype(BF16)
            c_scr[gi] = wide(decay, 2 * DH) * cext + jnp.dot(kw_t, vext, preferred_element_type=F32)
            m_scr[gi:gi + 1, :] = m_new

    if write_state:
        @pl.when(c == pl.num_programs(1) - 1)
        def _():
            cout_ref[...] = c_scr[...]
            mout_ref[...] = m_scr[...]


def _mlstm(q, k, v, gates, gates_t, c0ext, m0, *, tok0, nseq, seq_len, tc, write_state):
    nc = seq_len // tc
    blk0 = tok0 // tc
    fwd = lambda b, c: b * nc + c
    bwd = lambda b, c: b * nc + nc - 1 - c
    tokspec = lambda w, f, o: pl.BlockSpec((tc, w), lambda b, c: (o + f(b, c), 0))
    gtspec = lambda f: pl.BlockSpec((LANES, tc), lambda b, c: (0, blk0 + f(b, c)))
    ng = N_DIRS * N_HEADS
    in_specs = [tokspec(MLSTM_W, fwd, blk0)] * 3 + [tokspec(LANES, fwd, blk0), gtspec(fwd)] \
        + [tokspec(MLSTM_W, bwd, blk0)] * 3 + [tokspec(LANES, bwd, blk0), gtspec(bwd)] \
        + [pl.BlockSpec((None, ng, DH, 2 * DH), lambda b, c: (b, 0, 0, 0)),
           pl.BlockSpec((None, ng, LANES), lambda b, c: (b, 0, 0))]
    out_specs = [tokspec(MLSTM_W, fwd, 0), tokspec(MLSTM_W, bwd, 0)]
    out_shape = [jax.ShapeDtypeStruct((nseq * seq_len, MLSTM_W), F32)] * 2
    if write_state:
        out_specs += [pl.BlockSpec((None, ng, DH, 2 * DH), lambda b, c: (b, 0, 0, 0)),
                      pl.BlockSpec((None, ng, LANES), lambda b, c: (b, 0, 0))]
        out_shape += [jax.ShapeDtypeStruct((nseq, ng, DH, 2 * DH), F32),
                      jax.ShapeDtypeStruct((nseq, ng, LANES), F32)]
    return pl.pallas_call(
        functools.partial(_mlstm_kernel, tc=tc, write_state=write_state),
        grid=(nseq, nc),
        in_specs=in_specs,
        out_specs=out_specs,
        out_shape=out_shape,
        scratch_shapes=[pltpu.VMEM((ng, DH, 2 * DH), F32), pltpu.VMEM((ng, LANES), F32)],
        compiler_params=_cparams(("arbitrary", "arbitrary")),
        name="mlstm_ctx" if write_state else "mlstm_lat",
    )(q, k, v, gates, gates_t, q, k, v, gates, gates_t, c0ext, m0)


def _mix_kernel(xc_ref, xl_ref, conv_ref, hfc_ref, hfl_ref, hbc_ref, hbl_ref, og_ref, mod_ref, gm_ref, wout_ref,
                g2_ref, wst_ref, x1_ref, h2t_ref, sc_ref, *, tm, n_ctx_tok, lat_len):
    i = pl.program_id(0)
    is_ctx, row = _mod_row(i, tm, n_ctx_tok, lat_len)
    mrow = lambda j: mod_ref[pl.ds(row, 1), j * D_MODEL:(j + 1) * D_MODEL]
    hs = jnp.where(is_ctx, hfc_ref[...] + hbc_ref[...], hfl_ref[...] + hbl_ref[...])
    hn = jnp.concatenate([_rms(hs[:, hd * DH:(hd + 1) * DH]) for hd in range(N_HEADS)], axis=1)
    og = og_ref[...]
    ml = (hn * gm_ref[...] * (1.0 / (1.0 + jnp.exp(-og)))).astype(BF16)
    mix = (jnp.dot(conv_ref[...], wout_ref[0:CONV_W, :], preferred_element_type=F32)
           + jnp.dot(ml, wout_ref[CONV_W:, :], preferred_element_type=F32))
    x1 = jnp.where(is_ctx, xc_ref[...], xl_ref[...]) + mrow(2) * mix
    x1_ref[...] = x1
    h2f = _rms(x1) * g2_ref[...] * (1.0 + mrow(4)) + mrow(3)
    h2t = h2f.T.astype(BF16)
    h2t_ref[...] = h2t
    sc = jnp.dot(wst_ref[...], h2t, preferred_element_type=F32)
    sc_ref[...] = sc.reshape(2 * PEER_HEADS, NKEYS, tm)


def _fold_keys_kernel(k_ref, wq_ref, o_ref):
    o_ref[...] = lax.dot_general(k_ref[...], wq_ref[...], (((1,), (1,)), ((), ())),
                                 preferred_element_type=F32).astype(BF16)


def _fold_keys(wq, k1, k2):
    n = 2 * PEER_HEADS
    keys = jnp.stack([k1, k2])
    return pl.pallas_call(
        _fold_keys_kernel,
        grid=(n,),
        in_specs=[pl.BlockSpec((None, NKEYS, NKEYS), lambda j: (j % 2, 0, 0)),
                  pl.BlockSpec((D_MODEL, NKEYS), lambda j: (0, j))],
        out_specs=pl.BlockSpec((NKEYS, D_MODEL), lambda j: (j, 0)),
        out_shape=jax.ShapeDtypeStruct((n * NKEYS, D_MODEL), BF16),
        compiler_params=_cparams(("arbitrary",)),
        name="fold_keys",
    )(keys, wq)


def _mix(x_ctx, x_lat, conv, hf_ctx, hf_lat, hb_ctx, hb_lat, og, mod, g_mlstm, w_out, g2, wq, k1, k2,
         *, tm, n_ctx_tok, lat_len):
    T = x_ctx.shape[0] + x_lat.shape[0]
    tok = lambda w: pl.BlockSpec((tm, w), lambda i: (i, 0))
    full = lambda a: pl.BlockSpec(a.shape, lambda i: (0,) * a.ndim)
    consts = (mod, g_mlstm.reshape(1, -1), w_out.astype(BF16), g2.reshape(1, -1), _fold_keys(wq, k1, k2))
    return pl.pallas_call(
        functools.partial(_mix_kernel, tm=tm, n_ctx_tok=n_ctx_tok, lat_len=lat_len),
        grid=(T // tm,),
        in_specs=_group_specs(tm, D_MODEL, n_ctx_tok) + [tok(CONV_W)] + _group_specs(tm, MLSTM_W, n_ctx_tok) * 2
        + [tok(MLSTM_W)] + [full(a) for a in consts],
        out_specs=[tok(D_MODEL), pl.BlockSpec((D_MODEL, tm), lambda i: (0, i)),
                   pl.BlockSpec((2 * PEER_HEADS, NKEYS, tm), lambda i: (0, 0, i))],
        out_shape=[jax.ShapeDtypeStruct((T, D_MODEL), F32),
                   jax.ShapeDtypeStruct((D_MODEL, T), BF16),
                   jax.ShapeDtypeStruct((2 * PEER_HEADS, NKEYS, T), F32)],
        compiler_params=_cparams(("arbitrary",)),
        name="mix",
    )(x_ctx, x_lat, conv, hf_ctx, hf_lat, hb_ctx, hb_lat, og, *consts)


def _sort16_pairs():
    def merge(lo, hi, r):
        step = r * 2
        if step < hi - lo:
            yield from merge(lo, hi, step)
            yield from merge(lo + r, hi, step)
            yield from ((i, i + r) for i in range(lo + r, hi - r, step))
        else:
            yield (lo, lo + r)

    def sort(lo, hi):
        if hi - lo >= 1:
            mid = lo + (hi - lo) // 2
            yield from sort(lo, mid)
            yield from sort(mid + 1, hi)
            yield from merge(lo, hi, 1)

    return tuple(sort(0, TOPK - 1))


_SORT16 = _sort16_pairs()
_CAND_PAIRS = tuple((r1, r2) for r1 in range(TOPK) for r2 in range(TOPK) if (r1 + 1) * (r2 + 1) <= TOPK)


def _cmpx(x, i, j):
    a, b = x[i], x[j]
    if b is None:
        return
    if a is None:
        x[i], x[j] = b, None
        return
    x[i], x[j] = jnp.maximum(a, b), jnp.minimum(a, b)


def _sort16(x):
    x = list(x)
    for i, j in _SORT16:
        _cmpx(x, i, j)
    return x


def _merge_top16(a, b):
    x = []
    for k in range(TOPK):
        p, q = a[k], b[TOPK - 1 - k]
        x.append(q if p is None else p if q is None else jnp.maximum(p, q))
    d = TOPK // 2
    while d:
        for k in range(TOPK):
            if not k & d:
                _cmpx(x, k, k + d)
        d //= 2
    return x


def _top16_values(groups):
    groups = [_sort16(g) for g in groups]
    while len(groups) > 1:
        groups = [_merge_top16(groups[k], groups[k + 1]) for k in range(0, len(groups), 2)]
    return groups[0]


def _count_gt(vs, s):
    g = jnp.where(vs[0] > s, 1.0, 0.0)
    for r in range(1, len(vs)):
        g = jnp.where(vs[r] > s, float(r + 1), g)
    return g


def _xpose8(x):
    x = list(x)
    sub = lax.broadcasted_iota(jnp.int32, (8, LANES), 0)
    for d in (4, 2, 1):
        keep = (sub & d) == 0
        for a in range(8):
            if not a & d:
                xa, xb = x[a], x[a + d]
                x[a] = jnp.where(keep, xa, pltpu.roll(xb, d, axis=0))
                x[a + d] = jnp.where(keep, pltpu.roll(xa, 8 - d, axis=0), xb)
    return x


def _select_kernel(sc_ref, r2_ref, p2_ref, c_ref, p1_ref, s_ref, t_scr):
    one = lambda m: jnp.where(m, 1.0, 0.0)
    lane_tile = lambda r: slice(r * LANES, (r + 1) * LANES)
    for h in range(2):
        for kg in range(NKEYS // 8):
            tiles = _xpose8([sc_ref[h, kg * 8:(kg + 1) * 8, lane_tile(r)] for r in range(8)])
            for k in range(8):
                s_ref[h, kg * 8 + k] = tiles[k]
    v = [_top16_values([[s_ref[h, g * TOPK + k] for k in range(TOPK)] for g in range(NKEYS // TOPK)])
         for h in range(2)]
    v1, v2 = v

    tie = jnp.zeros((8, LANES), F32)
    for h in range(2):
        n_ge = one(s_ref[h, 0] >= v[h][TOPK - 1])
        for j in range(1, NKEYS):
            n_ge = n_ge + one(s_ref[h, j] >= v[h][TOPK - 1])
        tie = jnp.maximum(tie, one(n_ge > float(TOPK)))
        for r in range(TOPK - 1):
            tie = jnp.maximum(tie, one(v[h][r] == v[h][r + 1]))
    has_tie = jnp.max(tie) > 0.0

    cand = {p: v1[p[0]] + v2[p[1]] for p in _CAND_PAIRS}
    rest = [cand[p] for p in _CAND_PAIRS if p[0] > 0]
    rest += [None] * (-len(rest) % TOPK)
    groups = [[cand[(0, r2)] for r2 in range(TOPK)]] + [rest[k:k + TOPK] for k in range(0, len(rest), TOPK)]
    while len(groups) & (len(groups) - 1):
        groups.append([None] * TOPK)
    w = _top16_values(groups)
    tau = w[TOPK - 1]
    n_gt = sum(one(cand[p] > tau) for p in _CAND_PAIRS)
    need = float(TOPK) - n_gt
    eq_seen = jnp.zeros((8, LANES), F32)
    counts = [jnp.zeros((8, LANES), F32) for _ in range(TOPK)]
    for p in _CAND_PAIRS:
        eq = cand[p] == tau
        counts[p[0]] = counts[p[0]] + one((cand[p] > tau) | (eq & (eq_seen < need)))
        eq_seen = eq_seen + one(eq)
    z = jnp.ones((8, LANES), F32)
    for r in range(1, TOPK):
        z = z + jnp.exp(w[r] - w[0])
    inv_z = 0.5 / z

    def tables(with_ties):
        for kg in range(NKEYS // 8):
            tabs = [[], [], [], []]
            for j in range(kg * 8, (kg + 1) * 8):
                s1 = s_ref[0, j]
                s2 = s_ref[1, j]
                if with_ties:
                    rank1 = _count_gt(v1, s1) + t_scr[0, j]
                    ckey = jnp.zeros((8, LANES), F32)
                    for r in range(TOPK):
                        ckey = jnp.where(rank1 == float(r), counts[r], ckey)
                    rank2 = _count_gt(v2, s2) + t_scr[1, j]
                else:
                    ckey = jnp.where(v1[0] > s1, counts[1], counts[0])
                    for r in range(1, TOPK - 1):
                        ckey = jnp.where(v1[r] > s1, counts[r + 1], ckey)
                    ckey = jnp.where(v1[TOPK - 1] > s1, 0.0, ckey)
                    rank2 = _count_gt(v2, s2)
                tabs[0].append(rank2)
                tabs[1].append(jnp.exp(s2 - v2[0]))
                tabs[2].append(ckey)
                tabs[3].append(jnp.exp(s1 - v1[0]) * inv_z)
            for ref, tab in zip((r2_ref, p2_ref, c_ref, p1_ref), tabs):
                for r, tile in enumerate(_xpose8(tab)):
                    ref[kg * 8:(kg + 1) * 8, lane_tile(r)] = tile

    @pl.when(has_tie)
    def _():
        t_scr[...] = jnp.zeros_like(t_scr)

        def lower_equal(jp, carry):
            for h in range(2):
                sv = s_ref[h, jp]
                for j in range(1, NKEYS):
                    inc = jnp.where(jp < j, 1.0, 0.0)
                    t_scr[h, j] = t_scr[h, j] + jnp.where(s_ref[h, j] == sv, inc, 0.0)
            return carry
        lax.fori_loop(0, NKEYS - 1, lower_equal, 0)
        tables(True)

    @pl.when(jnp.logical_not(has_tie))
    def _():
        tables(False)


def _select(scores):
    T = scores.shape[-1]
    tt = 8 * LANES
    tab = pl.BlockSpec((None, NKEYS, tt), lambda i, hd: (hd, 0, i))
    return pl.pallas_call(
        _select_kernel,
        grid=(T // tt, PEER_HEADS),
        in_specs=[pl.BlockSpec((2, NKEYS, tt), lambda i, hd: (hd, 0, i))],
        out_specs=[tab] * 4,
        out_shape=[jax.ShapeDtypeStruct((PEER_HEADS, NKEYS, T), F32)] * 4,
        scratch_shapes=[pltpu.VMEM((2, NKEYS, 8, LANES), F32), pltpu.VMEM((2, NKEYS, 8, LANES), F32)],
        compiler_params=_cparams(("arbitrary", "arbitrary")),
        name="select",
    )(scores)


_SQRT_HALF = 0.7071067811865476
_JROWS = 16
_PEER_NB = 8 * NKEYS
_PEER_STEPS = N_EXPERTS // _PEER_NB
_GATE_CHAINS = 1


def _zero_after(x):
    u = pltpu.bitcast(x, jnp.uint32)
    u = lax.shift_right_logical(lax.shift_right_logical(u, jnp.uint32(16)), jnp.uint32(16))
    return pltpu.bitcast(u, BF16)


def _peer_kernel(h2t_ref, u_ref, vt_ref, r2_ref, p2_ref, c_ref, p1_ref, x1_ref, mod_ref, gf_ref, yc_ref, yl_ref,
                 a_scr, w_scr, acc_scr, tab_scr, *, tb, n_ctx_tok, lat_len):
    L = pl.program_id(0)
    n_lg = tb // LANES
    vpu_blk = L - 1
    acc_blk = L - 2

    @pl.when(L == 0)
    def _():
        a_scr[...] = jnp.zeros_like(a_scr)
        w_scr[...] = jnp.zeros_like(w_scr)
        acc_scr[...] = jnp.zeros_like(acc_scr)

    @pl.when((L == 0) | (vpu_blk % _PEER_STEPS == 0))
    def _():
        for jg in range(NKEYS // _JROWS):
            rows = slice(jg * _JROWS, (jg + 1) * _JROWS)
            for lg in range(n_lg):
                lanes = slice(lg * LANES, (lg + 1) * LANES)
                for hd in range(PEER_HEADS):
                    tab_scr[jg, lg, hd * _JROWS:(hd + 1) * _JROWS, :] = r2_ref[hd, rows, lanes].astype(BF16)
                    tab_scr[jg, lg, (PEER_HEADS + hd) * _JROWS:(PEER_HEADS + hd + 1) * _JROWS, :] = \
                        p2_ref[hd, rows, lanes].astype(BF16)

    @pl.when((acc_blk >= 0) & (acc_blk % _PEER_STEPS == 0))
    def _():
        acc_scr[...] = jnp.zeros_like(acc_scr)

    def stages(cur):
        prv = 1 - cur
        a_scr[cur] = jnp.dot(pltpu.bitcast(u_ref[...], BF16), h2t_ref[...], preferred_element_type=F32)
        acc_scr[...] += jnp.dot(pltpu.bitcast(vt_ref[...], BF16), w_scr[cur], preferred_element_type=F32)
        link = [None] * _GATE_CHAINS
        for lg in range(n_lg):
            lanes = slice(lg * LANES, (lg + 1) * LANES)
            chain = lg * _GATE_CHAINS // n_lg
            for i8 in range(_PEER_NB // NKEYS):
                bcast = lambda ref, hd: jnp.broadcast_to(ref[hd, i8:i8 + 1, lanes], (_JROWS, LANES)).astype(BF16)
                crow = [bcast(c_ref, hd) for hd in range(PEER_HEADS)]
                p1row = [bcast(p1_ref, hd) for hd in range(PEER_HEADS)]
                for jg in range(NKEYS // _JROWS):
                    rows = slice(jg * _JROWS, (jg + 1) * _JROWS)
                    arow = slice(i8 * NKEYS + jg * _JROWS, i8 * NKEYS + (jg + 1) * _JROWS)
                    a = a_scr[prv, arow, lanes]
                    terms = []
                    for hd in range(PEER_HEADS):
                        r2 = tab_scr[jg, lg, hd * _JROWS:(hd + 1) * _JROWS, :]
                        p2 = tab_scr[jg, lg, (PEER_HEADS + hd) * _JROWS:(PEER_HEADS + hd + 1) * _JROWS, :]
                        terms.append(jnp.where(r2 < crow[hd], p2, jnp.zeros_like(p2)) * p1row[hd])
                    if link[chain] is not None:
                        terms[0] = terms[0] + _zero_after(link[chain])
                    while len(terms) > 1:
                        terms = [terms[k] + terms[k + 1] for k in range(0, len(terms), 2)]
                    g = terms[0]
                    gelu = a * (1.0 + lax.erf(a * _SQRT_HALF))
                    w = g * gelu.astype(BF16)
                    w_scr[prv, arow, lanes] = w
                    link[chain] = w

    for slot in range(2):
        pl.when(L % 2 == slot)(functools.partial(stages, slot))

    @pl.when((acc_blk >= 0) & (acc_blk % _PEER_STEPS == _PEER_STEPS - 1))
    def _():
        is_ctx, row = _mod_row(acc_blk // _PEER_STEPS, tb, n_ctx_tok, lat_len)
        ga2 = mod_ref[pl.ds(row, 1), 5 * D_MODEL:6 * D_MODEL]
        x2 = x1_ref[...] + ga2 * acc_scr[...].T
        y = _rms(x2) * gf_ref[...]

        @pl.when(is_ctx)
        def _():
            yc_ref[...] = y

        @pl.when(jnp.logical_not(is_ctx))
        def _():
            yl_ref[...] = y


def _pack_kernel(x_ref, o_ref, *, transpose):
    x = x_ref[...]
    if transpose:
        x = x.T
    o_ref[...] = pltpu.bitcast(x.astype(BF16), jnp.uint32)


def _pack_table(x, *, transpose):
    n, d = x.shape
    blk = _PEER_NB
    out_rows, out_cols = (d // 2, n) if transpose else (n // 2, d)
    return pl.pallas_call(
        functools.partial(_pack_kernel, transpose=transpose),
        grid=(n // blk,),
        in_specs=[pl.BlockSpec((blk, d), lambda j: (j, 0))],
        out_specs=(pl.BlockSpec((d // 2, blk), lambda j: (0, j)) if transpose
                   else pl.BlockSpec((blk // 2, d), lambda j: (j, 0))),
        out_shape=jax.ShapeDtypeStruct((out_rows, out_cols), jnp.uint32),
        compiler_params=_cparams(("arbitrary",)),
        name="pack_vt" if transpose else "pack_u",
    )(x)


def _peer(h2t, u_bf, vt_bf, r2, p2, cc, p1, x1, mod, g_final, *, tb, n_ctx_tok, lat_len):
    T = h2t.shape[1]
    nb = _PEER_NB
    n_tiles = T // tb
    nct = n_ctx_tok // tb
    n_blocks = n_tiles * _PEER_STEPS
    blk = lambda L, lag: jnp.clip(L - lag, 0, n_blocks - 1)
    tile = lambda L, lag: blk(L, lag) // _PEER_STEPS
    step = lambda L, lag: blk(L, lag) % _PEER_STEPS
    tab = pl.BlockSpec((PEER_HEADS, NKEYS, tb), lambda L: (0, 0, tile(L, 1)))
    rowtab = pl.BlockSpec((PEER_HEADS, nb // NKEYS, tb), lambda L: (0, step(L, 1), tile(L, 1)))
    return pl.pallas_call(
        functools.partial(_peer_kernel, tb=tb, n_ctx_tok=n_ctx_tok, lat_len=lat_len),
        grid=(n_blocks + 2,),
        in_specs=[pl.BlockSpec((D_MODEL, tb), lambda L: (0, tile(L, 0))),
                  pl.BlockSpec((nb // 2, D_MODEL), lambda L: (step(L, 0), 0)),
                  pl.BlockSpec((D_MODEL // 2, nb), lambda L: (0, step(L, 2))),
                  tab, tab, rowtab, rowtab,
                  pl.BlockSpec((tb, D_MODEL), lambda L: (tile(L, 2), 0)),
                  pl.BlockSpec(mod.shape, lambda L: (0, 0)),
                  pl.BlockSpec((1, D_MODEL), lambda L: (0, 0))],
        out_specs=[pl.BlockSpec((tb, D_MODEL), lambda L: (jnp.minimum(tile(L, 2), nct - 1), 0)),
                   pl.BlockSpec((tb, D_MODEL), lambda L: (jnp.maximum(tile(L, 2) - nct, 0), 0))],
        out_shape=[jax.ShapeDtypeStruct((n_ctx_tok, D_MODEL), F32),
                   jax.ShapeDtypeStruct((T - n_ctx_tok, D_MODEL), F32)],
        scratch_shapes=[pltpu.VMEM((2, nb, tb), F32), pltpu.VMEM((2, nb, tb), BF16), pltpu.VMEM((D_MODEL, tb), F32),
                        pltpu.VMEM((NKEYS // _JROWS, tb // LANES, 2 * PEER_HEADS * _JROWS, LANES), BF16)],
        compiler_params=_cparams(("arbitrary",)),
        name="peer",
    )(h2t, u_bf, vt_bf, r2, p2, cc, p1, x1, mod, g_final.reshape(1, -1))


def _layer(x_prompt, x_sample, state_C, state_n, state_m, c, c_ctx, w_ada, b_ada, g_norm1, w_in, b_igate, b_fgate,
           conv_w, conv_b, g_mlstm, w_out, g_norm2, peer_wq, peer_k1, peer_k2, peer_u, peer_v, g_final,
           *, tm, tc_lat, tb):
    B, S, _ = x_prompt.shape
    DB, DS, _ = x_sample.shape
    n_ctx_tok = B * S
    ng = N_DIRS * N_HEADS
    x_ctx = x_prompt.reshape(n_ctx_tok, D_MODEL)
    x_lat = x_sample.reshape(DB * DS, D_MODEL)
    cv8 = jnp.zeros((8, D_MODEL), F32).at[0].set(c_ctx).at[1:1 + DB].set(c)
    mod = _modulation(cv8, w_ada, b_ada)

    conv, q, k, v, og, gates, gates_t = _in_proj(
        x_ctx, x_lat, mod, g_norm1, w_in, b_igate, b_fgate, conv_w, conv_b,
        tm=tm, n_ctx_tok=n_ctx_tok, ctx_len=S, lat_len=DS)

    def ext_state(C, n_, m_):
        nb_ = jnp.broadcast_to(n_[..., None], n_.shape + (DH,))
        cext = jnp.concatenate([C, nb_], axis=-1).reshape(-1, ng, DH, 2 * DH)
        mext = jnp.broadcast_to(m_.reshape(-1, ng, 1), (C.shape[0], ng, LANES))
        return cext, mext

    z_c, z_m = ext_state(jnp.zeros((B, N_DIRS, N_HEADS, DH, DH), F32), jnp.zeros((B, N_DIRS, N_HEADS, DH), F32),
                         jnp.zeros((B, N_DIRS, N_HEADS), F32))
    hf_c, hb_c, c_fin, m_fin = _mlstm(q, k, v, gates, gates_t, z_c, z_m, tok0=0, nseq=B, seq_len=S, tc=S,
                                      write_state=True)
    l_c, l_m = ext_state(state_C, state_n, state_m)
    hf_l, hb_l = _mlstm(q, k, v, gates, gates_t, l_c, l_m, tok0=n_ctx_tok, nseq=DB, seq_len=DS, tc=tc_lat,
                        write_state=False)

    x1, h2t, scores = _mix(x_ctx, x_lat, conv, hf_c, hf_l, hb_c, hb_l, og, mod, g_mlstm, w_out, g_norm2,
                           peer_wq, peer_k1, peer_k2, tm=tm, n_ctx_tok=n_ctx_tok, lat_len=DS)
    r2, p2, cc, p1 = _select(scores)
    u_pk = _pack_table(peer_u, transpose=False)
    vt_pk = _pack_table(peer_v, transpose=True)
    y_ctx, y_lat = _peer(h2t, u_pk, vt_pk, r2, p2, cc, p1, x1, mod, g_final,
                         tb=tb, n_ctx_tok=n_ctx_tok, lat_len=DS)

    y_prompt = y_ctx.reshape(B, S, D_MODEL)
    y_sample = y_lat.reshape(DB, DS, D_MODEL)
    new_C = c_fin[..., :DH].reshape(B, 1, N_DIRS, N_HEADS, DH, DH)
    new_n = c_fin[..., DH].reshape(B, 1, N_DIRS, N_HEADS, DH)
    new_m = m_fin[..., 0].reshape(B, 1, N_DIRS, N_HEADS)
    return y_prompt, y_sample, new_C, new_n, new_m


def kernel(x_prompt, x_sample, state_C, state_n, state_m, c, c_ctx, w_ada, b_ada, g_norm1, w_in, b_igate, b_fgate,
           conv_w, conv_b, g_mlstm, w_out, g_norm2, peer_wq, peer_k1, peer_k2, peer_u, peer_v, g_final):
    return _layer(x_prompt, x_sample, state_C[:, 0], state_n[:, 0], state_m[:, 0], c, c_ctx, w_ada[0], b_ada[0],
                  g_norm1[0], w_in[0], b_igate[0], b_fgate[0], conv_w[0], conv_b[0], g_mlstm[0], w_out[0],
                  g_norm2[0], peer_wq[0], peer_k1[0], peer_k2[0], peer_u[0], peer_v[0], g_final,
                  tm=512, tc_lat=256, tb=512)
```

```python
import functools

import jax
import jax.numpy as jnp
from jax import lax
from jax.experimental import pallas as pl
from jax.experimental.pallas import tpu as pltpu

F32 = jnp.float32
BF16 = jnp.bfloat16

D_MODEL = 1024
CONV_W = 512
N_HEADS = 4
DH = 128
MLSTM_W = N_HEADS * DH
N_DIRS = 2
GRID_W = 64
N_GROUPS = 7
GROUP_W = 512
GATE_COL0 = N_GROUPS * GROUP_W
N_GATES = 2 * N_DIRS * N_HEADS
PEER_HEADS = 8
NKEYS = 128
TOPK = 16
N_EXPERTS = NKEYS * NKEYS
EPS = 1e-6
LANES = 128
SUBLANES = 8
NEG_INF = float("-inf")

VMEM_LIMIT = 56 * 1024 * 1024


def _cparams(sem):
    return pltpu.CompilerParams(dimension_semantics=sem, vmem_limit_bytes=VMEM_LIMIT)


def _mod_kernel(cv_ref, w_ref, b_ref, o_ref):
    cv = cv_ref[...]
    s = cv * (1.0 / (1.0 + jnp.exp(-cv)))
    o_ref[...] = jnp.dot(s, w_ref[...], preferred_element_type=F32) + b_ref[...]


def _modulation(cv8, w_ada, b_ada):
    n = w_ada.shape[1]
    tn = 1024
    return pl.pallas_call(
        _mod_kernel,
        grid=(n // tn,),
        in_specs=[pl.BlockSpec((8, D_MODEL), lambda j: (0, 0)),
                  pl.BlockSpec((D_MODEL, tn), lambda j: (0, j)),
                  pl.BlockSpec((1, tn), lambda j: (0, j))],
        out_specs=pl.BlockSpec((8, tn), lambda j: (0, j)),
        out_shape=jax.ShapeDtypeStruct((8, n), F32),
        compiler_params=_cparams(("arbitrary",)),
        name="modulation",
    )(cv8, w_ada, b_ada.reshape(1, n))


def _mod_row(i, tm, n_ctx_tok, lat_len):
    n_ctx_tiles = n_ctx_tok // tm
    tiles_per_seq = lat_len // tm
    is_ctx = i < n_ctx_tiles
    row = jnp.where(is_ctx, 0, 1 + (i - n_ctx_tiles) // tiles_per_seq)
    return is_ctx, row


def _group_specs(tm, width, n_ctx_tok):
    nct = n_ctx_tok // tm
    return [pl.BlockSpec((tm, width), lambda i: (jnp.minimum(i, nct - 1), 0)),
            pl.BlockSpec((tm, width), lambda i: (jnp.maximum(i - nct, 0), 0))]


def _rms(x):
    return x * lax.rsqrt(jnp.mean(x * x, axis=-1, keepdims=True) + EPS)


def _log_sigmoid(z):
    return jnp.minimum(z, 0.0) - jnp.log(1.0 + jnp.exp(-jnp.abs(z)))


def _inproj_kernel(xc_ref, xl_ref, mod_ref, g1_ref, win_ref, wg_ref, wgt_ref, bg_ref, bgt_ref, cw_ref, cb_ref,
                   conv_ref, q_ref, k_ref, v_ref, o_ref, gates_ref, gatest_ref,
                   *, tm, n_ctx_tok, ctx_len, lat_len):
    i = pl.program_id(0)
    is_ctx, row = _mod_row(i, tm, n_ctx_tok, lat_len)
    sh1 = mod_ref[pl.ds(row, 1), 0:D_MODEL]
    sc1 = mod_ref[pl.ds(row, 1), D_MODEL:2 * D_MODEL]
    x = jnp.where(is_ctx, xc_ref[...], xl_ref[...])
    h = _rms(x) * g1_ref[...] * (1.0 + sc1) + sh1
    hb = h.astype(BF16)

    def grp(g):
        return jnp.dot(hb, win_ref[:, g * GROUP_W:(g + 1) * GROUP_W], preferred_element_type=F32)

    u = grp(1) * grp(2)
    pm = jnp.where(is_ctx, ctx_len - 1, GRID_W - 1)
    pos = lax.broadcasted_iota(jnp.int32, (tm, 1), 0) & pm
    u_prev = jnp.where(pos == 0, 0.0, pltpu.roll(u, 1, axis=0))
    u_next = jnp.where(pos == pm, 0.0, pltpu.roll(u, tm - 1, axis=0))
    y = u_prev * cw_ref[0:1, :] + u * cw_ref[1:2, :] + u_next * cw_ref[2:3, :] + cb_ref[...]
    conv_ref[...] = (grp(0) * y).astype(BF16)

    q_ref[...] = (grp(3) * (DH ** -0.5)).astype(BF16)
    k_ref[...] = grp(4).astype(BF16)
    v_ref[...] = grp(5).astype(BF16)
    o_ref[...] = grp(6)

    z = jnp.dot(hb, wg_ref[...], preferred_element_type=F32) + bg_ref[...]
    lane = lax.broadcasted_iota(jnp.int32, z.shape, 1)
    gates_ref[...] = jnp.where(lane >= N_DIRS * N_HEADS, _log_sigmoid(z), z)
    zt = lax.dot_general(wgt_ref[...], hb, (((1,), (1,)), ((), ())), preferred_element_type=F32) + bgt_ref[...]
    sub = lax.broadcasted_iota(jnp.int32, zt.shape, 0)
    gatest_ref[...] = jnp.where(sub >= N_DIRS * N_HEADS, _log_sigmoid(zt), zt)


def _in_proj(x_ctx, x_lat, mod, g1, w_in, b_igate, b_fgate, conv_w, conv_b, *, tm, n_ctx_tok, ctx_len, lat_len):
    T = x_ctx.shape[0] + x_lat.shape[0]
    win = w_in[:, :GATE_COL0].astype(BF16)
    wg = jnp.zeros((D_MODEL, LANES), F32).at[:, :N_GATES].set(w_in[:, GATE_COL0:]).astype(BF16)
    bgate = jnp.zeros((1, LANES), F32).at[0, :N_GATES].set(
        jnp.concatenate([b_igate.reshape(-1), b_fgate.reshape(-1)]))
    tok = lambda w: pl.BlockSpec((tm, w), lambda i: (i, 0))
    full = lambda a: pl.BlockSpec(a.shape, lambda i: (0,) * a.ndim)
    args = (x_ctx, x_lat, mod, g1.reshape(1, -1), win, wg, wg.T, bgate, bgate.T, conv_w, conv_b.reshape(1, -1))
    return pl.pallas_call(
        functools.partial(_inproj_kernel, tm=tm, n_ctx_tok=n_ctx_tok, ctx_len=ctx_len, lat_len=lat_len),
        grid=(T // tm,),
        in_specs=_group_specs(tm, D_MODEL, n_ctx_tok) + [full(a) for a in args[2:]],
        out_specs=[tok(CONV_W), tok(MLSTM_W), tok(MLSTM_W), tok(MLSTM_W), tok(MLSTM_W), tok(LANES),
                   pl.BlockSpec((LANES, tm), lambda i: (0, i))],
        out_shape=[jax.ShapeDtypeStruct((T, CONV_W), BF16),
                   jax.ShapeDtypeStruct((T, MLSTM_W), BF16),
                   jax.ShapeDtypeStruct((T, MLSTM_W), BF16),
                   jax.ShapeDtypeStruct((T, MLSTM_W), BF16),
                   jax.ShapeDtypeStruct((T, MLSTM_W), F32),
                   jax.ShapeDtypeStruct((T, LANES), F32),
                   jax.ShapeDtypeStruct((LANES, T), F32)],
        compiler_params=_cparams(("arbitrary",)),
        name="in_proj",
    )(*args)


def _split_bf16(x):
    hi = x.astype(BF16)
    return hi, (x - hi.astype(F32)).astype(BF16)


def _mlstm_kernel(qf_ref, kf_ref, vf_ref, gf_ref, gtf_ref, qb_ref, kb_ref, vb_ref, gb_ref, gtb_ref,
                  c0_ref, m0_ref, hf_ref, hb_ref, *rest, tc, write_state):
    if write_state:
        cout_ref, mout_ref, c_scr, m_scr = rest
    else:
        c_scr, m_scr = rest
    c = pl.program_id(1)

    @pl.when(c == 0)
    def _():
        c_scr[...] = c0_ref[...]
        m_scr[...] = m0_ref[...]

    r_io = lax.broadcasted_iota(jnp.int32, (tc, tc), 0)
    c_io = lax.broadcasted_iota(jnp.int32, (tc, tc), 1)
    ones = jnp.ones((tc, DH), BF16)
    wide = lambda x, width: jnp.concatenate([x] * (width // DH), axis=1)

    for d in range(N_DIRS):
        q_ref, k_ref, v_ref, g_ref, gt_ref, h_ref = (
            (qf_ref, kf_ref, vf_ref, gf_ref, gtf_ref, hf_ref) if d == 0 else
            (qb_ref, kb_ref, vb_ref, gb_ref, gtb_ref, hb_ref))
        causal = (c_io <= r_io) if d == 0 else (c_io >= r_io)
        last = tc - 1 if d == 0 else 0
        tri = causal.astype(BF16)
        tri_t = ((r_io <= c_io) if d == 0 else (r_io >= c_io)).astype(BF16)
        g = g_ref[...]
        gt = gt_ref[...]
        g_hi, g_lo = _split_bf16(g)
        gt_hi, gt_lo = _split_bf16(gt)
        bcum = (jnp.dot(tri, g_hi, preferred_element_type=F32) + jnp.dot(tri, g_lo, preferred_element_type=F32))
        bcum_t = (jnp.dot(gt_hi, tri_t, preferred_element_type=F32)
                  + jnp.dot(gt_lo, tri_t, preferred_element_type=F32))
        for hd in range(N_HEADS):
            gi = d * N_HEADS + hd
            gfi = N_DIRS * N_HEADS + gi
            sl = slice(hd * DH, (hd + 1) * DH)
            q = q_ref[:, sl]
            k = k_ref[:, sl]
            v = v_ref[:, sl]
            ig_c = jnp.broadcast_to(g[:, gi:gi + 1], (tc, DH))
            b_c = jnp.broadcast_to(bcum[:, gfi:gfi + 1], (tc, DH))
            ig_r = gt[gi:gi + 1, :]
            b_r = bcum_t[gfi:gfi + 1, :]
            m_prev = m_scr[gi:gi + 1, :]
            dmat = jnp.where(causal, wide(b_c, tc) - b_r + ig_r, NEG_INF)
            inter = b_c + m_prev
            m_t = jnp.maximum(inter, jnp.max(dmat, axis=-1, keepdims=True))
            s = (lax.dot_general(q, k, (((1,), (1,)), ((), ())), preferred_element_type=F32)
                 * jnp.exp(dmat - wide(m_t, tc)))
            a = jnp.exp(inter - m_t)
            cext = c_scr[gi]
            vext = jnp.concatenate([v, ones], axis=1)
            nd = (wide(a, 2 * DH) * jnp.dot(q, cext.astype(BF16), preferred_element_type=F32)
                  + jnp.dot(s.astype(BF16), vext, preferred_element_type=F32))
            num = nd[:, :DH]
            den = nd[:, DH:]
            h_ref[:, sl] = num / jnp.maximum(jnp.abs(den), jnp.exp(-m_t))
            m_new = m_t[last:last + 1, :]
            b_last = b_c[last:last + 1, :]
            w_c = jnp.exp(b_last - b_c + ig_c - m_new)
            decay = jnp.exp(b_last + m_prev - m_new)
            kw_t = (k.astype(F32) * w_c).T.astype(BF16)
            c_scr[gi] = wide(decay, 2 * DH) * cext + jnp.dot(kw_t, vext, preferred_element_type=F32)
            m_scr[gi:gi + 1, :] = m_new

    if write_state:
        @pl.when(c == pl.num_programs(1) - 1)
        def _():
            cout_ref[...] = c_scr[...]
            mout_ref[...] = m_scr[...]


def _mlstm(q, k, v, gates, gates_t, c0ext, m0, *, tok0, nseq, seq_len, tc, write_state):
    nc = seq_len // tc
    blk0 = tok0 // tc
    fwd = lambda b, c: b * nc + c
    bwd = lambda b, c: b * nc + nc - 1 - c
    tokspec = lambda w, f, o: pl.BlockSpec((tc, w), lambda b, c: (o + f(b, c), 0))
    gtspec = lambda f: pl.BlockSpec((LANES, tc), lambda b, c: (0, blk0 + f(b, c)))
    ng = N_DIRS * N_HEADS
    in_specs = [tokspec(MLSTM_W, fwd, blk0)] * 3 + [tokspec(LANES, fwd, blk0), gtspec(fwd)] \
        + [tokspec(MLSTM_W, bwd, blk0)] * 3 + [tokspec(LANES, bwd, blk0), gtspec(bwd)] \
        + [pl.BlockSpec((None, ng, DH, 2 * DH), lambda b, c: (b, 0, 0, 0)),
           pl.BlockSpec((None, ng, LANES), lambda b, c: (b, 0, 0))]
    out_specs = [tokspec(MLSTM_W, fwd, 0), tokspec(MLSTM_W, bwd, 0)]
    out_shape = [jax.ShapeDtypeStruct((nseq * seq_len, MLSTM_W), F32)] * 2
    if write_state:
        out_specs += [pl.BlockSpec((None, ng, DH, 2 * DH), lambda b, c: (b, 0, 0, 0)),
                      pl.BlockSpec((None, ng, LANES), lambda b, c: (b, 0, 0))]
        out_shape += [jax.ShapeDtypeStruct((nseq, ng, DH, 2 * DH), F32),
                      jax.ShapeDtypeStruct((nseq, ng, LANES), F32)]
    return pl.pallas_call(
        functools.partial(_mlstm_kernel, tc=tc, write_state=write_state),
        grid=(nseq, nc),
        in_specs=in_specs,
        out_specs=out_specs,
        out_shape=out_shape,
        scratch_shapes=[pltpu.VMEM((ng, DH, 2 * DH), F32), pltpu.VMEM((ng, LANES), F32)],
        compiler_params=_cparams(("arbitrary", "arbitrary")),
        name="mlstm_ctx" if write_state else "mlstm_lat",
    )(q, k, v, gates, gates_t, q, k, v, gates, gates_t, c0ext, m0)


def _mix_kernel(xc_ref, xl_ref, conv_ref, hfc_ref, hfl_ref, hbc_ref, hbl_ref, og_ref, mod_ref, gm_ref, wout_ref,
                g2_ref, wst_ref, x1_ref, h2t_ref, sc_ref, *, tm, n_ctx_tok, lat_len):
    i = pl.program_id(0)
    is_ctx, row = _mod_row(i, tm, n_ctx_tok, lat_len)
    mrow = lambda j: mod_ref[pl.ds(row, 1), j * D_MODEL:(j + 1) * D_MODEL]
    hs = jnp.where(is_ctx, hfc_ref[...] + hbc_ref[...], hfl_ref[...] + hbl_ref[...])
    hn = jnp.concatenate([_rms(hs[:, hd * DH:(hd + 1) * DH]) for hd in range(N_HEADS)], axis=1)
    og = og_ref[...]
    ml = (hn * gm_ref[...] * (1.0 / (1.0 + jnp.exp(-og)))).astype(BF16)
    mix = (jnp.dot(conv_ref[...], wout_ref[0:CONV_W, :], preferred_element_type=F32)
           + jnp.dot(ml, wout_ref[CONV_W:, :], preferred_element_type=F32))
    x1 = jnp.where(is_ctx, xc_ref[...], xl_ref[...]) + mrow(2) * mix
    x1_ref[...] = x1
    h2f = _rms(x1) * g2_ref[...] * (1.0 + mrow(4)) + mrow(3)
    h2t = h2f.T.astype(BF16)
    h2t_ref[...] = h2t
    sc = jnp.dot(wst_ref[...], h2t, preferred_element_type=F32)
    sc_ref[...] = sc.reshape(2 * PEER_HEADS, NKEYS, tm)


def _fold_keys_kernel(k_ref, wq_ref, o_ref):
    o_ref[...] = lax.dot_general(k_ref[...], wq_ref[...], (((1,), (1,)), ((), ())),
                                 preferred_element_type=F32).astype(BF16)


def _fold_keys(wq, k1, k2):
    n = 2 * PEER_HEADS
    keys = jnp.stack([k1, k2])
    return pl.pallas_call(
        _fold_keys_kernel,
        grid=(n,),
        in_specs=[pl.BlockSpec((None, NKEYS, NKEYS), lambda j: (j % 2, 0, 0)),
                  pl.BlockSpec((D_MODEL, NKEYS), lambda j: (0, j))],
        out_specs=pl.BlockSpec((NKEYS, D_MODEL), lambda j: (j, 0)),
        out_shape=jax.ShapeDtypeStruct((n * NKEYS, D_MODEL), BF16),
        compiler_params=_cparams(("arbitrary",)),
        name="fold_keys",
    )(keys, wq)


def _mix(x_ctx, x_lat, conv, hf_ctx, hf_lat, hb_ctx, hb_lat, og, mod, g_mlstm, w_out, g2, wq, k1, k2,
         *, tm, n_ctx_tok, lat_len):
    T = x_ctx.shape[0] + x_lat.shape[0]
    tok = lambda w: pl.BlockSpec((tm, w), lambda i: (i, 0))
    full = lambda a: pl.BlockSpec(a.shape, lambda i: (0,) * a.ndim)
    consts = (mod, g_mlstm.reshape(1, -1), w_out.astype(BF16), g2.reshape(1, -1), _fold_keys(wq, k1, k2))
    return pl.pallas_call(
        functools.partial(_mix_kernel, tm=tm, n_ctx_tok=n_ctx_tok, lat_len=lat_len),
        grid=(T // tm,),
        in_specs=_group_specs(tm, D_MODEL, n_ctx_tok) + [tok(CONV_W)] + _group_specs(tm, MLSTM_W, n_ctx_tok) * 2
        + [tok(MLSTM_W)] + [full(a) for a in consts],
        out_specs=[tok(D_MODEL), pl.BlockSpec((D_MODEL, tm), lambda i: (0, i)),
                   pl.BlockSpec((2 * PEER_HEADS, NKEYS, tm), lambda i: (0, 0, i))],
        out_shape=[jax.ShapeDtypeStruct((T, D_MODEL), F32),
                   jax.ShapeDtypeStruct((D_MODEL, T), BF16),
                   jax.ShapeDtypeStruct((2 * PEER_HEADS, NKEYS, T), F32)],
        compiler_params=_cparams(("arbitrary",)),
        name="mix",
    )(x_ctx, x_lat, conv, hf_ctx, hf_lat, hb_ctx, hb_lat, og, *consts)


def _sort16_pairs():
    def merge(lo, hi, r):
        step = r * 2
        if step < hi - lo:
            yield from merge(lo, hi, step)
            yield from merge(lo + r, hi, step)
            yield from ((i, i + r) for i in range(lo + r, hi - r, step))
        else:
            yield (lo, lo + r)

    def sort(lo, hi):
        if hi - lo >= 1:
            mid = lo + (hi - lo) // 2
            yield from sort(lo, mid)
            yield from sort(mid + 1, hi)
            yield from merge(lo, hi, 1)

    return tuple(sort(0, TOPK - 1))


_SORT16 = _sort16_pairs()
_CAND_PAIRS = tuple((r1, r2) for r1 in range(TOPK) for r2 in range(TOPK) if (r1 + 1) * (r2 + 1) <= TOPK)


def _cmpx(x, i, j):
    a, b = x[i], x[j]
    if b is None:
        return
    if a is None:
        x[i], x[j] = b, None
        return
    x[i], x[j] = jnp.maximum(a, b), jnp.minimum(a, b)


def _sort16(x):
    x = list(x)
    for i, j in _SORT16:
        _cmpx(x, i, j)
    return x


def _merge_top16(a, b):
    x = []
    for k in range(TOPK):
        p, q = a[k], b[TOPK - 1 - k]
        x.append(q if p is None else p if q is None else jnp.maximum(p, q))
    d = TOPK // 2
    while d:
        for k in range(TOPK):
            if not k & d:
                _cmpx(x, k, k + d)
        d //= 2
    return x


def _top16_values(groups):
    groups = [_sort16(g) for g in groups]
    while len(groups) > 1:
        groups = [_merge_top16(groups[k], groups[k + 1]) for k in range(0, len(groups), 2)]
    return groups[0]


def _count_gt(vs, s):
    g = jnp.where(vs[0] > s, 1.0, 0.0)
    for r in range(1, len(vs)):
        g = jnp.where(vs[r] > s, float(r + 1), g)
    return g


def _xpose8(x):
    x = list(x)
    sub = lax.broadcasted_iota(jnp.int32, (SUBLANES, LANES), 0)
    for d in (4, 2, 1):
        keep = (sub & d) == 0
        for a in range(8):
            if not a & d:
                xa, xb = x[a], x[a + d]
                x[a] = jnp.where(keep, xa, pltpu.roll(xb, d, axis=0))
                x[a + d] = jnp.where(keep, pltpu.roll(xa, 8 - d, axis=0), xb)
    return x


def _select_kernel(sc_ref, r2_ref, p2_ref, c_ref, p1_ref, s_ref, t_scr):
    one = lambda m: jnp.where(m, 1.0, 0.0)
    lane_tile = lambda r: slice(r * LANES, (r + 1) * LANES)
    for h in range(2):
        for kg in range(NKEYS // 8):
            tiles = _xpose8([sc_ref[h, kg * 8:(kg + 1) * 8, lane_tile(r)] for r in range(8)])
            for k in range(8):
                s_ref[h, kg * 8 + k] = tiles[k]
    v = [_top16_values([[s_ref[h, g * TOPK + k] for k in range(TOPK)] for g in range(NKEYS // TOPK)])
         for h in range(2)]
    v1, v2 = v

    tie = jnp.zeros((SUBLANES, LANES), F32)
    for h in range(2):
        n_ge = one(s_ref[h, 0] >= v[h][TOPK - 1])
        for j in range(1, NKEYS):
            n_ge = n_ge + one(s_ref[h, j] >= v[h][TOPK - 1])
        tie = jnp.maximum(tie, one(n_ge > float(TOPK)))
        for r in range(TOPK - 1):
            tie = jnp.maximum(tie, one(v[h][r] == v[h][r + 1]))
    has_tie = jnp.max(tie) > 0.0

    cand = {p: v1[p[0]] + v2[p[1]] for p in _CAND_PAIRS}
    rest = [cand[p] for p in _CAND_PAIRS if p[0] > 0]
    rest += [None] * (-len(rest) % TOPK)
    groups = [[cand[(0, r2)] for r2 in range(TOPK)]] + [rest[k:k + TOPK] for k in range(0, len(rest), TOPK)]
    while len(groups) & (len(groups) - 1):
        groups.append([None] * TOPK)
    w = _top16_values(groups)
    tau = w[TOPK - 1]
    n_gt = sum(one(cand[p] > tau) for p in _CAND_PAIRS)
    need = float(TOPK) - n_gt
    eq_seen = jnp.zeros((SUBLANES, LANES), F32)
    counts = [jnp.zeros((SUBLANES, LANES), F32) for _ in range(TOPK)]
    for p in _CAND_PAIRS:
        eq = cand[p] == tau
        counts[p[0]] = counts[p[0]] + one((cand[p] > tau) | (eq & (eq_seen < need)))
        eq_seen = eq_seen + one(eq)
    z = jnp.ones((SUBLANES, LANES), F32)
    for r in range(1, TOPK):
        z = z + jnp.exp(w[r] - w[0])
    inv_z = 0.5 / z

    def tables(with_ties):
        for kg in range(NKEYS // 8):
            tabs = [[], [], [], []]
            for j in range(kg * 8, (kg + 1) * 8):
                s1 = s_ref[0, j]
                s2 = s_ref[1, j]
                if with_ties:
                    rank1 = _count_gt(v1, s1) + t_scr[0, j]
                    ckey = jnp.zeros((SUBLANES, LANES), F32)
                    for r in range(TOPK):
                        ckey = jnp.where(rank1 == float(r), counts[r], ckey)
                    rank2 = _count_gt(v2, s2) + t_scr[1, j]
                else:
                    ckey = jnp.where(v1[0] > s1, counts[1], counts[0])
                    for r in range(1, TOPK - 1):
                        ckey = jnp.where(v1[r] > s1, counts[r + 1], ckey)
                    ckey = jnp.where(v1[TOPK - 1] > s1, 0.0, ckey)
                    rank2 = _count_gt(v2, s2)
                tabs[0].append(rank2)
                tabs[1].append(jnp.exp(s2 - v2[0]))
                tabs[2].append(ckey)
                tabs[3].append(jnp.exp(s1 - v1[0]) * inv_z)
            for ref, tab in zip((r2_ref, p2_ref, c_ref, p1_ref), tabs):
                for r, tile in enumerate(_xpose8(tab)):
                    ref[kg * 8:(kg + 1) * 8, lane_tile(r)] = tile

    @pl.when(has_tie)
    def _():
        t_scr[...] = jnp.zeros_like(t_scr)

        def lower_equal(jp, carry):
            for h in range(2):
                sv = s_ref[h, jp]
                for j in range(1, NKEYS):
                    inc = jnp.where(jp < j, 1.0, 0.0)
                    t_scr[h, j] = t_scr[h, j] + jnp.where(s_ref[h, j] == sv, inc, 0.0)
            return carry
        lax.fori_loop(0, NKEYS - 1, lower_equal, 0)
        tables(True)

    @pl.when(jnp.logical_not(has_tie))
    def _():
        tables(False)


def _select(scores):
    T = scores.shape[-1]
    tt = 8 * LANES
    tab = pl.BlockSpec((None, NKEYS, tt), lambda i, hd: (hd, 0, i))
    return pl.pallas_call(
        _select_kernel,
        grid=(T // tt, PEER_HEADS),
        in_specs=[pl.BlockSpec((2, NKEYS, tt), lambda i, hd: (hd, 0, i))],
        out_specs=[tab] * 4,
        out_shape=[jax.ShapeDtypeStruct((PEER_HEADS, NKEYS, T), F32)] * 4,
        scratch_shapes=[pltpu.VMEM((2, NKEYS, 8, LANES), F32), pltpu.VMEM((2, NKEYS, 8, LANES), F32)],
        compiler_params=_cparams(("arbitrary", "arbitrary")),
        name="select",
    )(scores)


_SQRT_HALF = 0.7071067811865476
_JROWS = 16
_PEER_NB = 8 * NKEYS
_PEER_STEPS = N_EXPERTS // _PEER_NB


def _zero_after(x):
    u = pltpu.bitcast(x, jnp.uint32)
    u = lax.shift_right_logical(lax.shift_right_logical(u, jnp.uint32(16)), jnp.uint32(16))
    return pltpu.bitcast(u, BF16)


def _peer_kernel(h2t_ref, u_ref, vt_ref, r2_ref, p2_ref, c_ref, p1_ref, x1_ref, mod_ref, gf_ref, yc_ref, yl_ref,
                 a_scr, w_scr, acc_scr, tab_scr, *, tb, n_ctx_tok, lat_len):
    L = pl.program_id(0)
    n_lg = tb // LANES
    vpu_blk = L - 1
    acc_blk = L - 2

    @pl.when(L == 0)
    def _():
        a_scr[...] = jnp.zeros_like(a_scr)
        w_scr[...] = jnp.zeros_like(w_scr)
        acc_scr[...] = jnp.zeros_like(acc_scr)

    @pl.when((L == 0) | (vpu_blk % _PEER_STEPS == 0))
    def _():
        for jg in range(NKEYS // _JROWS):
            rows = slice(jg * _JROWS, (jg + 1) * _JROWS)
            for lg in range(n_lg):
                lanes = slice(lg * LANES, (lg + 1) * LANES)
                for hd in range(PEER_HEADS):
                    tab_scr[jg, lg, hd * _JROWS:(hd + 1) * _JROWS, :] = r2_ref[hd, rows, lanes].astype(BF16)
                    tab_scr[jg, lg, (PEER_HEADS + hd) * _JROWS:(PEER_HEADS + hd + 1) * _JROWS, :] = \
                        p2_ref[hd, rows, lanes].astype(BF16)

    @pl.when((acc_blk >= 0) & (acc_blk % _PEER_STEPS == 0))
    def _():
        acc_scr[...] = jnp.zeros_like(acc_scr)

    def stages(cur):
        prv = 1 - cur
        a_scr[cur] = jnp.dot(pltpu.bitcast(u_ref[...], BF16), h2t_ref[...], preferred_element_type=F32)
        acc_scr[...] += jnp.dot(pltpu.bitcast(vt_ref[...], BF16), w_scr[cur], preferred_element_type=F32)
        prev_w = None
        for lg in range(n_lg):
            lanes = slice(lg * LANES, (lg + 1) * LANES)
            for i8 in range(_PEER_NB // NKEYS):
                bcast = lambda ref, hd: jnp.broadcast_to(ref[hd, i8:i8 + 1, lanes], (_JROWS, LANES)).astype(BF16)
                crow = [bcast(c_ref, hd) for hd in range(PEER_HEADS)]
                p1row = [bcast(p1_ref, hd) for hd in range(PEER_HEADS)]
                for jg in range(NKEYS // _JROWS):
                    rows = slice(jg * _JROWS, (jg + 1) * _JROWS)
                    arow = slice(i8 * NKEYS + jg * _JROWS, i8 * NKEYS + (jg + 1) * _JROWS)
                    a = a_scr[prv, arow, lanes]
                    terms = []
                    for hd in range(PEER_HEADS):
                        r2 = tab_scr[jg, lg, hd * _JROWS:(hd + 1) * _JROWS, :]
                        p2 = tab_scr[jg, lg, (PEER_HEADS + hd) * _JROWS:(PEER_HEADS + hd + 1) * _JROWS, :]
                        terms.append(jnp.where(r2 < crow[hd], p2, jnp.zeros_like(p2)) * p1row[hd])
                    if prev_w is not None:
                        terms[0] = terms[0] + _zero_after(prev_w)
                    while len(terms) > 1:
                        terms = [terms[k] + terms[k + 1] for k in range(0, len(terms), 2)]
                    g = terms[0]
                    gelu = a * (1.0 + lax.erf(a * _SQRT_HALF))
                    w = g * gelu.astype(BF16)
                    w_scr[prv, arow, lanes] = w
                    prev_w = w

    for slot in range(2):
        pl.when(L % 2 == slot)(functools.partial(stages, slot))

    @pl.when((acc_blk >= 0) & (acc_blk % _PEER_STEPS == _PEER_STEPS - 1))
    def _():
        is_ctx, row = _mod_row(acc_blk // _PEER_STEPS, tb, n_ctx_tok, lat_len)
        ga2 = mod_ref[pl.ds(row, 1), 5 * D_MODEL:6 * D_MODEL]
        x2 = x1_ref[...] + ga2 * acc_scr[...].T
        y = _rms(x2) * gf_ref[...]

        @pl.when(is_ctx)
        def _():
            yc_ref[...] = y

        @pl.when(jnp.logical_not(is_ctx))
        def _():
            yl_ref[...] = y


def _pack_kernel(x_ref, o_ref, *, transpose):
    x = x_ref[...]
    if transpose:
        x = x.T
    o_ref[...] = pltpu.bitcast(x.astype(BF16), jnp.uint32)


def _pack_table(x, *, transpose):
    n, d = x.shape
    blk = _PEER_NB
    out_rows, out_cols = (d // 2, n) if transpose else (n // 2, d)
    return pl.pallas_call(
        functools.partial(_pack_kernel, transpose=transpose),
        grid=(n // blk,),
        in_specs=[pl.BlockSpec((blk, d), lambda j: (j, 0))],
        out_specs=(pl.BlockSpec((d // 2, blk), lambda j: (0, j)) if transpose
                   else pl.BlockSpec((blk // 2, d), lambda j: (j, 0))),
        out_shape=jax.ShapeDtypeStruct((out_rows, out_cols), jnp.uint32),
        compiler_params=_cparams(("arbitrary",)),
        name="pack_vt" if transpose else "pack_u",
    )(x)


def _peer(h2t, u_bf, vt_bf, r2, p2, cc, p1, x1, mod, g_final, *, tb, n_ctx_tok, lat_len):
    T = h2t.shape[1]
    nb = _PEER_NB
    n_tiles = T // tb
    nct = n_ctx_tok // tb
    n_blocks = n_tiles * _PEER_STEPS
    blk = lambda L, lag: jnp.clip(L - lag, 0, n_blocks - 1)
    tile = lambda L, lag: blk(L, lag) // _PEER_STEPS
    step = lambda L, lag: blk(L, lag) % _PEER_STEPS
    tab = pl.BlockSpec((PEER_HEADS, NKEYS, tb), lambda L: (0, 0, tile(L, 1)))
    rowtab = pl.BlockSpec((PEER_HEADS, nb // NKEYS, tb), lambda L: (0, step(L, 1), tile(L, 1)))
    return pl.pallas_call(
        functools.partial(_peer_kernel, tb=tb, n_ctx_tok=n_ctx_tok, lat_len=lat_len),
        grid=(n_blocks + 2,),
        in_specs=[pl.BlockSpec((D_MODEL, tb), lambda L: (0, tile(L, 0))),
                  pl.BlockSpec((nb // 2, D_MODEL), lambda L: (step(L, 0), 0)),
                  pl.BlockSpec((D_MODEL // 2, nb), lambda L: (0, step(L, 2))),
                  tab, tab, rowtab, rowtab,
                  pl.BlockSpec((tb, D_MODEL), lambda L: (tile(L, 2), 0)),
                  pl.BlockSpec(mod.shape, lambda L: (0, 0)),
                  pl.BlockSpec((1, D_MODEL), lambda L: (0, 0))],
        out_specs=[pl.BlockSpec((tb, D_MODEL), lambda L: (jnp.minimum(tile(L, 2), nct - 1), 0)),
                   pl.BlockSpec((tb, D_MODEL), lambda L: (jnp.maximum(tile(L, 2) - nct, 0), 0))],
        out_shape=[jax.ShapeDtypeStruct((n_ctx_tok, D_MODEL), F32),
                   jax.ShapeDtypeStruct((T - n_ctx_tok, D_MODEL), F32)],
        scratch_shapes=[pltpu.VMEM((2, nb, tb), F32), pltpu.VMEM((2, nb, tb), BF16), pltpu.VMEM((D_MODEL, tb), F32),
                        pltpu.VMEM((NKEYS // _JROWS, tb // LANES, 2 * PEER_HEADS * _JROWS, LANES), BF16)],
        compiler_params=_cparams(("arbitrary",)),
        name="peer",
    )(h2t, u_bf, vt_bf, r2, p2, cc, p1, x1, mod, g_final.reshape(1, -1))


def _layer(x_prompt, x_sample, state_C, state_n, state_m, c, c_ctx, w_ada, b_ada, g_norm1, w_in, b_igate, b_fgate,
           conv_w, conv_b, g_mlstm, w_out, g_norm2, peer_wq, peer_k1, peer_k2, peer_u, peer_v, g_final,
           *, tm, tc_lat, tb):
    B, S, _ = x_prompt.shape
    DB, DS, _ = x_sample.shape
    n_ctx_tok = B * S
    ng = N_DIRS * N_HEADS
    x_ctx = x_prompt.reshape(n_ctx_tok, D_MODEL)
    x_lat = x_sample.reshape(DB * DS, D_MODEL)
    cv8 = jnp.zeros((8, D_MODEL), F32).at[0].set(c_ctx).at[1:1 + DB].set(c)
    mod = _modulation(cv8, w_ada, b_ada)

    conv, q, k, v, og, gates, gates_t = _in_proj(
        x_ctx, x_lat, mod, g_norm1, w_in, b_igate, b_fgate, conv_w, conv_b,
        tm=tm, n_ctx_tok=n_ctx_tok, ctx_len=S, lat_len=DS)

    def ext_state(C, n_, m_):
        nb_ = jnp.broadcast_to(n_[..., None], n_.shape + (DH,))
        cext = jnp.concatenate([C, nb_], axis=-1).reshape(-1, ng, DH, 2 * DH)
        mext = jnp.broadcast_to(m_.reshape(-1, ng, 1), (C.shape[0], ng, LANES))
        return cext, mext

    z_c, z_m = ext_state(jnp.zeros((B, N_DIRS, N_HEADS, DH, DH), F32), jnp.zeros((B, N_DIRS, N_HEADS, DH), F32),
                         jnp.zeros((B, N_DIRS, N_HEADS), F32))
    hf_c, hb_c, c_fin, m_fin = _mlstm(q, k, v, gates, gates_t, z_c, z_m, tok0=0, nseq=B, seq_len=S, tc=S,
                                      write_state=True)
    l_c, l_m = ext_state(state_C, state_n, state_m)
    hf_l, hb_l = _mlstm(q, k, v, gates, gates_t, l_c, l_m, tok0=n_ctx_tok, nseq=DB, seq_len=DS, tc=tc_lat,
                        write_state=False)

    x1, h2t, scores = _mix(x_ctx, x_lat, conv, hf_c, hf_l, hb_c, hb_l, og, mod, g_mlstm, w_out, g_norm2,
                           peer_wq, peer_k1, peer_k2, tm=tm, n_ctx_tok=n_ctx_tok, lat_len=DS)
    r2, p2, cc, p1 = _select(scores)
    u_pk = _pack_table(peer_u, transpose=False)
    vt_pk = _pack_table(peer_v, transpose=True)
    y_ctx, y_lat = _peer(h2t, u_pk, vt_pk, r2, p2, cc, p1, x1, mod, g_final,
                         tb=tb, n_ctx_tok=n_ctx_tok, lat_len=DS)

    y_prompt = y_ctx.reshape(B, S, D_MODEL)
    y_sample = y_lat.reshape(DB, DS, D_MODEL)
    new_C = c_fin[..., :DH].reshape(B, 1, N_DIRS, N_HEADS, DH, DH)
    new_n = c_fin[..., DH].reshape(B, 1, N_DIRS, N_HEADS, DH)
    new_m = m_fin[..., 0].reshape(B, 1, N_DIRS, N_HEADS)
    return y_prompt, y_sample, new_C, new_n, new_m


def kernel(x_prompt, x_sample, state_C, state_n, state_m, c, c_ctx, w_ada, b_ada, g_norm1, w_in, b_igate, b_fgate,
           conv_w, conv_b, g_mlstm, w_out, g_norm2, peer_wq, peer_k1, peer_k2, peer_u, peer_v, g_final):
    return _layer(x_prompt, x_sample, state_C[:, 0], state_n[:, 0], state_m[:, 0], c, c_ctx, w_ada[0], b_ada[0],
                  g_norm1[0], w_in[0], b_igate[0], b_fgate[0], conv_w[0], conv_b[0], g_mlstm[0], w_out[0],
                  g_norm2[0], peer_wq[0], peer_k1[0], peer_k2[0], peer_u[0], peer_v[0], g_final,
                  tm=512, tc_lat=256, tb=512)
```

```python
import functools

import jax
import jax.numpy as jnp
from jax import lax
from jax.experimental import pallas as pl
from jax.experimental.pallas import tpu as pltpu

F32 = jnp.float32
BF16 = jnp.bfloat16

D_MODEL = 1024
CONV_W = 512
N_HEADS = 4
DH = 128
MLSTM_W = N_HEADS * DH
N_DIRS = 2
GRID_W = 64
N_GROUPS = 7
GROUP_W = 512
GATE_COL0 = N_GROUPS * GROUP_W
N_GATES = 2 * N_DIRS * N_HEADS
PEER_HEADS = 8
NKEYS = 128
TOPK = 16
N_EXPERTS = NKEYS * NKEYS
EPS = 1e-6
LANES = 128
SUBLANES = 8
NEG_INF = float("-inf")

VMEM_LIMIT = 56 * 1024 * 1024


def _cparams(sem):
    return pltpu.CompilerParams(dimension_semantics=sem, vmem_limit_bytes=VMEM_LIMIT)


def _mod_kernel(cv_ref, w_ref, b_ref, o_ref):
    cv = cv_ref[...]
    s = cv * (1.0 / (1.0 + jnp.exp(-cv)))
    o_ref[...] = jnp.dot(s, w_ref[...], preferred_element_type=F32) + b_ref[...]


def _modulation(cv8, w_ada, b_ada):
    n = w_ada.shape[1]
    tn = 1024
    return pl.pallas_call(
        _mod_kernel,
        grid=(n // tn,),
        in_specs=[pl.BlockSpec((8, D_MODEL), lambda j: (0, 0)),
                  pl.BlockSpec((D_MODEL, tn), lambda j: (0, j)),
                  pl.BlockSpec((1, tn), lambda j: (0, j))],
        out_specs=pl.BlockSpec((8, tn), lambda j: (0, j)),
        out_shape=jax.ShapeDtypeStruct((8, n), F32),
        compiler_params=_cparams(("arbitrary",)),
        name="modulation",
    )(cv8, w_ada, b_ada.reshape(1, n))


def _mod_row(i, tm, n_ctx_tok, lat_len):
    n_ctx_tiles = n_ctx_tok // tm
    tiles_per_seq = lat_len // tm
    is_ctx = i < n_ctx_tiles
    row = jnp.where(is_ctx, 0, 1 + (i - n_ctx_tiles) // tiles_per_seq)
    return is_ctx, row


def _group_specs(tm, width, n_ctx_tok):
    nct = n_ctx_tok // tm
    return [pl.BlockSpec((tm, width), lambda i: (jnp.minimum(i, nct - 1), 0)),
            pl.BlockSpec((tm, width), lambda i: (jnp.maximum(i - nct, 0), 0))]


def _rms(x):
    return x * lax.rsqrt(jnp.mean(x * x, axis=-1, keepdims=True) + EPS)


def _log_sigmoid(z):
    return jnp.minimum(z, 0.0) - jnp.log(1.0 + jnp.exp(-jnp.abs(z)))


def _inproj_kernel(xc_ref, xl_ref, mod_ref, g1_ref, win_ref, wg_ref, wgt_ref, bg_ref, bgt_ref, cw_ref, cb_ref,
                   conv_ref, q_ref, k_ref, v_ref, o_ref, gates_ref, gatest_ref,
                   *, tm, n_ctx_tok, ctx_len, lat_len):
    i = pl.program_id(0)
    is_ctx, row = _mod_row(i, tm, n_ctx_tok, lat_len)
    sh1 = mod_ref[pl.ds(row, 1), 0:D_MODEL]
    sc1 = mod_ref[pl.ds(row, 1), D_MODEL:2 * D_MODEL]
    x = jnp.where(is_ctx, xc_ref[...], xl_ref[...])
    h = _rms(x) * g1_ref[...] * (1.0 + sc1) + sh1
    hb = h.astype(BF16)

    def grp(g):
        return jnp.dot(hb, win_ref[:, g * GROUP_W:(g + 1) * GROUP_W], preferred_element_type=F32)

    u = grp(1) * grp(2)
    pm = jnp.where(is_ctx, ctx_len - 1, GRID_W - 1)
    pos = lax.broadcasted_iota(jnp.int32, (tm, 1), 0) & pm
    u_prev = jnp.where(pos == 0, 0.0, pltpu.roll(u, 1, axis=0))
    u_next = jnp.where(pos == pm, 0.0, pltpu.roll(u, tm - 1, axis=0))
    y = u_prev * cw_ref[0:1, :] + u * cw_ref[1:2, :] + u_next * cw_ref[2:3, :] + cb_ref[...]
    conv_ref[...] = (grp(0) * y).astype(BF16)

    q_ref[...] = (grp(3) * (DH ** -0.5)).astype(BF16)
    k_ref[...] = grp(4).astype(BF16)
    v_ref[...] = grp(5).astype(BF16)
    o_ref[...] = grp(6)

    z = jnp.dot(hb, wg_ref[...], preferred_element_type=F32) + bg_ref[...]
    lane = lax.broadcasted_iota(jnp.int32, z.shape, 1)
    gates_ref[...] = jnp.where(lane >= N_DIRS * N_HEADS, _log_sigmoid(z), z)
    zt = lax.dot_general(wgt_ref[...], hb, (((1,), (1,)), ((), ())), preferred_element_type=F32) + bgt_ref[...]
    sub = lax.broadcasted_iota(jnp.int32, zt.shape, 0)
    gatest_ref[...] = jnp.where(sub >= N_DIRS * N_HEADS, _log_sigmoid(zt), zt)


def _in_proj(x_ctx, x_lat, mod, g1, w_in, b_igate, b_fgate, conv_w, conv_b, *, tm, n_ctx_tok, ctx_len, lat_len):
    T = x_ctx.shape[0] + x_lat.shape[0]
    win = w_in[:, :GATE_COL0].astype(BF16)
    wg = jnp.zeros((D_MODEL, LANES), F32).at[:, :N_GATES].set(w_in[:, GATE_COL0:]).astype(BF16)
    bgate = jnp.zeros((1, LANES), F32).at[0, :N_GATES].set(
        jnp.concatenate([b_igate.reshape(-1), b_fgate.reshape(-1)]))
    tok = lambda w: pl.BlockSpec((tm, w), lambda i: (i, 0))
    full = lambda a: pl.BlockSpec(a.shape, lambda i: (0,) * a.ndim)
    args = (x_ctx, x_lat, mod, g1.reshape(1, -1), win, wg, wg.T, bgate, bgate.T, conv_w, conv_b.reshape(1, -1))
    return pl.pallas_call(
        functools.partial(_inproj_kernel, tm=tm, n_ctx_tok=n_ctx_tok, ctx_len=ctx_len, lat_len=lat_len),
        grid=(T // tm,),
        in_specs=_group_specs(tm, D_MODEL, n_ctx_tok) + [full(a) for a in args[2:]],
        out_specs=[tok(CONV_W), tok(MLSTM_W), tok(MLSTM_W), tok(MLSTM_W), tok(MLSTM_W), tok(LANES),
                   pl.BlockSpec((LANES, tm), lambda i: (0, i))],
        out_shape=[jax.ShapeDtypeStruct((T, CONV_W), BF16),
                   jax.ShapeDtypeStruct((T, MLSTM_W), BF16),
                   jax.ShapeDtypeStruct((T, MLSTM_W), BF16),
                   jax.ShapeDtypeStruct((T, MLSTM_W), BF16),
                   jax.ShapeDtypeStruct((T, MLSTM_W), F32),
                   jax.ShapeDtypeStruct((T, LANES), F32),
                   jax.ShapeDtypeStruct((LANES, T), F32)],
        compiler_params=_cparams(("parallel",)),
        name="in_proj",
    )(*args)


def _split_bf16(x):
    hi = x.astype(BF16)
    return hi, (x - hi.astype(F32)).astype(BF16)


def _mlstm_kernel(qf_ref, kf_ref, vf_ref, gf_ref, gtf_ref, qb_ref, kb_ref, vb_ref, gb_ref, gtb_ref,
                  c0_ref, m0_ref, hf_ref, hb_ref, *rest, tc, write_state):
    if write_state:
        cout_ref, mout_ref, c_scr, m_scr = rest
    else:
        c_scr, m_scr = rest
    c = pl.program_id(1)

    @pl.when(c == 0)
    def _():
        c_scr[...] = c0_ref[...]
        m_scr[...] = m0_ref[...]

    r_io = lax.broadcasted_iota(jnp.int32, (tc, tc), 0)
    c_io = lax.broadcasted_iota(jnp.int32, (tc, tc), 1)
    ones = jnp.ones((tc, DH), BF16)
    wide = lambda x, width: jnp.concatenate([x] * (width // DH), axis=1)

    for d in range(N_DIRS):
        q_ref, k_ref, v_ref, g_ref, gt_ref, h_ref = (
            (qf_ref, kf_ref, vf_ref, gf_ref, gtf_ref, hf_ref) if d == 0 else
            (qb_ref, kb_ref, vb_ref, gb_ref, gtb_ref, hb_ref))
        causal = (c_io <= r_io) if d == 0 else (c_io >= r_io)
        last = tc - 1 if d == 0 else 0
        tri = causal.astype(BF16)
        tri_t = ((r_io <= c_io) if d == 0 else (r_io >= c_io)).astype(BF16)
        g = g_ref[...]
        gt = gt_ref[...]
        g_hi, g_lo = _split_bf16(g)
        gt_hi, gt_lo = _split_bf16(gt)
        bcum = (jnp.dot(tri, g_hi, preferred_element_type=F32) + jnp.dot(tri, g_lo, preferred_element_type=F32))
        bcum_t = (jnp.dot(gt_hi, tri_t, preferred_element_type=F32)
                  + jnp.dot(gt_lo, tri_t, preferred_element_type=F32))
        for hd in range(N_HEADS):
            gi = d * N_HEADS + hd
            gfi = N_DIRS * N_HEADS + gi
            sl = slice(hd * DH, (hd + 1) * DH)
            q = q_ref[:, sl]
            k = k_ref[:, sl]
            v = v_ref[:, sl]
            ig_c = jnp.broadcast_to(g[:, gi:gi + 1], (tc, DH))
            b_c = jnp.broadcast_to(bcum[:, gfi:gfi + 1], (tc, DH))
            ig_r = gt[gi:gi + 1, :]
            b_r = bcum_t[gfi:gfi + 1, :]
            m_prev = m_scr[gi:gi + 1, :]
            dmat = jnp.where(causal, wide(b_c, tc) - b_r + ig_r, NEG_INF)
            inter = b_c + m_prev
            m_t = jnp.maximum(inter, jnp.max(dmat, axis=-1, keepdims=True))
            s = (lax.dot_general(q, k, (((1,), (1,)), ((), ())), preferred_element_type=F32)
                 * jnp.exp(dmat - wide(m_t, tc)))
            a = jnp.exp(inter - m_t)
            cext = c_scr[gi]
            vext = jnp.concatenate([v, ones], axis=1)
            nd = (wide(a, 2 * DH) * jnp.dot(q, cext.astype(BF16), preferred_element_type=F32)
                  + jnp.dot(s.astype(BF16), vext, preferred_element_type=F32))
            num = nd[:, :DH]
            den = nd[:, DH:]
            h_ref[:, sl] = num / jnp.maximum(jnp.abs(den), jnp.exp(-m_t))
            m_new = m_t[last:last + 1, :]
            b_last = b_c[last:last + 1, :]
            w_c = jnp.exp(b_last - b_c + ig_c - m_new)
            decay = jnp.exp(b_last + m_prev - m_new)
            kw_t = (k.astype(F32) * w_c).T.astype(BF16)
            c_scr[gi] = wide(decay, 2 * DH) * cext + jnp.dot(kw_t, vext, preferred_element_type=F32)
            m_scr[gi:gi + 1, :] = m_new

    if write_state:
        @pl.when(c == pl.num_programs(1) - 1)
        def _():
            cout_ref[...] = c_scr[...]
            mout_ref[...] = m_scr[...]


def _mlstm(q, k, v, gates, gates_t, c0ext, m0, *, tok0, nseq, seq_len, tc, write_state):
    nc = seq_len // tc
    blk0 = tok0 // tc
    fwd = lambda b, c: b * nc + c
    bwd = lambda b, c: b * nc + nc - 1 - c
    tokspec = lambda w, f, o: pl.BlockSpec((tc, w), lambda b, c: (o + f(b, c), 0))
    gtspec = lambda f: pl.BlockSpec((LANES, tc), lambda b, c: (0, blk0 + f(b, c)))
    ng = N_DIRS * N_HEADS
    in_specs = [tokspec(MLSTM_W, fwd, blk0)] * 3 + [tokspec(LANES, fwd, blk0), gtspec(fwd)] \
        + [tokspec(MLSTM_W, bwd, blk0)] * 3 + [tokspec(LANES, bwd, blk0), gtspec(bwd)] \
        + [pl.BlockSpec((None, ng, DH, 2 * DH), lambda b, c: (b, 0, 0, 0)),
           pl.BlockSpec((None, ng, LANES), lambda b, c: (b, 0, 0))]
    out_specs = [tokspec(MLSTM_W, fwd, 0), tokspec(MLSTM_W, bwd, 0)]
    out_shape = [jax.ShapeDtypeStruct((nseq * seq_len, MLSTM_W), F32)] * 2
    if write_state:
        out_specs += [pl.BlockSpec((None, ng, DH, 2 * DH), lambda b, c: (b, 0, 0, 0)),
                      pl.BlockSpec((None, ng, LANES), lambda b, c: (b, 0, 0))]
        out_shape += [jax.ShapeDtypeStruct((nseq, ng, DH, 2 * DH), F32),
                      jax.ShapeDtypeStruct((nseq, ng, LANES), F32)]
    return pl.pallas_call(
        functools.partial(_mlstm_kernel, tc=tc, write_state=write_state),
        grid=(nseq, nc),
        in_specs=in_specs,
        out_specs=out_specs,
        out_shape=out_shape,
        scratch_shapes=[pltpu.VMEM((ng, DH, 2 * DH), F32), pltpu.VMEM((ng, LANES), F32)],
        compiler_params=_cparams(("arbitrary", "arbitrary")),
        name="mlstm_ctx" if write_state else "mlstm_lat",
    )(q, k, v, gates, gates_t, q, k, v, gates, gates_t, c0ext, m0)


def _mix_kernel(xc_ref, xl_ref, conv_ref, hfc_ref, hfl_ref, hbc_ref, hbl_ref, og_ref, mod_ref, gm_ref, wout_ref,
                g2_ref, wst_ref, x1_ref, h2t_ref, sc_ref, *, tm, n_ctx_tok, lat_len):
    i = pl.program_id(0)
    is_ctx, row = _mod_row(i, tm, n_ctx_tok, lat_len)
    mrow = lambda j: mod_ref[pl.ds(row, 1), j * D_MODEL:(j + 1) * D_MODEL]
    hs = jnp.where(is_ctx, hfc_ref[...] + hbc_ref[...], hfl_ref[...] + hbl_ref[...])
    hn = jnp.concatenate([_rms(hs[:, hd * DH:(hd + 1) * DH]) for hd in range(N_HEADS)], axis=1)
    og = og_ref[...]
    ml = (hn * gm_ref[...] * (1.0 / (1.0 + jnp.exp(-og)))).astype(BF16)
    mix = (jnp.dot(conv_ref[...], wout_ref[0:CONV_W, :], preferred_element_type=F32)
           + jnp.dot(ml, wout_ref[CONV_W:, :], preferred_element_type=F32))
    x1 = jnp.where(is_ctx, xc_ref[...], xl_ref[...]) + mrow(2) * mix
    x1_ref[...] = x1
    h2f = _rms(x1) * g2_ref[...] * (1.0 + mrow(4)) + mrow(3)
    h2t = h2f.T.astype(BF16)
    h2t_ref[...] = h2t
    sc = jnp.dot(pltpu.bitcast(wst_ref[...], BF16), h2t, preferred_element_type=F32)
    sc_ref[...] = sc.reshape(2 * PEER_HEADS, NKEYS, tm)


def _fold_keys_kernel(k_ref, wq_ref, o_ref):
    w = lax.dot_general(k_ref[...], wq_ref[...], (((1,), (1,)), ((), ())), preferred_element_type=F32)
    o_ref[...] = pltpu.bitcast(w.astype(BF16), jnp.uint32)


def _fold_keys(wq, k1, k2):
    n = 2 * PEER_HEADS
    keys = jnp.stack([k1, k2])
    return pl.pallas_call(
        _fold_keys_kernel,
        grid=(n,),
        in_specs=[pl.BlockSpec((None, NKEYS, NKEYS), lambda j: (j % 2, 0, 0)),
                  pl.BlockSpec((D_MODEL, NKEYS), lambda j: (0, j))],
        out_specs=pl.BlockSpec((NKEYS // 2, D_MODEL), lambda j: (j, 0)),
        out_shape=jax.ShapeDtypeStruct((n * NKEYS // 2, D_MODEL), jnp.uint32),
        compiler_params=_cparams(("parallel",)),
        name="fold_keys",
    )(keys, wq)


def _mix(x_ctx, x_lat, conv, hf_ctx, hf_lat, hb_ctx, hb_lat, og, mod, g_mlstm, w_out, g2, wq, k1, k2,
         *, tm, n_ctx_tok, lat_len):
    T = x_ctx.shape[0] + x_lat.shape[0]
    tok = lambda w: pl.BlockSpec((tm, w), lambda i: (i, 0))
    full = lambda a: pl.BlockSpec(a.shape, lambda i: (0,) * a.ndim)
    consts = (mod, g_mlstm.reshape(1, -1), w_out.astype(BF16), g2.reshape(1, -1), _fold_keys(wq, k1, k2))
    return pl.pallas_call(
        functools.partial(_mix_kernel, tm=tm, n_ctx_tok=n_ctx_tok, lat_len=lat_len),
        grid=(T // tm,),
        in_specs=_group_specs(tm, D_MODEL, n_ctx_tok) + [tok(CONV_W)] + _group_specs(tm, MLSTM_W, n_ctx_tok) * 2
        + [tok(MLSTM_W)] + [full(a) for a in consts],
        out_specs=[tok(D_MODEL), pl.BlockSpec((D_MODEL, tm), lambda i: (0, i)),
                   pl.BlockSpec((2 * PEER_HEADS, NKEYS, tm), lambda i: (0, 0, i))],
        out_shape=[jax.ShapeDtypeStruct((T, D_MODEL), F32),
                   jax.ShapeDtypeStruct((D_MODEL, T), BF16),
                   jax.ShapeDtypeStruct((2 * PEER_HEADS, NKEYS, T), F32)],
        compiler_params=_cparams(("parallel",)),
        name="mix",
    )(x_ctx, x_lat, conv, hf_ctx, hf_lat, hb_ctx, hb_lat, og, *consts)


def _sort16_pairs():
    def merge(lo, hi, r):
        step = r * 2
        if step < hi - lo:
            yield from merge(lo, hi, step)
            yield from merge(lo + r, hi, step)
            yield from ((i, i + r) for i in range(lo + r, hi - r, step))
        else:
            yield (lo, lo + r)

    def sort(lo, hi):
        if hi - lo >= 1:
            mid = lo + (hi - lo) // 2
            yield from sort(lo, mid)
            yield from sort(mid + 1, hi)
            yield from merge(lo, hi, 1)

    return tuple(sort(0, TOPK - 1))


_SORT16 = _sort16_pairs()
_CAND_PAIRS = tuple((r1, r2) for r1 in range(TOPK) for r2 in range(TOPK) if (r1 + 1) * (r2 + 1) <= TOPK)


def _cmpx(x, i, j):
    a, b = x[i], x[j]
    if b is None:
        return
    if a is None:
        x[i], x[j] = b, None
        return
    x[i], x[j] = jnp.maximum(a, b), jnp.minimum(a, b)


def _sort16(x):
    x = list(x)
    for i, j in _SORT16:
        _cmpx(x, i, j)
    return x


def _merge_top16(a, b):
    x = []
    for k in range(TOPK):
        p, q = a[k], b[TOPK - 1 - k]
        x.append(q if p is None else p if q is None else jnp.maximum(p, q))
    d = TOPK // 2
    while d:
        for k in range(TOPK):
            if not k & d:
                _cmpx(x, k, k + d)
        d //= 2
    return x


def _top16_values(groups):
    groups = [_sort16(g) for g in groups]
    while len(groups) > 1:
        groups = [_merge_top16(groups[k], groups[k + 1]) for k in range(0, len(groups), 2)]
    return groups[0]


def _count_gt(vs, s):
    g = jnp.where(vs[0] > s, 1.0, 0.0)
    for r in range(1, len(vs)):
        g = jnp.where(vs[r] > s, float(r + 1), g)
    return g


def _xpose8(x):
    x = list(x)
    sub = lax.broadcasted_iota(jnp.int32, (SUBLANES, LANES), 0)
    for d in (4, 2, 1):
        keep = (sub & d) == 0
        for a in range(8):
            if not a & d:
                xa, xb = x[a], x[a + d]
                x[a] = jnp.where(keep, xa, pltpu.roll(xb, d, axis=0))
                x[a + d] = jnp.where(keep, pltpu.roll(xa, 8 - d, axis=0), xb)
    return x


def _select_kernel(sc_ref, r2_ref, p2_ref, c_ref, p1_ref, s_ref, t_scr):
    one = lambda m: jnp.where(m, 1.0, 0.0)
    lane_tile = lambda r: slice(r * LANES, (r + 1) * LANES)
    for h in range(2):
        for kg in range(NKEYS // 8):
            tiles = _xpose8([sc_ref[h, kg * 8:(kg + 1) * 8, lane_tile(r)] for r in range(8)])
            for k in range(8):
                s_ref[h, kg * 8 + k] = tiles[k]
    v = [_top16_values([[s_ref[h, g * TOPK + k] for k in range(TOPK)] for g in range(NKEYS // TOPK)])
         for h in range(2)]
    v1, v2 = v

    tie = jnp.zeros((SUBLANES, LANES), F32)
    for h in range(2):
        n_ge = one(s_ref[h, 0] >= v[h][TOPK - 1])
        for j in range(1, NKEYS):
            n_ge = n_ge + one(s_ref[h, j] >= v[h][TOPK - 1])
        tie = jnp.maximum(tie, one(n_ge > float(TOPK)))
        for r in range(TOPK - 1):
            tie = jnp.maximum(tie, one(v[h][r] == v[h][r + 1]))
    has_tie = jnp.max(tie) > 0.0

    cand = {p: v1[p[0]] + v2[p[1]] for p in _CAND_PAIRS}
    rest = [cand[p] for p in _CAND_PAIRS if p[0] > 0]
    rest += [None] * (-len(rest) % TOPK)
    groups = [[cand[(0, r2)] for r2 in range(TOPK)]] + [rest[k:k + TOPK] for k in range(0, len(rest), TOPK)]
    while len(groups) & (len(groups) - 1):
        groups.append([None] * TOPK)
    w = _top16_values(groups)
    tau = w[TOPK - 1]
    n_gt = sum(one(cand[p] > tau) for p in _CAND_PAIRS)
    need = float(TOPK) - n_gt
    eq_seen = jnp.zeros((SUBLANES, LANES), F32)
    counts = [jnp.zeros((SUBLANES, LANES), F32) for _ in range(TOPK)]
    for p in _CAND_PAIRS:
        eq = cand[p] == tau
        counts[p[0]] = counts[p[0]] + one((cand[p] > tau) | (eq & (eq_seen < need)))
        eq_seen = eq_seen + one(eq)
    z = jnp.ones((SUBLANES, LANES), F32)
    for r in range(1, TOPK):
        z = z + jnp.exp(w[r] - w[0])
    inv_z = 0.5 / z

    def tables(with_ties):
        for kg in range(NKEYS // 8):
            tabs = [[], [], [], []]
            for j in range(kg * 8, (kg + 1) * 8):
                s1 = s_ref[0, j]
                s2 = s_ref[1, j]
                if with_ties:
                    rank1 = _count_gt(v1, s1) + t_scr[0, j]
                    ckey = jnp.zeros((SUBLANES, LANES), F32)
                    for r in range(TOPK):
                        ckey = jnp.where(rank1 == float(r), counts[r], ckey)
                    rank2 = _count_gt(v2, s2) + t_scr[1, j]
                else:
                    ckey = jnp.where(v1[0] > s1, counts[1], counts[0])
                    for r in range(1, TOPK - 1):
                        ckey = jnp.where(v1[r] > s1, counts[r + 1], ckey)
                    ckey = jnp.where(v1[TOPK - 1] > s1, 0.0, ckey)
                    rank2 = _count_gt(v2, s2)
                tabs[0].append(rank2)
                tabs[1].append(jnp.exp(s2 - v2[0]))
                tabs[2].append(ckey)
                tabs[3].append(jnp.exp(s1 - v1[0]) * inv_z)
            for ref, tab in zip((r2_ref, p2_ref, c_ref, p1_ref), tabs):
                for r, tile in enumerate(_xpose8(tab)):
                    ref[kg * 8:(kg + 1) * 8, lane_tile(r)] = tile

    @pl.when(has_tie)
    def _():
        t_scr[...] = jnp.zeros_like(t_scr)

        def lower_equal(jp, carry):
            for h in range(2):
                sv = s_ref[h, jp]
                for j in range(1, NKEYS):
                    inc = jnp.where(jp < j, 1.0, 0.0)
                    t_scr[h, j] = t_scr[h, j] + jnp.where(s_ref[h, j] == sv, inc, 0.0)
            return carry
        lax.fori_loop(0, NKEYS - 1, lower_equal, 0)
        tables(True)

    @pl.when(jnp.logical_not(has_tie))
    def _():
        tables(False)


def _select(scores):
    T = scores.shape[-1]
    tt = 8 * LANES
    tab = pl.BlockSpec((None, NKEYS, tt), lambda i, hd: (hd, 0, i))
    return pl.pallas_call(
        _select_kernel,
        grid=(T // tt, PEER_HEADS),
        in_specs=[pl.BlockSpec((2, NKEYS, tt), lambda i, hd: (hd, 0, i))],
        out_specs=[tab] * 4,
        out_shape=[jax.ShapeDtypeStruct((PEER_HEADS, NKEYS, T), F32)] * 4,
        scratch_shapes=[pltpu.VMEM((2, NKEYS, 8, LANES), F32), pltpu.VMEM((2, NKEYS, 8, LANES), F32)],
        compiler_params=_cparams(("parallel", "parallel")),
        name="select",
    )(scores)


_SQRT_HALF = 0.7071067811865476
_JROWS = 16
_PEER_NB = 8 * NKEYS
_PEER_STEPS = N_EXPERTS // _PEER_NB


def _zero_after(x):
    u = pltpu.bitcast(x, jnp.uint32)
    u = lax.shift_right_logical(lax.shift_right_logical(u, jnp.uint32(16)), jnp.uint32(16))
    return pltpu.bitcast(u, BF16)


def _peer_kernel(h2t_ref, u_ref, vt_ref, r2_ref, p2_ref, c_ref, p1_ref, x1_ref, mod_ref, gf_ref, yc_ref, yl_ref,
                 a_scr, w_scr, acc_scr, tab_scr, *, tb, n_ctx_tok, lat_len):
    L = pl.program_id(0)
    n_lg = tb // LANES
    vpu_blk = L - 1
    acc_blk = L - 2

    @pl.when(L == 0)
    def _():
        a_scr[...] = jnp.zeros_like(a_scr)
        w_scr[...] = jnp.zeros_like(w_scr)
        acc_scr[...] = jnp.zeros_like(acc_scr)

    @pl.when((L == 0) | (vpu_blk % _PEER_STEPS == 0))
    def _():
        for jg in range(NKEYS // _JROWS):
            rows = slice(jg * _JROWS, (jg + 1) * _JROWS)
            for lg in range(n_lg):
                lanes = slice(lg * LANES, (lg + 1) * LANES)
                for hd in range(PEER_HEADS):
                    tab_scr[jg, lg, hd * _JROWS:(hd + 1) * _JROWS, :] = r2_ref[hd, rows, lanes].astype(BF16)
                    tab_scr[jg, lg, (PEER_HEADS + hd) * _JROWS:(PEER_HEADS + hd + 1) * _JROWS, :] = \
                        p2_ref[hd, rows, lanes].astype(BF16)

    @pl.when((acc_blk >= 0) & (acc_blk % _PEER_STEPS == 0))
    def _():
        acc_scr[...] = jnp.zeros_like(acc_scr)

    def stages(cur):
        prv = 1 - cur
        a_scr[cur] = jnp.dot(pltpu.bitcast(u_ref[...], BF16), h2t_ref[...], preferred_element_type=F32)
        acc_scr[...] += jnp.dot(pltpu.bitcast(vt_ref[...], BF16), w_scr[cur], preferred_element_type=F32)
        prev_w = None
        for lg in range(n_lg):
            lanes = slice(lg * LANES, (lg + 1) * LANES)
            for i8 in range(_PEER_NB // NKEYS):
                bcast = lambda ref, hd: jnp.broadcast_to(ref[hd, i8:i8 + 1, lanes], (_JROWS, LANES)).astype(BF16)
                crow = [bcast(c_ref, hd) for hd in range(PEER_HEADS)]
                p1row = [bcast(p1_ref, hd) for hd in range(PEER_HEADS)]
                for jg in range(NKEYS // _JROWS):
                    rows = slice(jg * _JROWS, (jg + 1) * _JROWS)
                    arow = slice(i8 * NKEYS + jg * _JROWS, i8 * NKEYS + (jg + 1) * _JROWS)
                    a = a_scr[prv, arow, lanes]
                    terms = []
                    for hd in range(PEER_HEADS):
                        r2 = tab_scr[jg, lg, hd * _JROWS:(hd + 1) * _JROWS, :]
                        p2 = tab_scr[jg, lg, (PEER_HEADS + hd) * _JROWS:(PEER_HEADS + hd + 1) * _JROWS, :]
                        terms.append(jnp.where(r2 < crow[hd], p2, jnp.zeros_like(p2)) * p1row[hd])
                    if prev_w is not None:
                        terms[0] = terms[0] + _zero_after(prev_w)
                    while len(terms) > 1:
                        terms = [terms[k] + terms[k + 1] for k in range(0, len(terms), 2)]
                    g = terms[0]
                    gelu = a * (1.0 + lax.erf(a * _SQRT_HALF))
                    w = g * gelu.astype(BF16)
                    w_scr[prv, arow, lanes] = w
                    prev_w = w

    for slot in range(2):
        pl.when(L % 2 == slot)(functools.partial(stages, slot))

    @pl.when((acc_blk >= 0) & (acc_blk % _PEER_STEPS == _PEER_STEPS - 1))
    def _():
        is_ctx, row = _mod_row(acc_blk // _PEER_STEPS, tb, n_ctx_tok, lat_len)
        ga2 = mod_ref[pl.ds(row, 1), 5 * D_MODEL:6 * D_MODEL]
        x2 = x1_ref[...] + ga2 * acc_scr[...].T
        y = _rms(x2) * gf_ref[...]

        @pl.when(is_ctx)
        def _():
            yc_ref[...] = y

        @pl.when(jnp.logical_not(is_ctx))
        def _():
            yl_ref[...] = y


def _pack_kernel(x_ref, o_ref, *, transpose):
    x = x_ref[...]
    if transpose:
        x = x.T
    o_ref[...] = pltpu.bitcast(x.astype(BF16), jnp.uint32)


def _pack_table(x, *, transpose):
    n, d = x.shape
    blk = _PEER_NB
    out_rows, out_cols = (d // 2, n) if transpose else (n // 2, d)
    return pl.pallas_call(
        functools.partial(_pack_kernel, transpose=transpose),
        grid=(n // blk,),
        in_specs=[pl.BlockSpec((blk, d), lambda j: (j, 0))],
        out_specs=(pl.BlockSpec((d // 2, blk), lambda j: (0, j)) if transpose
                   else pl.BlockSpec((blk // 2, d), lambda j: (j, 0))),
        out_shape=jax.ShapeDtypeStruct((out_rows, out_cols), jnp.uint32),
        compiler_params=_cparams(("arbitrary",)),
        name="pack_vt" if transpose else "pack_u",
    )(x)


def _peer(h2t, u_bf, vt_bf, r2, p2, cc, p1, x1, mod, g_final, *, tb, n_ctx_tok, lat_len):
    T = h2t.shape[1]
    nb = _PEER_NB
    n_tiles = T // tb
    nct = n_ctx_tok // tb
    n_blocks = n_tiles * _PEER_STEPS
    blk = lambda L, lag: jnp.clip(L - lag, 0, n_blocks - 1)
    tile = lambda L, lag: blk(L, lag) // _PEER_STEPS
    step = lambda L, lag: blk(L, lag) % _PEER_STEPS
    tab = pl.BlockSpec((PEER_HEADS, NKEYS, tb), lambda L: (0, 0, tile(L, 1)))
    rowtab = pl.BlockSpec((PEER_HEADS, nb // NKEYS, tb), lambda L: (0, step(L, 1), tile(L, 1)))
    return pl.pallas_call(
        functools.partial(_peer_kernel, tb=tb, n_ctx_tok=n_ctx_tok, lat_len=lat_len),
        grid=(n_blocks + 2,),
        in_specs=[pl.BlockSpec((D_MODEL, tb), lambda L: (0, tile(L, 0))),
                  pl.BlockSpec((nb // 2, D_MODEL), lambda L: (step(L, 0), 0)),
                  pl.BlockSpec((D_MODEL // 2, nb), lambda L: (0, step(L, 2))),
                  tab, tab, rowtab, rowtab,
                  pl.BlockSpec((tb, D_MODEL), lambda L: (tile(L, 2), 0)),
                  pl.BlockSpec(mod.shape, lambda L: (0, 0)),
                  pl.BlockSpec((1, D_MODEL), lambda L: (0, 0))],
        out_specs=[pl.BlockSpec((tb, D_MODEL), lambda L: (jnp.minimum(tile(L, 2), nct - 1), 0)),
                   pl.BlockSpec((tb, D_MODEL), lambda L: (jnp.maximum(tile(L, 2) - nct, 0), 0))],
        out_shape=[jax.ShapeDtypeStruct((n_ctx_tok, D_MODEL), F32),
                   jax.ShapeDtypeStruct((T - n_ctx_tok, D_MODEL), F32)],
        scratch_shapes=[pltpu.VMEM((2, nb, tb), F32), pltpu.VMEM((2, nb, tb), BF16), pltpu.VMEM((D_MODEL, tb), F32),
                        pltpu.VMEM((NKEYS // _JROWS, tb // LANES, 2 * PEER_HEADS * _JROWS, LANES), BF16)],
        compiler_params=_cparams(("arbitrary",)),
        name="peer",
    )(h2t, u_bf, vt_bf, r2, p2, cc, p1, x1, mod, g_final.reshape(1, -1))


def _layer(x_prompt, x_sample, state_C, state_n, state_m, c, c_ctx, w_ada, b_ada, g_norm1, w_in, b_igate, b_fgate,
           conv_w, conv_b, g_mlstm, w_out, g_norm2, peer_wq, peer_k1, peer_k2, peer_u, peer_v, g_final,
           *, tm, tc_lat, tb):
    B, S, _ = x_prompt.shape
    DB, DS, _ = x_sample.shape
    n_ctx_tok = B * S
    ng = N_DIRS * N_HEADS
    x_ctx = x_prompt.reshape(n_ctx_tok, D_MODEL)
    x_lat = x_sample.reshape(DB * DS, D_MODEL)
    cv8 = jnp.zeros((8, D_MODEL), F32).at[0].set(c_ctx).at[1:1 + DB].set(c)
    mod = _modulation(cv8, w_ada, b_ada)

    conv, q, k, v, og, gates, gates_t = _in_proj(
        x_ctx, x_lat, mod, g_norm1, w_in, b_igate, b_fgate, conv_w, conv_b,
        tm=tm, n_ctx_tok=n_ctx_tok, ctx_len=S, lat_len=DS)

    def ext_state(C, n_, m_):
        nb_ = jnp.broadcast_to(n_[..., None], n_.shape + (DH,))
        cext = jnp.concatenate([C, nb_], axis=-1).reshape(-1, ng, DH, 2 * DH)
        mext = jnp.broadcast_to(m_.reshape(-1, ng, 1), (C.shape[0], ng, LANES))
        return cext, mext

    z_c, z_m = ext_state(jnp.zeros((B, N_DIRS, N_HEADS, DH, DH), F32), jnp.zeros((B, N_DIRS, N_HEADS, DH), F32),
                         jnp.zeros((B, N_DIRS, N_HEADS), F32))
    hf_c, hb_c, c_fin, m_fin = _mlstm(q, k, v, gates, gates_t, z_c, z_m, tok0=0, nseq=B, seq_len=S, tc=S,
                                      write_state=True)
    l_c, l_m = ext_state(state_C, state_n, state_m)
    hf_l, hb_l = _mlstm(q, k, v, gates, gates_t, l_c, l_m, tok0=n_ctx_tok, nseq=DB, seq_len=DS, tc=tc_lat,
                        write_state=False)

    x1, h2t, scores = _mix(x_ctx, x_lat, conv, hf_c, hf_l, hb_c, hb_l, og, mod, g_mlstm, w_out, g_norm2,
                           peer_wq, peer_k1, peer_k2, tm=tm, n_ctx_tok=n_ctx_tok, lat_len=DS)
    r2, p2, cc, p1 = _select(scores)
    u_pk = _pack_table(peer_u, transpose=False)
    vt_pk = _pack_table(peer_v, transpose=True)
    y_ctx, y_lat = _peer(h2t, u_pk, vt_pk, r2, p2, cc, p1, x1, mod, g_final,
                         tb=tb, n_ctx_tok=n_ctx_tok, lat_len=DS)

    y_prompt = y_ctx.reshape(B, S, D_MODEL)
    y_sample = y_lat.reshape(DB, DS, D_MODEL)
    new_C = c_fin[..., :DH].reshape(B, 1, N_DIRS, N_HEADS, DH, DH)
    new_n = c_fin[..., DH].reshape(B, 1, N_DIRS, N_HEADS, DH)
    new_m = m_fin[..., 0].reshape(B, 1, N_DIRS, N_HEADS)
    return y_prompt, y_sample, new_C, new_n, new_m


def kernel(x_prompt, x_sample, state_C, state_n, state_m, c, c_ctx, w_ada, b_ada, g_norm1, w_in, b_igate, b_fgate,
           conv_w, conv_b, g_mlstm, w_out, g_norm2, peer_wq, peer_k1, peer_k2, peer_u, peer_v, g_final):
    return _layer(x_prompt, x_sample, state_C[:, 0], state_n[:, 0], state_m[:, 0], c, c_ctx, w_ada[0], b_ada[0],
                  g_norm1[0], w_in[0], b_igate[0], b_fgate[0], conv_w[0], conv_b[0], g_mlstm[0], w_out[0],
                  g_norm2[0], peer_wq[0], peer_k1[0], peer_k2[0], peer_u[0], peer_v[0], g_final,
                  tm=512, tc_lat=256, tb=512)
```
